```python
import jax, jax.numpy as jnp
from jax import lax
import numpy as np

D_MODEL = 1024
BATCH = 2
SEQ = 8192
DEPTH = 2

N_META = 16
N_A_LAYERS = DEPTH // 2
N_B_LAYERS = DEPTH - N_A_LAYERS
M_HEADS = 4
M_DV = D_MODEL // M_HEADS
M_DK = M_DV // 2
M_CHUNK = 64
GATE_CAP = 15.0
M_PROJ = 2 * M_HEADS * M_DK + 2 * D_MODEL + 2 * M_HEADS
F_HEADS = 16
F_HD = D_MODEL // F_HEADS
Q_BLOCK = 128
N_GROUPS = 4
EXP_PER_GROUP = 4
N_EXPERTS = N_GROUPS * EXP_PER_GROUP
TOP_K_IN_GROUP = 2
D_EXPERT = D_MODEL // 4
EPS = 1e-6
NEG = -1e30

kernel_name = "mlstm_fox_yoco_hmoe_trunk"


def rmsnorm(x, g):
    xf = x.astype(jnp.float32)
    y = xf * lax.rsqrt(jnp.mean(xf * xf, axis=-1, keepdims=True) + EPS)
    return (y * g.astype(jnp.float32)).astype(x.dtype)


def _heads(t, n, d):
    b, l = t.shape[:2]
    return t.reshape(b, l, n, d).transpose(0, 2, 1, 3)


def mlstm_mixer(h, w_in, b_gate, g_out, w_out):
    B, L, _ = h.shape
    f32 = jnp.float32
    qk_w = M_HEADS * M_DK
    proj = h @ w_in
    q, k, v, o, gates = jnp.split(proj, [qk_w, 2 * qk_w, 2 * qk_w + D_MODEL, 2 * qk_w + 2 * D_MODEL], axis=-1)
    gates = gates.astype(f32) + b_gate.astype(f32)
    gates = GATE_CAP * jnp.tanh(gates / GATE_CAP)
    i_pre = gates[..., :M_HEADS].transpose(0, 2, 1)
    log_f = jax.nn.log_sigmoid(gates[..., M_HEADS:]).transpose(0, 2, 1)
    pad = M_CHUNK - N_META
    def pad_seq(t, val=0.0):
        return jnp.pad(t, [(0, 0), (0, 0), (pad, 0)] + [(0, 0)] * (t.ndim - 3), constant_values=val)
    q = pad_seq(_heads(q, M_HEADS, M_DK).astype(f32) * (M_DK ** -0.5))
    k = pad_seq(_heads(k, M_HEADS, M_DK).astype(f32))
    v = pad_seq(_heads(v, M_HEADS, M_DV).astype(f32))
    li = pad_seq(i_pre, NEG)
    lf = pad_seq(log_f, 0.0)
    Lp = L + pad
    NC = Lp // M_CHUNK
    def chunk(t):
        return t.reshape(t.shape[:2] + (NC, M_CHUNK) + t.shape[3:])
    q, k, v, li, lf = chunk(q), chunk(k), chunk(v), chunk(li), chunk(lf)
    b = jnp.cumsum(lf, axis=-1)
    g_tot = b[..., -1]
    a = g_tot[..., None] - b + li
    m_loc = jnp.max(a, axis=-1)
    wts = jnp.exp(a - m_loc[..., None])
    kv_loc = jnp.einsum('bhncd,bhnce->bhnde', k * wts[..., None], v)
    n_loc = jnp.einsum('bhncd,bhnc->bhnd', k, wts)

    def step(carry, inp):
        c_prev, n_prev, m_prev = carry
        g, m_l, kv_l, n_l = inp
        m_new = jnp.maximum(g + m_prev, m_l)
        s_old = jnp.exp(g + m_prev - m_new)
        s_loc = jnp.exp(m_l - m_new)
        c_new = s_old[..., None, None] * c_prev + s_loc[..., None, None] * kv_l
        n_new = s_old[..., None] * n_prev + s_loc[..., None] * n_l
        return (c_new, n_new, m_new), (c_prev, n_prev, m_prev)

    init = (jnp.zeros((B, M_HEADS, M_DK, M_DV), f32), jnp.zeros((B, M_HEADS, M_DK), f32),
            jnp.zeros((B, M_HEADS), f32))
    mv = lambda t: jnp.moveaxis(t, 2, 0)
    _, (c_st, n_st, m_st) = lax.scan(step, init, (mv(g_tot), mv(m_loc), mv(kv_loc), mv(n_loc)))
    c_st = jnp.moveaxis(c_st, 0, 2)
    n_st = jnp.moveaxis(n_st, 0, 2)
    m_st = jnp.moveaxis(m_st, 0, 2)
    causal = jnp.tril(jnp.ones((M_CHUNK, M_CHUNK), bool))
    dmat = jnp.where(causal, b[..., :, None] - b[..., None, :] + li[..., None, :], NEG)
    bq = b + m_st[..., None]
    m_t = jnp.maximum(jnp.max(dmat, axis=-1), bq)
    s_mat = jnp.einsum('bhncd,bhnsd->bhncs', q, k) * jnp.exp(dmat - m_t[..., None])
    inter = jnp.exp(bq - m_t)
    num = (jnp.einsum('bhncs,bhnse->bhnce', s_mat, v)
           + inter[..., None] * jnp.einsum('bhncd,bhnde->bhnce', q, c_st))
    den = s_mat.sum(-1) + inter * jnp.einsum('bhncd,bhnd->bhnc', q, n_st)
    hout = num / jnp.maximum(jnp.abs(den), jnp.exp(-m_t))[..., None]
    hout = hout.reshape(B, M_HEADS, Lp, M_DV)[:, :, pad:].transpose(0, 2, 1, 3)
    hout = rmsnorm(hout.astype(h.dtype), g_out)
    hout = hout * jax.nn.sigmoid(o.astype(f32)).astype(h.dtype).reshape(B, L, M_HEADS, M_DV)
    return hout.reshape(B, L, D_MODEL) @ w_out


def shared_kv(s, g_kv, w_kv, b_fg, g_k):
    h = rmsnorm(s, g_kv)
    k, v, f_pre = jnp.split(h @ w_kv, [D_MODEL, 2 * D_MODEL], axis=-1)
    k = rmsnorm(_heads(k, F_HEADS, F_HD), g_k)
    v = _heads(v, F_HEADS, F_HD)
    log_f = jax.nn.log_sigmoid(f_pre.astype(jnp.float32) + b_fg.astype(jnp.float32))
    F = jnp.cumsum(log_f, axis=1).transpose(0, 2, 1)
    return k, v, F


def fox_mixer(h, w_q, g_q, w_o, k, v, F):
    B, S, _ = h.shape
    L = k.shape[2]
    q = rmsnorm(_heads(h @ w_q, F_HEADS, F_HD), g_q) * (F_HD ** -0.5)
    nb = S // Q_BLOCK
    qb = q.reshape(B, F_HEADS, nb, Q_BLOCK, F_HD).transpose(2, 0, 1, 3, 4)
    fqb = F[:, :, N_META:].reshape(B, F_HEADS, nb, Q_BLOCK).transpose(2, 0, 1, 3)
    qpos = (N_META + jnp.arange(S, dtype=jnp.int32)).reshape(nb, Q_BLOCK)
    kpos = jnp.arange(L, dtype=jnp.int32)

    def block(args):
        qi, fi, pi = args
        s = (jnp.einsum('bhqd,bhkd->bhqk', qi, k).astype(jnp.float32)
             + fi[..., None] - F[:, :, None, :])
        s = jnp.where(kpos[None, :] <= pi[:, None], s, NEG)
        p = jax.nn.softmax(s, axis=-1).astype(v.dtype)
        return jnp.einsum('bhqk,bhkd->bhqd', p, v)

    o = lax.map(block, (qb, fqb, qpos))
    o = o.transpose(1, 0, 3, 2, 4).reshape(B, S, D_MODEL)
    return o @ w_o


def hier_moe(h, w_group, b_group, w_router, b_router, w_gate, w_up, w_down):
    shp = h.shape
    x = h.reshape(-1, D_MODEL)
    T = x.shape[0]
    f32 = jnp.float32
    gl = (x @ w_group).astype(f32) + b_group.astype(f32)
    top_g = jnp.argmax(gl, axis=-1)
    p_g = jnp.take_along_axis(jax.nn.softmax(gl, axis=-1), top_g[:, None], axis=-1)
    el = ((x @ w_router).astype(f32) + b_router.astype(f32)).reshape(T, N_GROUPS, EXP_PER_GROUP)
    el_g = jnp.take_along_axis(el, top_g[:, None, None], axis=1)[:, 0]
    top_v, top_i = lax.top_k(el_g, TOP_K_IN_GROUP)
    p_e = jax.nn.softmax(top_v, axis=-1) * p_g
    eidx = top_g[:, None] * EXP_PER_GROUP + top_i
    gates = (jax.nn.one_hot(eidx, N_EXPERTS, dtype=f32) * p_e[..., None]).sum(1).astype(x.dtype)
    y = jnp.zeros_like(x)
    for grp in range(N_GROUPS):
        sl = slice(grp * EXP_PER_GROUP, (grp + 1) * EXP_PER_GROUP)
        hg = jnp.einsum('td,edf->tef', x, w_gate[sl])
        hu = jnp.einsum('td,edf->tef', x, w_up[sl])
        act = jax.nn.silu(hg) * hu * gates[:, sl, None]
        y = y + jnp.einsum('tef,efd->td', act, w_down[sl])
    return y.reshape(shp)


def setup_inputs(seed: int = 0) -> dict:
    key = jax.random.key(seed)
    ks = jax.random.split(key, 24)
    f32 = jnp.float32
    def nrm(k, shape, fan):
        return jax.random.normal(k, shape, f32) * (fan ** -0.5)
    def gain(k, shape):
        return 1.0 + 0.02 * jax.random.normal(k, shape, f32)
    NA, NB = N_A_LAYERS, N_B_LAYERS
    kb1, kb2 = jax.random.split(ks[4])
    m_b_gate = jnp.concatenate([
        -2.0 + 0.5 * jax.random.normal(kb1, (NA, M_HEADS), f32),
        jax.random.uniform(kb2, (NA, M_HEADS), f32, 3.0, 6.0)], axis=-1)
    return {
        "x": jax.random.normal(ks[0], (BATCH, SEQ, D_MODEL), f32),
        "meta_tokens": jax.random.normal(ks[1], (N_META, D_MODEL), f32),
        "norm_mix": gain(ks[2], (DEPTH, D_MODEL)),
        "norm_ffn": gain(ks[3], (DEPTH, D_MODEL)),
        "m_w_in": nrm(ks[5], (NA, D_MODEL, M_PROJ), D_MODEL),
        "m_b_gate": m_b_gate,
        "m_g_out": gain(ks[6], (NA, M_HEADS, M_DV)),
        "m_w_out": nrm(ks[7], (NA, D_MODEL, D_MODEL), D_MODEL),
        "kv_norm": gain(ks[8], (D_MODEL,)),
        "kv_w": nrm(ks[9], (D_MODEL, 2 * D_MODEL + F_HEADS), D_MODEL),
        "kv_b_f": jax.random.uniform(ks[10], (F_HEADS,), f32, 1.0, 6.0),
        "k_norm": gain(ks[11], (F_HD,)),
        "f_w_q": nrm(ks[12], (NB, D_MODEL, D_MODEL), D_MODEL),
        "f_q_norm": gain(ks[13], (NB, F_HD)),
        "f_w_o": nrm(ks[14], (NB, D_MODEL, D_MODEL), D_MODEL),
        "moe_w_group": nrm(ks[15], (DEPTH, D_MODEL, N_GROUPS), D_MODEL),
        "moe_b_group": 0.01 * jax.random.normal(ks[16], (DEPTH, N_GROUPS), f32),
        "moe_w_router": nrm(ks[17], (DEPTH, D_MODEL, N_EXPERTS), D_MODEL),
        "moe_b_router": 0.01 * jax.random.normal(ks[18], (DEPTH, N_EXPERTS), f32),
        "moe_w_gate": nrm(ks[19], (DEPTH, N_EXPERTS, D_MODEL, D_EXPERT), D_MODEL),
        "moe_w_up": nrm(ks[20], (DEPTH, N_EXPERTS, D_MODEL, D_EXPERT), D_MODEL),
        "moe_w_down": nrm(ks[21], (DEPTH, N_EXPERTS, D_EXPERT, D_MODEL), D_EXPERT),
    }


def reference(x, meta_tokens, norm_mix, norm_ffn, m_w_in, m_b_gate, m_g_out, m_w_out,
              kv_norm, kv_w, kv_b_f, k_norm, f_w_q, f_q_norm, f_w_o,
              moe_w_group, moe_b_group, moe_w_router, moe_b_router,
              moe_w_gate, moe_w_up, moe_w_down):
    B = x.shape[0]
    meta = jnp.broadcast_to(meta_tokens[None].astype(x.dtype), (B, N_META, D_MODEL))
    h = jnp.concatenate([meta, x], axis=1)
    k_sh = v_sh = F_sh = None
    for l in range(DEPTH):
        if l < N_A_LAYERS:
            h = h + mlstm_mixer(rmsnorm(h, norm_mix[l]), m_w_in[l], m_b_gate[l], m_g_out[l], m_w_out[l])
        else:
            if l == N_A_LAYERS:
                k_sh, v_sh, F_sh = shared_kv(h, kv_norm, kv_w, kv_b_f, k_norm)
                h = h[:, N_META:]
            j = l - N_A_LAYERS
            h = h + fox_mixer(rmsnorm(h, norm_mix[l]), f_w_q[j], f_q_norm[j], f_w_o[j], k_sh, v_sh, F_sh)
        h = h + hier_moe(rmsnorm(h, norm_ffn[l]), moe_w_group[l], moe_b_group[l], moe_w_router[l],
                         moe_b_router[l], moe_w_gate[l], moe_w_up[l], moe_w_down[l])
    return h
```

```python
import functools
import math

import jax
import jax.numpy as jnp
from jax import lax
from jax.experimental import pallas as pl
from jax.experimental.pallas import tpu as pltpu

D_MODEL = 1024
N_META = 16
M_HEADS = 4
M_DV = D_MODEL // M_HEADS
M_DK = M_DV // 2
M_CHUNK = 64
GATE_CAP = 15.0
F_HEADS = 16
F_HD = D_MODEL // F_HEADS
N_GROUPS = 4
EXP_PER_GROUP = 4
N_EXPERTS = N_GROUPS * EXP_PER_GROUP
D_EXPERT = D_MODEL // 4
EPS = 1e-6
NEG = -1e30

LANES = 128
ROW_TILE = 512
REAL0 = 512
PAD_FRONT = REAL0 - N_META
MLSTM_ROWS = 256
ATT_TQ = 512
ATT_TK = 512
LOG2E = 1.4426950408889634
VMEM_LIMIT = 56 * 1024 * 1024

_HI = lax.Precision.HIGHEST
_f32 = jnp.float32
_bf16 = jnp.bfloat16


def _dot(a, b):
    return jnp.dot(a, b, preferred_element_type=_f32)


def _dot_hi(a, b):
    return jnp.dot(a, b, preferred_element_type=_f32, precision=_HI)


def _dot_nt(a, b):
    return lax.dot_general(a, b, (((1,), (1,)), ((), ())), preferred_element_type=_f32)


def _rmsnorm(x, g):
    return x * lax.rsqrt(jnp.mean(x * x, axis=-1, keepdims=True) + EPS) * g


def _log_sigmoid(x):
    return jnp.minimum(x, 0.0) - jnp.log1p(jnp.exp(-jnp.abs(x)))


def _head_pair_rmsnorm(x, g2):
    lane = lax.broadcasted_iota(jnp.int32, x.shape, 1)
    lo = lane < F_HD
    sq = x * x
    s_lo = jnp.sum(jnp.where(lo, sq, 0.0), axis=1, keepdims=True)
    s_hi = jnp.sum(jnp.where(lo, 0.0, sq), axis=1, keepdims=True)
    ms = jnp.where(lo, s_lo, s_hi) * (1.0 / F_HD)
    return x * lax.rsqrt(ms + EPS) * g2


def _params(*sem):
    return pltpu.CompilerParams(dimension_semantics=sem, vmem_limit_bytes=VMEM_LIMIT)


def _const_spec(shape):
    return pl.BlockSpec(shape, lambda *_: (0,) * len(shape))


def _mlstm_in_kernel(blocks_per_batch, h_ref, g_ref, wq_ref, wkt_ref, wv_ref, wo_ref, wgate_ref, bgate_ref,
                     q_ref, kt_ref, v_ref, o_ref, u_ref, b_ref):
    i = pl.program_id(0)
    x = h_ref[...]
    xn = _rmsnorm(x, g_ref[...])
    xb = xn.astype(_bf16)
    q_ref[...] = (_dot(xb, wq_ref[...]) * (M_DK ** -0.5)).astype(_bf16)
    kt_ref[...] = _dot_nt(wkt_ref[...], xb).astype(_bf16)
    v_ref[...] = _dot(xb, wv_ref[...]).astype(_bf16)
    o_ref[...] = jax.nn.sigmoid(_dot(xb, wo_ref[...])).astype(_bf16)
    gates = _dot_hi(xn, wgate_ref[...]) + bgate_ref[...]
    gates = GATE_CAP * jnp.tanh(gates * (1.0 / GATE_CAP))
    li = gates[:, :LANES]
    lf = _log_sigmoid(gates[:, LANES:])
    rows = x.shape[0]
    row_in_batch = (i % blocks_per_batch) * rows + lax.broadcasted_iota(jnp.int32, (rows, LANES), 0)
    is_pad = row_in_batch < PAD_FRONT
    li = jnp.where(is_pad, NEG, li)
    lf = jnp.where(is_pad, 0.0, lf)
    r = lax.broadcasted_iota(jnp.int32, (rows, rows), 0)
    c = lax.broadcasted_iota(jnp.int32, (rows, rows), 1)
    chunk_tril = jnp.where((r // M_CHUNK == c // M_CHUNK) & (c <= r), 1.0, 0.0).astype(_f32)
    b = _dot_hi(chunk_tril, lf)
    u = li - b
    u_ref[...] = u[:, :8]
    b_ref[...] = b[:, :8]


def _mlstm_in_proj(hp, gain, w_in, b_gate, batch_rows):
    T = hp.shape[0]
    qk_w = M_HEADS * M_DK
    wq = w_in[:, :qk_w].astype(_bf16)
    wkt = w_in[:, qk_w:2 * qk_w].T.astype(_bf16)
    wv = w_in[:, 2 * qk_w:2 * qk_w + D_MODEL].astype(_bf16)
    wo = w_in[:, 2 * qk_w + D_MODEL:2 * qk_w + 2 * D_MODEL].astype(_bf16)
    wg = w_in[:, 2 * qk_w + 2 * D_MODEL:]
    wgate = jnp.zeros((D_MODEL, 2 * LANES), _f32)
    wgate = wgate.at[:, :M_HEADS].set(wg[:, :M_HEADS]).at[:, LANES:LANES + M_HEADS].set(wg[:, M_HEADS:])
    bgate = jnp.zeros((1, 2 * LANES), _f32)
    bgate = bgate.at[0, :M_HEADS].set(b_gate[:M_HEADS]).at[0, LANES:LANES + M_HEADS].set(b_gate[M_HEADS:])
    tm = ROW_TILE
    row = lambda n: pl.BlockSpec((tm, n), lambda i: (i, 0))
    return pl.pallas_call(
        functools.partial(_mlstm_in_kernel, batch_rows // tm),
        grid=(T // tm,),
        in_specs=[row(D_MODEL), _const_spec((1, D_MODEL)), _const_spec((D_MODEL, qk_w)),
                  _const_spec((qk_w, D_MODEL)), _const_spec((D_MODEL, D_MODEL)), _const_spec((D_MODEL, D_MODEL)),
                  _const_spec((D_MODEL, 2 * LANES)), _const_spec((1, 2 * LANES))],
        out_specs=[row(qk_w), pl.BlockSpec((qk_w, tm), lambda i: (0, i)), row(D_MODEL), row(D_MODEL),
                   row(8), row(8)],
        out_shape=[jax.ShapeDtypeStruct((T, qk_w), _bf16), jax.ShapeDtypeStruct((qk_w, T), _bf16),
                   jax.ShapeDtypeStruct((T, D_MODEL), _bf16), jax.ShapeDtypeStruct((T, D_MODEL), _bf16),
                   jax.ShapeDtypeStruct((T, 8), _f32), jax.ShapeDtypeStruct((T, 8), _f32)],
        compiler_params=_params("parallel"),
        name="mlstm_in_proj",
    )(hp, gain.reshape(1, D_MODEL), wq, wkt, wv, wo, wgate, bgate)


def _mlstm_core_kernel(q_ref, kt_ref, v_ref, o_ref, ut_ref, b_ref, gout_ref, out_ref, state_ref, m_ref):
    C = M_CHUNK
    n_chunks = q_ref.shape[0] // C

    @pl.when(pl.program_id(1) == 0)
    def _():
        state_ref[...] = jnp.zeros_like(state_ref)
        m_ref[...] = jnp.zeros_like(m_ref)

    r = lax.broadcasted_iota(jnp.int32, (C, C), 0)
    c = lax.broadcasted_iota(jnp.int32, (C, C), 1)
    causal = c <= r
    ones_col = jnp.where(lax.broadcasted_iota(jnp.int32, (C, LANES), 1) == 0, 1.0, 0.0).astype(_bf16)

    for h in range(M_HEADS):
        S = state_ref[h]
        m = m_ref[h][0:1, 0:1]
        for ci in range(n_chunks):
            rows = slice(ci * C, (ci + 1) * C)
            qc = q_ref[rows, h * M_DK:(h + 1) * M_DK]
            ktc = kt_ref[h * M_DK:(h + 1) * M_DK, rows]
            vc = v_ref[rows, h * M_DV:(h + 1) * M_DV]
            u_row = ut_ref[h:h + 1, rows]
            b_col = b_ref[rows, h:h + 1]
            v_aug = jnp.concatenate([vc, ones_col], axis=1)
            cu_col = jnp.max(jnp.where(causal, u_row, NEG), axis=1, keepdims=True)
            M = jnp.maximum(cu_col, m)
            E = jnp.where(causal, jnp.exp(jnp.where(causal, u_row - M, 0.0)), 0.0)
            inter = jnp.exp(m - M)
            s_mat = (_dot(qc, ktc) * E).astype(_bf16)
            tot = _dot(s_mat, v_aug) + inter * _dot(qc, S.astype(_bf16))
            num = tot[:, :M_DV]
            den = tot[:, M_DV:M_DV + 1]
            hout = num / jnp.maximum(jnp.abs(den), jnp.exp(-(b_col + M)))
            y = hout * lax.rsqrt(jnp.mean(hout * hout, axis=-1, keepdims=True) + EPS)
            y = y * gout_ref[:, h * M_DV:(h + 1) * M_DV] * o_ref[rows, h * M_DV:(h + 1) * M_DV].astype(_f32)
            out_ref[rows, h * M_DV:(h + 1) * M_DV] = y.astype(out_ref.dtype)
            g_tot = b_col[C - 1:C, :]
            cu_last = cu_col[C - 1:C, :]
            wts_row = jnp.exp(u_row - cu_last)
            m_loc = g_tot + cu_last
            m_new = jnp.maximum(g_tot + m, m_loc)
            ktw = (ktc.astype(_f32) * wts_row).astype(_bf16)
            kv = _dot(ktw, v_aug)
            S = jnp.exp(g_tot + m - m_new) * S + jnp.exp(m_loc - m_new) * kv
            m = m_new
        state_ref[h] = S
        m_ref[h] = jnp.broadcast_to(m, m_ref.shape[1:])


def _mlstm_core(q, kt, v, o, ut, bcol, g_out, batch, batch_rows):
    R = MLSTM_ROWS
    spb = batch_rows // R
    row = lambda n: pl.BlockSpec((R, n), lambda b, s: (b * spb + s, 0))
    col = lambda n: pl.BlockSpec((n, R), lambda b, s: (0, b * spb + s))
    qk_w = M_HEADS * M_DK
    return pl.pallas_call(
        _mlstm_core_kernel,
        grid=(batch, spb),
        in_specs=[row(qk_w), col(qk_w), row(D_MODEL), row(D_MODEL), col(8), row(8), _const_spec((1, D_MODEL))],
        out_specs=row(D_MODEL),
        out_shape=jax.ShapeDtypeStruct((batch * batch_rows, D_MODEL), _bf16),
        scratch_shapes=[pltpu.VMEM((M_HEADS, M_DK, M_DV + LANES), _f32), pltpu.VMEM((M_HEADS, 8, LANES), _f32)],
        compiler_params=_params("parallel", "arbitrary"),
        name="mlstm_core",
    )(q, kt, v, o, ut, bcol, g_out.reshape(1, D_MODEL))


def _linear_residual_kernel(y_ref, w_ref, res_ref, out_ref):
    out_ref[...] = res_ref[...] + _dot(y_ref[...], w_ref[...])


def _linear_residual(y, w, res):
    T = y.shape[0]
    tm = ROW_TILE
    row = pl.BlockSpec((tm, D_MODEL), lambda i: (i, 0))
    return pl.pallas_call(
        _linear_residual_kernel,
        grid=(T // tm,),
        in_specs=[row, _const_spec((D_MODEL, D_MODEL)), row],
        out_specs=row,
        out_shape=jax.ShapeDtypeStruct((T, D_MODEL), _f32),
        compiler_params=_params("parallel"),
        name="linear_residual",
    )(y, w.astype(_bf16), res)


def _route(logits):
    lane = lax.broadcasted_iota(jnp.int32, logits.shape, 1)
    is_g = lane < N_GROUPS
    gl = jnp.where(is_g, logits, -jnp.inf)
    gmax = jnp.max(gl, axis=1, keepdims=True)
    top_g = jnp.min(jnp.where(gl == gmax, lane, LANES), axis=1, keepdims=True)
    p_g = 1.0 / jnp.sum(jnp.where(is_g, jnp.exp(gl - gmax), 0.0), axis=1, keepdims=True)
    e_lane = lane - N_GROUPS
    in_group = (e_lane >= top_g * EXP_PER_GROUP) & (e_lane < (top_g + 1) * EXP_PER_GROUP)
    el = jnp.where(in_group, logits, -jnp.inf)
    v1 = jnp.max(el, axis=1, keepdims=True)
    i1 = jnp.min(jnp.where(el == v1, lane, LANES), axis=1, keepdims=True)
    el2 = jnp.where(lane == i1, -jnp.inf, el)
    v2 = jnp.max(el2, axis=1, keepdims=True)
    i2 = jnp.min(jnp.where(el2 == v2, lane, LANES), axis=1, keepdims=True)
    e2 = jnp.exp(v2 - v1)
    p1 = p_g / (1.0 + e2)
    p2 = p_g * e2 / (1.0 + e2)
    return jnp.where(lane == i1, p1, jnp.where(lane == i2, p2, 0.0))


def _moe_kernel(h_ref, g_ref, wr_ref, br_ref, wg_ref, wu_ref, wd_ref, out_ref, xn_ref, gate_ref):
    grp = pl.program_id(1)

    @pl.when(grp == 0)
    def _():
        x = h_ref[...]
        xn = _rmsnorm(x, g_ref[...])
        xn_ref[...] = xn.astype(_bf16)
        gate_ref[...] = _route(_dot_hi(xn, wr_ref[...]) + br_ref[...])
        out_ref[...] = x

    xb = xn_ref[...]
    gates = gate_ref[...]
    lane = lax.broadcasted_iota(jnp.int32, gates.shape, 1)
    acc = jnp.zeros(out_ref.shape, _f32)
    for e in range(EXP_PER_GROUP):
        gate_e = jnp.sum(jnp.where(lane == N_GROUPS + grp * EXP_PER_GROUP + e, gates, 0.0), axis=1, keepdims=True)
        hg = _dot(xb, wg_ref[e])
        hu = _dot(xb, wu_ref[e])
        act = (hg * jax.nn.sigmoid(hg) * hu * gate_e).astype(_bf16)
        acc = acc + _dot(act, wd_ref[e])
    out_ref[...] += acc


def _moe(h, gain, w_group, b_group, w_router, b_router, w_gate, w_up, w_down):
    T = h.shape[0]
    tm = ROW_TILE
    wr = jnp.zeros((D_MODEL, LANES), _f32)
    wr = wr.at[:, :N_GROUPS].set(w_group).at[:, N_GROUPS:N_GROUPS + N_EXPERTS].set(w_router)
    br = jnp.zeros((1, LANES), _f32)
    br = br.at[0, :N_GROUPS].set(b_group).at[0, N_GROUPS:N_GROUPS + N_EXPERTS].set(b_router)
    row = pl.BlockSpec((tm, D_MODEL), lambda i, g: (i, 0))
    wspec_in = pl.BlockSpec((EXP_PER_GROUP, D_MODEL, D_EXPERT), lambda i, g: (g, 0, 0))
    wspec_out = pl.BlockSpec((EXP_PER_GROUP, D_EXPERT, D_MODEL), lambda i, g: (g, 0, 0))
    return pl.pallas_call(
        _moe_kernel,
        grid=(T // tm, N_GROUPS),
        in_specs=[row, _const_spec((1, D_MODEL)), _const_spec((D_MODEL, LANES)), _const_spec((1, LANES)),
                  wspec_in, wspec_in, wspec_out],
        out_specs=row,
        out_shape=jax.ShapeDtypeStruct((T, D_MODEL), _f32),
        scratch_shapes=[pltpu.VMEM((tm, D_MODEL), _bf16), pltpu.VMEM((tm, LANES), _f32)],
        compiler_params=_params("parallel", "arbitrary"),
        name="hier_moe",
    )(h, gain.reshape(1, D_MODEL), wr, br, w_gate.astype(_bf16), w_up.astype(_bf16), w_down.astype(_bf16))


def _kv_kernel(h_ref, g_ref, wk_ref, wv_ref, wf_ref, bf_ref, gk_ref, k_ref, v_ref, nf_ref, carry_ref):
    j = pl.program_id(1)

    @pl.when(j == 0)
    def _():
        carry_ref[...] = jnp.zeros_like(carry_ref)

    x = h_ref[...]
    rows = x.shape[0]
    xn = _rmsnorm(x, g_ref[...])
    xb = xn.astype(_bf16)
    kf = _dot(xb, wk_ref[...])
    for p in range(D_MODEL // LANES):
        sl = slice(p * LANES, (p + 1) * LANES)
        k_ref[:, sl] = _head_pair_rmsnorm(kf[:, sl], gk_ref[...]).astype(_bf16)
    v_ref[...] = _dot(xb, wv_ref[...]).astype(_bf16)
    lf = _log_sigmoid(_dot_hi(xn, wf_ref[...]) + bf_ref[...])
    row_in_batch = j * rows + lax.broadcasted_iota(jnp.int32, lf.shape, 0)
    lf = jnp.where(row_in_batch < PAD_FRONT, 0.0, lf)
    r = lax.broadcasted_iota(jnp.int32, (LANES, LANES), 0)
    c = lax.broadcasted_iota(jnp.int32, (LANES, LANES), 1)
    tril = jnp.where(c <= r, 1.0, 0.0).astype(_f32)
    carry = carry_ref[0:1, :]
    for sb in range(rows // LANES):
        sl = slice(sb * LANES, (sb + 1) * LANES)
        cs = _dot_hi(tril, lf[sl, :]) + carry
        nf_ref[sl, :] = (-LOG2E) * cs[:, :F_HEADS]
        carry = cs[LANES - 1:LANES, :]
    carry_ref[...] = jnp.broadcast_to(carry, carry_ref.shape)


def _shared_kv(h, gain, w_kv, b_f, g_k, batch, batch_rows):
    tm = ROW_TILE
    spb = batch_rows // tm
    wk = w_kv[:, :D_MODEL].astype(_bf16)
    wv = w_kv[:, D_MODEL:2 * D_MODEL].astype(_bf16)
    wf = jnp.zeros((D_MODEL, LANES), _f32).at[:, :F_HEADS].set(w_kv[:, 2 * D_MODEL:])
    bf = jnp.zeros((1, LANES), _f32).at[0, :F_HEADS].set(b_f)
    gk2 = jnp.tile(g_k, LANES // F_HD).reshape(1, LANES)
    row = lambda n: pl.BlockSpec((tm, n), lambda b, j: (b * spb + j, 0))
    T = batch * batch_rows
    return pl.pallas_call(
        _kv_kernel,
        grid=(batch, spb),
        in_specs=[row(D_MODEL), _const_spec((1, D_MODEL)), _const_spec((D_MODEL, D_MODEL)),
                  _const_spec((D_MODEL, D_MODEL)), _const_spec((D_MODEL, LANES)), _const_spec((1, LANES)),
                  _const_spec((1, LANES))],
        out_specs=[row(D_MODEL), row(D_MODEL), row(F_HEADS)],
        out_shape=[jax.ShapeDtypeStruct((T, D_MODEL), _bf16), jax.ShapeDtypeStruct((T, D_MODEL), _bf16),
                   jax.ShapeDtypeStruct((T, F_HEADS), _f32)],
        scratch_shapes=[pltpu.VMEM((8, LANES), _f32)],
        compiler_params=_params("parallel", "arbitrary"),
        name="shared_kv",
    )(h, gain.reshape(1, D_MODEL), wk, wv, wf, bf, gk2)


def _q_kernel(h_ref, g_ref, wq_ref, gq_ref, q_ref):
    xb = _rmsnorm(h_ref[...], g_ref[...]).astype(_bf16)
    qf = _dot(xb, wq_ref[...])
    for p in range(D_MODEL // LANES):
        sl = slice(p * LANES, (p + 1) * LANES)
        q_ref[:, sl] = (_head_pair_rmsnorm(qf[:, sl], gq_ref[...]) * (F_HD ** -0.5 * LOG2E)).astype(_bf16)


def _q_proj(h, gain, w_q, g_q):
    T = h.shape[0]
    tm = ROW_TILE
    row = pl.BlockSpec((tm, D_MODEL), lambda i: (i, 0))
    gq2 = jnp.tile(g_q, LANES // F_HD).reshape(1, LANES)
    return pl.pallas_call(
        _q_kernel,
        grid=(T // tm,),
        in_specs=[row, _const_spec((1, D_MODEL)), _const_spec((D_MODEL, D_MODEL)), _const_spec((1, LANES))],
        out_specs=row,
        out_shape=jax.ShapeDtypeStruct((T, D_MODEL), _bf16),
        compiler_params=_params("parallel"),
        name="attn_q_proj",
    )(h, gain.reshape(1, D_MODEL), w_q.astype(_bf16), gq2)


def _attn_kernel(q_ref, k_ref, v_ref, nf_ref, out_ref, m_ref, l_ref, acc_ref):
    iq = pl.program_id(2)
    tq, tk = ATT_TQ, ATT_TK
    q2 = q_ref[...]
    lane = lax.broadcasted_iota(jnp.int32, q2.shape, 1)
    zero = jnp.zeros_like(q2)
    q_heads = (jnp.where(lane < F_HD, q2, zero), jnp.where(lane < F_HD, zero, q2))
    m_ref[...] = jnp.full(m_ref.shape, NEG, _f32)
    l_ref[...] = jnp.zeros_like(l_ref)
    acc_ref[...] = jnp.zeros_like(acc_ref)
    q_row = REAL0 + iq * tq + lax.broadcasted_iota(jnp.int32, (tq, tk), 0)
    k_col = lax.broadcasted_iota(jnp.int32, (tq, tk), 1)

    def step(jk, masked):
        ks = pl.multiple_of(jk * tk, tk)
        kb = k_ref[pl.ds(ks, tk), :]
        vb = v_ref[pl.ds(ks, tk), :]
        for hh in range(2):
            s = _dot_nt(q_heads[hh], kb) + nf_ref[hh:hh + 1, pl.ds(ks, tk)]
            if masked:
                k_row = k_col + ks
                s = jnp.where((k_row <= q_row) & (k_row >= PAD_FRONT), s, NEG)
            m_prev = m_ref[hh]
            m_new = jnp.maximum(m_prev, jnp.max(s, axis=1, keepdims=True))
            alpha = jnp.exp2(m_prev - m_new)
            p = jnp.exp2(s - m_new)
            l_ref[hh] = alpha * l_ref[hh] + jnp.sum(p, axis=1, keepdims=True)
            acc_ref[hh] = alpha * acc_ref[hh] + _dot(p.astype(_bf16), vb)
            m_ref[hh] = m_new

    step(0, True)

    def body(jk, carry):
        step(jk, False)
        return carry

    lax.fori_loop(1, iq + 1, body, 0)
    step(iq + 1, True)
    o0 = acc_ref[0] / l_ref[0]
    o1 = acc_ref[1] / l_ref[1]
    out_ref[...] = jnp.where(lane < F_HD, o0, o1).astype(out_ref.dtype)


def _attention(q, k, v, nf, batch, seq, batch_rows):
    pairs = D_MODEL // LANES
    nq = seq // ATT_TQ
    return pl.pallas_call(
        _attn_kernel,
        grid=(batch, pairs, nq),
        in_specs=[pl.BlockSpec((None, ATT_TQ, LANES), lambda b, p, i: (b, i, p)),
                  pl.BlockSpec((None, batch_rows, LANES), lambda b, p, i: (b, 0, p)),
                  pl.BlockSpec((None, batch_rows, LANES), lambda b, p, i: (b, 0, p)),
                  pl.BlockSpec((None, None, 2, batch_rows), lambda b, p, i: (b, p, 0, 0))],
        out_specs=pl.BlockSpec((None, ATT_TQ, LANES), lambda b, p, i: (b, i, p)),
        out_shape=jax.ShapeDtypeStruct((batch, seq, D_MODEL), _bf16),
        scratch_shapes=[pltpu.VMEM((2, ATT_TQ, 1), _f32), pltpu.VMEM((2, ATT_TQ, 1), _f32),
                        pltpu.VMEM((2, ATT_TQ, LANES), _f32)],
        compiler_params=_params("parallel", "parallel", "arbitrary"),
        name="fox_attention",
    )(q, k, v, nf)


def kernel(x, meta_tokens, norm_mix, norm_ffn, m_w_in, m_b_gate, m_g_out, m_w_out, kv_norm, kv_w, kv_b_f, k_norm,
           f_w_q, f_q_norm, f_w_o, moe_w_group, moe_b_group, moe_w_router, moe_b_router, moe_w_gate, moe_w_up,
           moe_w_down):
    B, S, D = x.shape
    depth = norm_mix.shape[0]
    n_a = m_w_in.shape[0]
    LP = REAL0 + S
    assert D == D_MODEL and S % ATT_TQ == 0 and LP % ROW_TILE == 0 and meta_tokens.shape[0] == N_META
    meta = jnp.broadcast_to(meta_tokens[None].astype(x.dtype), (B, N_META, D))
    h = jnp.concatenate([jnp.zeros((B, PAD_FRONT, D), x.dtype), meta, x], axis=1).reshape(B * LP, D)
    k_sh = v_sh = nf_sh = None
    for l in range(depth):
        if l < n_a:
            q, kt, v, o, u, bcol = _mlstm_in_proj(h, norm_mix[l], m_w_in[l], m_b_gate[l], LP)
            y = _mlstm_core(q, kt, v, o, u.T, bcol, m_g_out[l].reshape(-1), B, LP)
            h = _linear_residual(y, m_w_out[l], h)
        else:
            if l == n_a:
                k_sh, v_sh, nf = _shared_kv(h, kv_norm, kv_w, kv_b_f, k_norm, B, LP)
                k_sh = k_sh.reshape(B, LP, D)
                v_sh = v_sh.reshape(B, LP, D)
                nf_sh = nf.reshape(B, LP, F_HEADS).transpose(0, 2, 1).reshape(B, F_HEADS // 2, 2, LP)
                h = h.reshape(B, LP, D)[:, REAL0:].reshape(B * S, D)
            j = l - n_a
            q = _q_proj(h, norm_mix[l], f_w_q[j], f_q_norm[j]).reshape(B, S, D)
            att = _attention(q, k_sh, v_sh, nf_sh, B, S, LP)
            h = _linear_residual(att.reshape(B * S, D), f_w_o[j], h)
        h = _moe(h, norm_ffn[l], moe_w_group[l], moe_b_group[l], moe_w_router[l], moe_b_router[l],
                 moe_w_gate[l], moe_w_up[l], moe_w_down[l])
    if depth == n_a:
        h = h.reshape(B, LP, D)[:, REAL0:]
    return h.reshape(B, S, D)
```

```python
import functools
import math

import jax
import jax.numpy as jnp
from jax import lax
from jax.experimental import pallas as pl
from jax.experimental.pallas import tpu as pltpu

D_MODEL = 1024
N_META = 16
M_HEADS = 4
M_DV = D_MODEL // M_HEADS
M_DK = M_DV // 2
M_CHUNK = 64
GATE_CAP = 15.0
F_HEADS = 16
F_HD = D_MODEL // F_HEADS
N_GROUPS = 4
EXP_PER_GROUP = 4
N_EXPERTS = N_GROUPS * EXP_PER_GROUP
D_EXPERT = D_MODEL // 4
EPS = 1e-6
NEG = -1e30

LANES = 128
ROW_TILE = 512
REAL0 = 512
PAD_FRONT = REAL0 - N_META
MLSTM_ROWS = 256
V_ROWS = F_HD + 16
ATT_TQ = 512
ATT_TK = 512
LOG2E = 1.4426950408889634
VMEM_LIMIT = 56 * 1024 * 1024

_HI = lax.Precision.HIGHEST
_f32 = jnp.float32
_bf16 = jnp.bfloat16


def _dot(a, b):
    return jnp.dot(a, b, preferred_element_type=_f32)


def _dot_hi(a, b):
    return jnp.dot(a, b, preferred_element_type=_f32, precision=_HI)


def _dot_nt(a, b):
    return lax.dot_general(a, b, (((1,), (1,)), ((), ())), preferred_element_type=_f32)


def _rmsnorm(x, g):
    return x * lax.rsqrt(jnp.mean(x * x, axis=-1, keepdims=True) + EPS) * g


def _log_sigmoid(x):
    return jnp.minimum(x, 0.0) - jnp.log1p(jnp.exp(-jnp.abs(x)))


def _head_pair_rmsnorm(x, g2):
    lane = lax.broadcasted_iota(jnp.int32, x.shape, 1)
    lo = lane < F_HD
    sq = x * x
    s_lo = jnp.sum(jnp.where(lo, sq, 0.0), axis=1, keepdims=True)
    s_hi = jnp.sum(jnp.where(lo, 0.0, sq), axis=1, keepdims=True)
    ms = jnp.where(lo, s_lo, s_hi) * (1.0 / F_HD)
    return x * lax.rsqrt(ms + EPS) * g2


def _params(*sem):
    return pltpu.CompilerParams(dimension_semantics=sem, vmem_limit_bytes=VMEM_LIMIT)


def _const_spec(shape):
    return pl.BlockSpec(shape, lambda *_: (0,) * len(shape))


def _mlstm_in_kernel(blocks_per_batch, h_ref, g_ref, wq_ref, wkt_ref, wv_ref, wo_ref, wgate_ref, bgate_ref,
                     q_ref, kt_ref, v_ref, o_ref, u_ref, b_ref):
    i = pl.program_id(0)
    x = h_ref[...]
    xn = _rmsnorm(x, g_ref[...])
    xb = xn.astype(_bf16)
    q_ref[...] = (_dot(xb, wq_ref[...]) * (M_DK ** -0.5)).astype(_bf16)
    kt_ref[...] = _dot_nt(wkt_ref[...], xb).astype(_bf16)
    v_ref[...] = _dot(xb, wv_ref[...]).astype(_bf16)
    o_ref[...] = jax.nn.sigmoid(_dot(xb, wo_ref[...])).astype(_bf16)
    gates = _dot_hi(xn, wgate_ref[...]) + bgate_ref[...]
    gates = GATE_CAP * jnp.tanh(gates * (1.0 / GATE_CAP))
    li = gates[:, :LANES]
    lf = _log_sigmoid(gates[:, LANES:])
    rows = x.shape[0]
    row_in_batch = (i % blocks_per_batch) * rows + lax.broadcasted_iota(jnp.int32, (rows, LANES), 0)
    is_pad = row_in_batch < PAD_FRONT
    li = jnp.where(is_pad, NEG, li)
    lf = jnp.where(is_pad, 0.0, lf)
    r = lax.broadcasted_iota(jnp.int32, (rows, rows), 0)
    c = lax.broadcasted_iota(jnp.int32, (rows, rows), 1)
    chunk_tril = jnp.where((r // M_CHUNK == c // M_CHUNK) & (c <= r), 1.0, 0.0).astype(_f32)
    b = _dot_hi(chunk_tril, lf)
    u = li - b
    u_ref[...] = u[:, :8]
    b_ref[...] = b[:, :8]


def _mlstm_in_proj(hp, gain, w_in, b_gate, batch_rows):
    T = hp.shape[0]
    qk_w = M_HEADS * M_DK
    wq = w_in[:, :qk_w].astype(_bf16)
    wkt = w_in[:, qk_w:2 * qk_w].T.astype(_bf16)
    wv = w_in[:, 2 * qk_w:2 * qk_w + D_MODEL].astype(_bf16)
    wo = w_in[:, 2 * qk_w + D_MODEL:2 * qk_w + 2 * D_MODEL].astype(_bf16)
    wg = w_in[:, 2 * qk_w + 2 * D_MODEL:]
    wgate = jnp.zeros((D_MODEL, 2 * LANES), _f32)
    wgate = wgate.at[:, :M_HEADS].set(wg[:, :M_HEADS]).at[:, LANES:LANES + M_HEADS].set(wg[:, M_HEADS:])
    bgate = jnp.zeros((1, 2 * LANES), _f32)
    bgate = bgate.at[0, :M_HEADS].set(b_gate[:M_HEADS]).at[0, LANES:LANES + M_HEADS].set(b_gate[M_HEADS:])
    tm = ROW_TILE
    row = lambda n: pl.BlockSpec((tm, n), lambda i: (i, 0))
    return pl.pallas_call(
        functools.partial(_mlstm_in_kernel, batch_rows // tm),
        grid=(T // tm,),
        in_specs=[row(D_MODEL), _const_spec((1, D_MODEL)), _const_spec((D_MODEL, qk_w)),
                  _const_spec((qk_w, D_MODEL)), _const_spec((D_MODEL, D_MODEL)), _const_spec((D_MODEL, D_MODEL)),
                  _const_spec((D_MODEL, 2 * LANES)), _const_spec((1, 2 * LANES))],
        out_specs=[row(qk_w), pl.BlockSpec((qk_w, tm), lambda i: (0, i)), row(D_MODEL), row(D_MODEL),
                   row(8), row(8)],
        out_shape=[jax.ShapeDtypeStruct((T, qk_w), _bf16), jax.ShapeDtypeStruct((qk_w, T), _bf16),
                   jax.ShapeDtypeStruct((T, D_MODEL), _bf16), jax.ShapeDtypeStruct((T, D_MODEL), _bf16),
                   jax.ShapeDtypeStruct((T, 8), _f32), jax.ShapeDtypeStruct((T, 8), _f32)],
        compiler_params=_params("parallel"),
        name="mlstm_in_proj",
    )(hp, gain.reshape(1, D_MODEL), wq, wkt, wv, wo, wgate, bgate)


def _mlstm_core_kernel(q_ref, kt_ref, v_ref, o_ref, ut_ref, b_ref, gout_ref, out_ref, state_ref, m_ref):
    C = M_CHUNK
    n_chunks = q_ref.shape[0] // C

    @pl.when(pl.program_id(1) == 0)
    def _():
        state_ref[...] = jnp.zeros_like(state_ref)
        m_ref[...] = jnp.zeros_like(m_ref)

    r = lax.broadcasted_iota(jnp.int32, (C, C), 0)
    c = lax.broadcasted_iota(jnp.int32, (C, C), 1)
    causal = c <= r
    ones_col = jnp.where(lax.broadcasted_iota(jnp.int32, (C, LANES), 1) == 0, 1.0, 0.0).astype(_bf16)

    for h in range(M_HEADS):
        S = state_ref[h]
        m = m_ref[h][0:1, 0:1]
        for ci in range(n_chunks):
            rows = slice(ci * C, (ci + 1) * C)
            qc = q_ref[rows, h * M_DK:(h + 1) * M_DK]
            ktc = kt_ref[h * M_DK:(h + 1) * M_DK, rows]
            vc = v_ref[rows, h * M_DV:(h + 1) * M_DV]
            u_row = ut_ref[h:h + 1, rows]
            b_col = b_ref[rows, h:h + 1]
            v_aug = jnp.concatenate([vc, ones_col], axis=1)
            cu_col = jnp.max(jnp.where(causal, u_row, NEG), axis=1, keepdims=True)
            M = jnp.maximum(cu_col, m)
            E = jnp.where(causal, jnp.exp(jnp.where(causal, u_row - M, 0.0)), 0.0)
            inter = jnp.exp(m - M)
            s_mat = (_dot(qc, ktc) * E).astype(_bf16)
            tot = _dot(s_mat, v_aug) + inter * _dot(qc, S.astype(_bf16))
            num = tot[:, :M_DV]
            den = tot[:, M_DV:M_DV + 1]
            hout = num / jnp.maximum(jnp.abs(den), jnp.exp(-(b_col + M)))
            y = hout * lax.rsqrt(jnp.mean(hout * hout, axis=-1, keepdims=True) + EPS)
            y = y * gout_ref[:, h * M_DV:(h + 1) * M_DV] * o_ref[rows, h * M_DV:(h + 1) * M_DV].astype(_f32)
            out_ref[rows, h * M_DV:(h + 1) * M_DV] = y.astype(out_ref.dtype)
            g_tot = b_col[C - 1:C, :]
            cu_last = cu_col[C - 1:C, :]
            wts_row = jnp.exp(u_row - cu_last)
            m_loc = g_tot + cu_last
            m_new = jnp.maximum(g_tot + m, m_loc)
            ktw = (ktc.astype(_f32) * wts_row).astype(_bf16)
            kv = _dot(ktw, v_aug)
            S = jnp.exp(g_tot + m - m_new) * S + jnp.exp(m_loc - m_new) * kv
            m = m_new
        state_ref[h] = S
        m_ref[h] = jnp.broadcast_to(m, m_ref.shape[1:])


def _mlstm_core(q, kt, v, o, ut, bcol, g_out, batch, batch_rows):
    R = MLSTM_ROWS
    spb = batch_rows // R
    row = lambda n: pl.BlockSpec((R, n), lambda b, s: (b * spb + s, 0))
    col = lambda n: pl.BlockSpec((n, R), lambda b, s: (0, b * spb + s))
    qk_w = M_HEADS * M_DK
    return pl.pallas_call(
        _mlstm_core_kernel,
        grid=(batch, spb),
        in_specs=[row(qk_w), col(qk_w), row(D_MODEL), row(D_MODEL), col(8), row(8), _const_spec((1, D_MODEL))],
        out_specs=row(D_MODEL),
        out_shape=jax.ShapeDtypeStruct((batch * batch_rows, D_MODEL), _bf16),
        scratch_shapes=[pltpu.VMEM((M_HEADS, M_DK, M_DV + LANES), _f32), pltpu.VMEM((M_HEADS, 8, LANES), _f32)],
        compiler_params=_params("parallel", "arbitrary"),
        name="mlstm_core",
    )(q, kt, v, o, ut, bcol, g_out.reshape(1, D_MODEL))


def _linear_residual_kernel(y_ref, w_ref, res_ref, out_ref):
    out_ref[...] = res_ref[...] + _dot(y_ref[...], w_ref[...])


def _linear_residual(y, w, res):
    T = y.shape[0]
    tm = ROW_TILE
    row = pl.BlockSpec((tm, D_MODEL), lambda i: (i, 0))
    return pl.pallas_call(
        _linear_residual_kernel,
        grid=(T // tm,),
        in_specs=[row, _const_spec((D_MODEL, D_MODEL)), row],
        out_specs=row,
        out_shape=jax.ShapeDtypeStruct((T, D_MODEL), _f32),
        compiler_params=_params("parallel"),
        name="linear_residual",
    )(y, w.astype(_bf16), res)


def _route(logits):
    lane = lax.broadcasted_iota(jnp.int32, logits.shape, 1)
    is_g = lane < N_GROUPS
    gl = jnp.where(is_g, logits, -jnp.inf)
    gmax = jnp.max(gl, axis=1, keepdims=True)
    top_g = jnp.min(jnp.where(gl == gmax, lane, LANES), axis=1, keepdims=True)
    p_g = 1.0 / jnp.sum(jnp.where(is_g, jnp.exp(gl - gmax), 0.0), axis=1, keepdims=True)
    e_lane = lane - N_GROUPS
    in_group = (e_lane >= top_g * EXP_PER_GROUP) & (e_lane < (top_g + 1) * EXP_PER_GROUP)
    el = jnp.where(in_group, logits, -jnp.inf)
    v1 = jnp.max(el, axis=1, keepdims=True)
    i1 = jnp.min(jnp.where(el == v1, lane, LANES), axis=1, keepdims=True)
    el2 = jnp.where(lane == i1, -jnp.inf, el)
    v2 = jnp.max(el2, axis=1, keepdims=True)
    i2 = jnp.min(jnp.where(el2 == v2, lane, LANES), axis=1, keepdims=True)
    e2 = jnp.exp(v2 - v1)
    p1 = p_g / (1.0 + e2)
    p2 = p_g * e2 / (1.0 + e2)
    return jnp.where(lane == i1, p1, jnp.where(lane == i2, p2, 0.0))


def _moe_kernel(h_ref, g_ref, wr_ref, br_ref, wg_ref, wu_ref, wd_ref, out_ref, xn_ref, gate_ref):
    grp = pl.program_id(1)

    @pl.when(grp == 0)
    def _():
        x = h_ref[...]
        xn = _rmsnorm(x, g_ref[...])
        xn_ref[...] = xn.astype(_bf16)
        gate_ref[...] = _route(_dot_hi(xn, wr_ref[...]) + br_ref[...])
        out_ref[...] = x

    xb = xn_ref[...]
    gates = gate_ref[...]
    lane = lax.broadcasted_iota(jnp.int32, gates.shape, 1)
    acc = jnp.zeros(out_ref.shape, _f32)
    for e in range(EXP_PER_GROUP):
        gate_e = jnp.sum(jnp.where(lane == N_GROUPS + grp * EXP_PER_GROUP + e, gates, 0.0), axis=1, keepdims=True)
        hg = _dot(xb, wg_ref[e])
        hu = _dot(xb, wu_ref[e])
        act = (hg * jax.nn.sigmoid(hg) * hu * gate_e).astype(_bf16)
        acc = acc + _dot(act, wd_ref[e])
    out_ref[...] += acc


def _moe(h, gain, w_group, b_group, w_router, b_router, w_gate, w_up, w_down):
    T = h.shape[0]
    tm = ROW_TILE
    wr = jnp.zeros((D_MODEL, LANES), _f32)
    wr = wr.at[:, :N_GROUPS].set(w_group).at[:, N_GROUPS:N_GROUPS + N_EXPERTS].set(w_router)
    br = jnp.zeros((1, LANES), _f32)
    br = br.at[0, :N_GROUPS].set(b_group).at[0, N_GROUPS:N_GROUPS + N_EXPERTS].set(b_router)
    row = pl.BlockSpec((tm, D_MODEL), lambda i, g: (i, 0))
    wspec_in = pl.BlockSpec((EXP_PER_GROUP, D_MODEL, D_EXPERT), lambda i, g: (g, 0, 0))
    wspec_out = pl.BlockSpec((EXP_PER_GROUP, D_EXPERT, D_MODEL), lambda i, g: (g, 0, 0))
    return pl.pallas_call(
        _moe_kernel,
        grid=(T // tm, N_GROUPS),
        in_specs=[row, _const_spec((1, D_MODEL)), _const_spec((D_MODEL, LANES)), _const_spec((1, LANES)),
                  wspec_in, wspec_in, wspec_out],
        out_specs=row,
        out_shape=jax.ShapeDtypeStruct((T, D_MODEL), _f32),
        scratch_shapes=[pltpu.VMEM((tm, D_MODEL), _bf16), pltpu.VMEM((tm, LANES), _f32)],
        compiler_params=_params("parallel", "arbitrary"),
        name="hier_moe",
    )(h, gain.reshape(1, D_MODEL), wr, br, w_gate.astype(_bf16), w_up.astype(_bf16), w_down.astype(_bf16))


def _split_heads_with_extra(pair, extra_of_head, out_ref, p):
    lane = lax.broadcasted_iota(jnp.int32, pair.shape, 1)
    swapped = pltpu.roll(pair, F_HD, 1)
    for hh, base in enumerate((pair, swapped)):
        h = 2 * p + hh
        out_ref[h] = jnp.where(lane < F_HD, base, extra_of_head(h, lane)).astype(out_ref.dtype)


def _kv_kernel(h_ref, g_ref, wk_ref, wvt_ref, wf_ref, bf_ref, gk_ref, k_ref, vt_ref, carry_ref):
    j = pl.program_id(1)

    @pl.when(j == 0)
    def _():
        carry_ref[...] = jnp.zeros_like(carry_ref)

    x = h_ref[...]
    rows = x.shape[0]
    xn = _rmsnorm(x, g_ref[...])
    xb = xn.astype(_bf16)
    vt = _dot_nt(wvt_ref[...], xb).astype(_bf16)
    for h in range(F_HEADS):
        vt_ref[h, :F_HD, :] = vt[h * F_HD:(h + 1) * F_HD, :]
        vt_ref[h, F_HD:, :] = jnp.ones((V_ROWS - F_HD, rows), _bf16)
    lf = _log_sigmoid(_dot_hi(xn, wf_ref[...]) + bf_ref[...])
    row_in_batch = j * rows + lax.broadcasted_iota(jnp.int32, lf.shape, 0)
    lf = jnp.where(row_in_batch < PAD_FRONT, 0.0, lf)
    r = lax.broadcasted_iota(jnp.int32, (LANES, LANES), 0)
    c = lax.broadcasted_iota(jnp.int32, (LANES, LANES), 1)
    tril = jnp.where(c <= r, 1.0, 0.0).astype(_f32)
    carry = carry_ref[0:1, :]
    parts = []
    for sb in range(rows // LANES):
        cs = _dot_hi(tril, lf[sb * LANES:(sb + 1) * LANES, :]) + carry
        parts.append(cs)
        carry = cs[LANES - 1:LANES, :]
    carry_ref[...] = jnp.broadcast_to(carry, carry_ref.shape)
    nf = (-LOG2E) * jnp.concatenate(parts, axis=0)

    def forget_lanes(h, lane):
        f = nf[:, h:h + 1]
        hi = f.astype(_bf16).astype(_f32)
        mid = (f - hi).astype(_bf16).astype(_f32)
        lo = (f - hi) - mid
        return jnp.where(lane == F_HD, hi, jnp.where(lane == F_HD + 1, mid, jnp.where(lane == F_HD + 2, lo, 0.0)))

    kf = _dot(xb, wk_ref[...])
    for p in range(D_MODEL // LANES):
        kn = _head_pair_rmsnorm(kf[:, p * LANES:(p + 1) * LANES], gk_ref[...])
        _split_heads_with_extra(kn, forget_lanes, k_ref, p)


def _shared_kv(h, gain, w_kv, b_f, g_k, batch, batch_rows):
    tm = ROW_TILE
    spb = batch_rows // tm
    wk = w_kv[:, :D_MODEL].astype(_bf16)
    wvt = w_kv[:, D_MODEL:2 * D_MODEL].T.astype(_bf16)
    wf = jnp.zeros((D_MODEL, LANES), _f32).at[:, :F_HEADS].set(w_kv[:, 2 * D_MODEL:])
    bf = jnp.zeros((1, LANES), _f32).at[0, :F_HEADS].set(b_f)
    gk2 = jnp.tile(g_k, LANES // F_HD).reshape(1, LANES)
    return pl.pallas_call(
        _kv_kernel,
        grid=(batch, spb),
        in_specs=[pl.BlockSpec((tm, D_MODEL), lambda b, j: (b * spb + j, 0)), _const_spec((1, D_MODEL)),
                  _const_spec((D_MODEL, D_MODEL)), _const_spec((D_MODEL, D_MODEL)), _const_spec((D_MODEL, LANES)),
                  _const_spec((1, LANES)), _const_spec((1, LANES))],
        out_specs=[pl.BlockSpec((None, F_HEADS, tm, LANES), lambda b, j: (b, 0, j, 0)),
                   pl.BlockSpec((None, F_HEADS, V_ROWS, tm), lambda b, j: (b, 0, 0, j))],
        out_shape=[jax.ShapeDtypeStruct((batch, F_HEADS, batch_rows, LANES), _bf16),
                   jax.ShapeDtypeStruct((batch, F_HEADS, V_ROWS, batch_rows), _bf16)],
        scratch_shapes=[pltpu.VMEM((8, LANES), _f32)],
        compiler_params=_params("parallel", "arbitrary"),
        name="shared_kv",
    )(h, gain.reshape(1, D_MODEL), wk, wvt, wf, bf, gk2)


def _q_kernel(h_ref, g_ref, wq_ref, gq_ref, q_ref):
    xb = _rmsnorm(h_ref[...], g_ref[...]).astype(_bf16)
    qf = _dot(xb, wq_ref[...])

    def one_lanes(h, lane):
        return jnp.where((lane >= F_HD) & (lane < F_HD + 3), 1.0, 0.0)

    for p in range(D_MODEL // LANES):
        qn = _head_pair_rmsnorm(qf[:, p * LANES:(p + 1) * LANES], gq_ref[...]) * (F_HD ** -0.5 * LOG2E)
        _split_heads_with_extra(qn, one_lanes, q_ref, p)


def _q_proj(h, gain, w_q, g_q, batch, seq):
    tm = ROW_TILE
    spb = seq // tm
    gq2 = jnp.tile(g_q, LANES // F_HD).reshape(1, LANES)
    return pl.pallas_call(
        _q_kernel,
        grid=(batch, spb),
        in_specs=[pl.BlockSpec((tm, D_MODEL), lambda b, j: (b * spb + j, 0)), _const_spec((1, D_MODEL)),
                  _const_spec((D_MODEL, D_MODEL)), _const_spec((1, LANES))],
        out_specs=pl.BlockSpec((None, F_HEADS, tm, LANES), lambda b, j: (b, 0, j, 0)),
        out_shape=jax.ShapeDtypeStruct((batch, F_HEADS, seq, LANES), _bf16),
        compiler_params=_params("parallel", "parallel"),
        name="attn_q_proj",
    )(h, gain.reshape(1, D_MODEL), w_q.astype(_bf16), gq2)


def _attn_kernel(q_ref, k_ref, vt_ref, out_ref, sa_ref, sb_ref, xa_ref, xb_ref, m_ref, acc_ref):
    iq = pl.program_id(2)
    tq, tk = ATT_TQ, ATT_TK
    m_ref[...] = jnp.full(m_ref.shape, NEG, _f32)
    acc_ref[...] = jnp.zeros_like(acc_ref)

    def scores(ks, rows):
        return [_dot_nt(k_ref[hh, pl.ds(ks, rows), :], q_ref[hh]) for hh in range(2)]

    def issue(jk, s_ref, x_ref):
        st = scores(pl.multiple_of(jk * tk, tk), tk)
        for hh in range(2):
            s_ref[hh] = st[hh]
            x_ref[hh] = jnp.max(st[hh], axis=0, keepdims=True)

    def consume(st, st_max, ks, rows, hh):
        m_prev = m_ref[hh]
        m_new = jnp.maximum(m_prev, st_max)
        alpha = jnp.exp2(m_prev - m_new)
        pt = jnp.exp2((st - m_new).astype(_bf16))
        acc_ref[hh] = alpha * acc_ref[hh] + _dot(vt_ref[hh, :, pl.ds(ks, rows)], pt)
        m_ref[hh] = m_new

    def consume_masked(st, mask, ks, rows, hh):
        st = jnp.where(mask, st, NEG)
        consume(st, jnp.max(st, axis=0, keepdims=True), ks, rows, hh)

    def consume_block(jk, s_ref, x_ref, mask=None):
        ks = pl.multiple_of(jk * tk, tk)
        for hh in range(2):
            if mask is None:
                consume(s_ref[hh], x_ref[hh], ks, tk, hh)
            else:
                consume_masked(s_ref[hh], mask, ks, tk, hh)

    k_idx = lax.broadcasted_iota(jnp.int32, (tk, tq), 0)
    q_idx = lax.broadcasted_iota(jnp.int32, (tk, tq), 1)
    causal = k_idx <= q_idx

    meta_rows = LANES
    key_id = lax.broadcasted_iota(jnp.int32, (meta_rows, tq), 0)
    st_meta = scores(REAL0 - meta_rows, meta_rows)
    issue(1, sa_ref, xa_ref)
    for hh in range(2):
        consume_masked(st_meta[hh], key_id >= meta_rows - N_META, REAL0 - meta_rows, meta_rows, hh)

    n_pairs = iq // 2

    def body(t, carry):
        j = 2 * t + 1
        issue(j + 1, sb_ref, xb_ref)
        consume_block(j, sa_ref, xa_ref)
        issue(j + 2, sa_ref, xa_ref)
        consume_block(j + 1, sb_ref, xb_ref)
        return carry

    lax.fori_loop(0, n_pairs, body, 0)
    j_last = 2 * n_pairs + 1

    @pl.when(iq % 2 == 0)
    def _():
        consume_block(j_last, sa_ref, xa_ref, causal)

    @pl.when(iq % 2 == 1)
    def _():
        issue(j_last + 1, sb_ref, xb_ref)
        consume_block(j_last, sa_ref, xa_ref)
        consume_block(j_last + 1, sb_ref, xb_ref, causal)

    ot = jnp.concatenate([acc_ref[hh, :F_HD, :] / acc_ref[hh, F_HD:F_HD + 1, :] for hh in range(2)], axis=0)
    out_ref[...] = ot.T.astype(out_ref.dtype)


def _attention(q, k, vt, batch, seq, batch_rows):
    assert ATT_TQ == ATT_TK and REAL0 % ATT_TK == 0
    pairs = F_HEADS // 2
    nq = seq // ATT_TQ
    return pl.pallas_call(
        _attn_kernel,
        grid=(batch, pairs, nq),
        in_specs=[pl.BlockSpec((None, 2, ATT_TQ, LANES), lambda b, p, i: (b, p, i, 0)),
                  pl.BlockSpec((None, 2, batch_rows, LANES), lambda b, p, i: (b, p, 0, 0)),
                  pl.BlockSpec((None, 2, V_ROWS, batch_rows), lambda b, p, i: (b, p, 0, 0))],
        out_specs=pl.BlockSpec((None, ATT_TQ, LANES), lambda b, p, i: (b, i, p)),
        out_shape=jax.ShapeDtypeStruct((batch, seq, D_MODEL), _bf16),
        scratch_shapes=[pltpu.VMEM((2, ATT_TK, ATT_TQ), _f32), pltpu.VMEM((2, ATT_TK, ATT_TQ), _f32),
                        pltpu.VMEM((2, 1, ATT_TQ), _f32), pltpu.VMEM((2, 1, ATT_TQ), _f32),
                        pltpu.VMEM((2, 1, ATT_TQ), _f32), pltpu.VMEM((2, V_ROWS, ATT_TQ), _f32)],
        compiler_params=_params("parallel", "parallel", "arbitrary"),
        name="fox_attention",
    )(q, k, vt)


def kernel(x, meta_tokens, norm_mix, norm_ffn, m_w_in, m_b_gate, m_g_out, m_w_out, kv_norm, kv_w, kv_b_f, k_norm,
           f_w_q, f_q_norm, f_w_o, moe_w_group, moe_b_group, moe_w_router, moe_b_router, moe_w_gate, moe_w_up,
           moe_w_down):
    B, S, D = x.shape
    depth = norm_mix.shape[0]
    n_a = m_w_in.shape[0]
    LP = REAL0 + S
    assert D == D_MODEL and S % ATT_TQ == 0 and LP % ROW_TILE == 0 and meta_tokens.shape[0] == N_META
    assert depth > n_a
    meta = jnp.broadcast_to(meta_tokens[None].astype(x.dtype), (B, N_META, D))
    h = jnp.concatenate([jnp.zeros((B, PAD_FRONT, D), x.dtype), meta, x], axis=1).reshape(B * LP, D)
    k_sh = vt_sh = None
    for l in range(depth):
        if l < n_a:
            q, kt, v, o, u, bcol = _mlstm_in_proj(h, norm_mix[l], m_w_in[l], m_b_gate[l], LP)
            y = _mlstm_core(q, kt, v, o, u.T, bcol, m_g_out[l].reshape(-1), B, LP)
            h = _linear_residual(y, m_w_out[l], h)
        else:
            if l == n_a:
                k_sh, vt_sh = _shared_kv(h, kv_norm, kv_w, kv_b_f, k_norm, B, LP)
                h = h.reshape(B, LP, D)[:, REAL0:].reshape(B * S, D)
            j = l - n_a
            q = _q_proj(h, norm_mix[l], f_w_q[j], f_q_norm[j], B, S)
            att = _attention(q, k_sh, vt_sh, B, S, LP)
            h = _linear_residual(att.reshape(B * S, D), f_w_o[j], h)
        h = _moe(h, norm_ffn[l], moe_w_group[l], moe_b_group[l], moe_w_router[l], moe_b_router[l],
                 moe_w_gate[l], moe_w_up[l], moe_w_down[l])
    return h.reshape(B, S, D)
```

```python
import functools

import jax
import jax.numpy as jnp
from jax import lax
from jax.experimental import pallas as pl
from jax.experimental.pallas import tpu as pltpu

D_MODEL = 1024
N_META = 16
M_HEADS = 4
M_DV = D_MODEL // M_HEADS
M_DK = M_DV // 2
M_CHUNK = 256
GATE_CAP = 15.0
F_HEADS = 16
F_HD = D_MODEL // F_HEADS
N_GROUPS = 4
EXP_PER_GROUP = 4
N_EXPERTS = N_GROUPS * EXP_PER_GROUP
D_EXPERT = D_MODEL // 4
EPS = 1e-6
NEG = -1e30

LANES = 128
ROW_TILE = 512
MOE_ROW_TILE = 1024
REAL0 = 512
PAD_FRONT = REAL0 - N_META
MLSTM_ROWS = 256
V_ROWS = F_HD + 16
ATT_TQ = 512
ATT_TK = 512
LOG2E = 1.4426950408889634
VMEM_LIMIT = 56 * 1024 * 1024

_f32 = jnp.float32
_bf16 = jnp.bfloat16


def _dot(a, b):
    return jnp.dot(a, b, preferred_element_type=_f32)


def _split2(w):
    hi = w.astype(_bf16)
    return hi, (w - hi.astype(_f32)).astype(_bf16)


def _dot_x3(x, w_hi, w_lo):
    x_hi, x_lo = _split2(x)
    return _dot(x_hi, w_hi) + _dot(x_lo, w_hi) + _dot(x_hi, w_lo)


def _dot_ones3(ones_mat, x):
    x1 = x.astype(_bf16)
    r1 = x - x1.astype(_f32)
    x2 = r1.astype(_bf16)
    x3 = (r1 - x2.astype(_f32)).astype(_bf16)
    return _dot(ones_mat, x1) + _dot(ones_mat, x2) + _dot(ones_mat, x3)


def _dot_nt(a, b):
    return lax.dot_general(a, b, (((1,), (1,)), ((), ())), preferred_element_type=_f32)


def _rmsnorm(x, g):
    return x * lax.rsqrt(jnp.mean(x * x, axis=-1, keepdims=True) + EPS) * g


def _log_sigmoid(x):
    return jnp.minimum(x, 0.0) - jnp.log1p(jnp.exp(-jnp.abs(x)))


def _params(*sem):
    return pltpu.CompilerParams(dimension_semantics=sem, vmem_limit_bytes=VMEM_LIMIT)


def _const_spec(shape):
    return pl.BlockSpec(shape, lambda *_: (0,) * len(shape))


def _mlstm_in_kernel(blocks_per_batch, h_ref, g_ref, wq_ref, wkt_ref, wv_ref, wo_ref, wg_hi_ref, wg_lo_ref,
                     bgate_ref, q_ref, kt_ref, v_ref, o_ref, u_ref, b_ref):
    i = pl.program_id(0)
    x = h_ref[...]
    xn = _rmsnorm(x, g_ref[...])
    xb = xn.astype(_bf16)
    q_ref[...] = (_dot(xb, wq_ref[...]) * (M_DK ** -0.5)).astype(_bf16)
    kt_ref[...] = _dot_nt(wkt_ref[...], xb).astype(_bf16)
    v_ref[...] = _dot(xb, wv_ref[...]).astype(_bf16)
    o_ref[...] = jax.nn.sigmoid(_dot(xb, wo_ref[...])).astype(_bf16)
    gates = _dot_x3(xn, wg_hi_ref[...], wg_lo_ref[...]) + bgate_ref[...]
    gates = GATE_CAP * jnp.tanh(gates * (1.0 / GATE_CAP))
    rows = x.shape[0]
    row_in_batch = (i % blocks_per_batch) * rows + lax.broadcasted_iota(jnp.int32, (rows, LANES), 0)
    is_pad = row_in_batch < PAD_FRONT
    li = jnp.where(is_pad, NEG, gates)
    lf = jnp.where(is_pad, 0.0, _log_sigmoid(gates))
    r = lax.broadcasted_iota(jnp.int32, (rows, rows), 0)
    c = lax.broadcasted_iota(jnp.int32, (rows, rows), 1)
    chunk_tril = jnp.where((r // M_CHUNK == c // M_CHUNK) & (c <= r), 1.0, 0.0).astype(_bf16)
    b = _dot_ones3(chunk_tril, lf)
    b = pltpu.roll(b, LANES - M_HEADS, 1)
    u_ref[...] = (li - b)[:, :8]
    b_ref[...] = b[:, :8]


def _mlstm_in_proj(hp, gain, w_in, b_gate, batch_rows):
    T = hp.shape[0]
    qk_w = M_HEADS * M_DK
    wq = w_in[:, :qk_w].astype(_bf16)
    wkt = w_in[:, qk_w:2 * qk_w].T.astype(_bf16)
    wv = w_in[:, 2 * qk_w:2 * qk_w + D_MODEL].astype(_bf16)
    wo = w_in[:, 2 * qk_w + D_MODEL:2 * qk_w + 2 * D_MODEL].astype(_bf16)
    n_gate = 2 * M_HEADS
    wg_hi, wg_lo = _split2(jnp.zeros((D_MODEL, LANES), _f32).at[:, :n_gate].set(w_in[:, 2 * qk_w + 2 * D_MODEL:]))
    bgate = jnp.zeros((1, LANES), _f32).at[0, :n_gate].set(b_gate)
    tm = ROW_TILE
    row = lambda n: pl.BlockSpec((tm, n), lambda i: (i, 0))
    return pl.pallas_call(
        functools.partial(_mlstm_in_kernel, batch_rows // tm),
        grid=(T // tm,),
        in_specs=[row(D_MODEL), _const_spec((1, D_MODEL)), _const_spec((D_MODEL, qk_w)),
                  _const_spec((qk_w, D_MODEL)), _const_spec((D_MODEL, D_MODEL)), _const_spec((D_MODEL, D_MODEL)),
                  _const_spec((D_MODEL, LANES)), _const_spec((D_MODEL, LANES)), _const_spec((1, LANES))],
        out_specs=[row(qk_w), pl.BlockSpec((qk_w, tm), lambda i: (0, i)), row(D_MODEL), row(D_MODEL),
                   row(8), row(8)],
        out_shape=[jax.ShapeDtypeStruct((T, qk_w), _bf16), jax.ShapeDtypeStruct((qk_w, T), _bf16),
                   jax.ShapeDtypeStruct((T, D_MODEL), _bf16), jax.ShapeDtypeStruct((T, D_MODEL), _bf16),
                   jax.ShapeDtypeStruct((T, 8), _f32), jax.ShapeDtypeStruct((T, 8), _f32)],
        compiler_params=_params("parallel"),
        name="mlstm_in_proj",
    )(hp, gain.reshape(1, D_MODEL), wq, wkt, wv, wo, wg_hi, wg_lo, bgate)


def _mlstm_core_kernel(q_ref, kt_ref, v_ref, o_ref, ut_ref, b_ref, gout_ref, out_ref, state_ref, m_ref):
    C = M_CHUNK
    n_chunks = q_ref.shape[0] // C

    @pl.when(pl.program_id(1) == 0)
    def _():
        state_ref[...] = jnp.zeros_like(state_ref)
        m_ref[...] = jnp.zeros_like(m_ref)

    r = lax.broadcasted_iota(jnp.int32, (C, C), 0)
    c = lax.broadcasted_iota(jnp.int32, (C, C), 1)
    causal = c <= r
    ones_col = jnp.where(lax.broadcasted_iota(jnp.int32, (C, LANES), 1) == 0, 1.0, 0.0).astype(_bf16)

    for h in range(M_HEADS):
        S = state_ref[h]
        m = m_ref[h][0:1, 0:1]
        for ci in range(n_chunks):
            rows = slice(ci * C, (ci + 1) * C)
            qc = q_ref[rows, h * M_DK:(h + 1) * M_DK]
            ktc = kt_ref[h * M_DK:(h + 1) * M_DK, rows]
            vc = v_ref[rows, h * M_DV:(h + 1) * M_DV]
            u_row = ut_ref[h:h + 1, rows]
            b_col = b_ref[rows, h:h + 1]
            v_aug = jnp.concatenate([vc, ones_col], axis=1)
            cu_col = jnp.max(jnp.where(causal, u_row, NEG), axis=1, keepdims=True)
            M = jnp.maximum(cu_col, m)
            E = jnp.where(causal, jnp.exp(jnp.where(causal, u_row - M, 0.0)), 0.0)
            inter = jnp.exp(m - M)
            s_mat = (_dot(qc, ktc) * E).astype(_bf16)
            tot = _dot(s_mat, v_aug) + inter * _dot(qc, S.astype(_bf16))
            num = tot[:, :M_DV]
            den = tot[:, M_DV:M_DV + 1]
            hout = num / jnp.maximum(jnp.abs(den), jnp.exp(-(b_col + M)))
            y = hout * lax.rsqrt(jnp.mean(hout * hout, axis=-1, keepdims=True) + EPS)
            y = y * gout_ref[:, h * M_DV:(h + 1) * M_DV] * o_ref[rows, h * M_DV:(h + 1) * M_DV].astype(_f32)
            out_ref[rows, h * M_DV:(h + 1) * M_DV] = y.astype(out_ref.dtype)
            g_tot = b_col[C - 1:C, :]
            cu_last = cu_col[C - 1:C, :]
            wts_row = jnp.exp(u_row - cu_last)
            m_loc = g_tot + cu_last
            m_new = jnp.maximum(g_tot + m, m_loc)
            ktw = (ktc.astype(_f32) * wts_row).astype(_bf16)
            kv = _dot(ktw, v_aug)
            S = jnp.exp(g_tot + m - m_new) * S + jnp.exp(m_loc - m_new) * kv
            m = m_new
        state_ref[h] = S
        m_ref[h] = jnp.broadcast_to(m, m_ref.shape[1:])


def _mlstm_core(q, kt, v, o, ut, bcol, g_out, batch, batch_rows):
    R = MLSTM_ROWS
    spb = batch_rows // R
    row = lambda n: pl.BlockSpec((R, n), lambda b, s: (b * spb + s, 0))
    col = lambda n: pl.BlockSpec((n, R), lambda b, s: (0, b * spb + s))
    qk_w = M_HEADS * M_DK
    return pl.pallas_call(
        _mlstm_core_kernel,
        grid=(batch, spb),
        in_specs=[row(qk_w), col(qk_w), row(D_MODEL), row(D_MODEL), col(8), row(8), _const_spec((1, D_MODEL))],
        out_specs=row(D_MODEL),
        out_shape=jax.ShapeDtypeStruct((batch * batch_rows, D_MODEL), _bf16),
        scratch_shapes=[pltpu.VMEM((M_HEADS, M_DK, M_DV + LANES), _f32), pltpu.VMEM((M_HEADS, 8, LANES), _f32)],
        compiler_params=_params("parallel", "arbitrary"),
        name="mlstm_core",
    )(q, kt, v, o, ut, bcol, g_out.reshape(1, D_MODEL))


def _linear_residual_kernel(y_ref, w_ref, res_ref, out_ref):
    out_ref[...] = res_ref[...] + _dot(y_ref[...], w_ref[...])


def _linear_residual(y, w, res, batch, seq, res_batch_rows, res_row0):
    tm = ROW_TILE
    spb = seq // tm
    assert res_batch_rows % tm == 0 and res_row0 % tm == 0
    row = pl.BlockSpec((tm, D_MODEL), lambda b, j: (b * spb + j, 0))
    res_row = pl.BlockSpec((tm, D_MODEL), lambda b, j: (b * (res_batch_rows // tm) + res_row0 // tm + j, 0))
    return pl.pallas_call(
        _linear_residual_kernel,
        grid=(batch, spb),
        in_specs=[row, _const_spec((D_MODEL, D_MODEL)), res_row],
        out_specs=row,
        out_shape=jax.ShapeDtypeStruct((batch * seq, D_MODEL), _f32),
        compiler_params=_params("parallel", "parallel"),
        name="linear_residual",
    )(y, w.astype(_bf16), res)


def _route(logits):
    lane = lax.broadcasted_iota(jnp.int32, logits.shape, 1)
    is_g = lane < N_GROUPS
    gl = jnp.where(is_g, logits, -jnp.inf)
    gmax = jnp.max(gl, axis=1, keepdims=True)
    top_g = jnp.min(jnp.where(gl == gmax, lane, LANES), axis=1, keepdims=True)
    p_g = 1.0 / jnp.sum(jnp.where(is_g, jnp.exp(gl - gmax), 0.0), axis=1, keepdims=True)
    e_lane = lane - N_GROUPS
    in_group = (e_lane >= top_g * EXP_PER_GROUP) & (e_lane < (top_g + 1) * EXP_PER_GROUP)
    el = jnp.where(in_group, logits, -jnp.inf)
    v1 = jnp.max(el, axis=1, keepdims=True)
    i1 = jnp.min(jnp.where(el == v1, lane, LANES), axis=1, keepdims=True)
    el2 = jnp.where(lane == i1, -jnp.inf, el)
    v2 = jnp.max(el2, axis=1, keepdims=True)
    i2 = jnp.min(jnp.where(el2 == v2, lane, LANES), axis=1, keepdims=True)
    e2 = jnp.exp(v2 - v1)
    p1 = p_g / (1.0 + e2)
    p2 = p_g * e2 / (1.0 + e2)
    return jnp.where(lane == i1, p1, jnp.where(lane == i2, p2, 0.0))


def _moe_kernel(h_ref, g_ref, wr_hi_ref, wr_lo_ref, br_ref, wg_ref, wu_ref, wd_ref, out_ref, xn_ref, gate_ref):
    grp = pl.program_id(1)

    @pl.when(grp == 0)
    def _():
        x = h_ref[...]
        xn = _rmsnorm(x, g_ref[...])
        xn_ref[...] = xn.astype(_bf16)
        gate_ref[...] = _route(_dot_x3(xn, wr_hi_ref[...], wr_lo_ref[...]) + br_ref[...])
        out_ref[...] = x

    xb = xn_ref[...]
    gates = gate_ref[...]
    lane = lax.broadcasted_iota(jnp.int32, gates.shape, 1)
    acc = jnp.zeros(out_ref.shape, _f32)
    for e in range(EXP_PER_GROUP):
        gate_e = jnp.sum(jnp.where(lane == N_GROUPS + grp * EXP_PER_GROUP + e, gates, 0.0), axis=1, keepdims=True)
        hg = _dot(xb, wg_ref[e])
        hu = _dot(xb, wu_ref[e])
        act = (hg * jax.nn.sigmoid(hg) * hu * gate_e).astype(_bf16)
        acc = acc + _dot(act, wd_ref[e])
    out_ref[...] += acc


def _moe(h, gain, w_group, b_group, w_router, b_router, w_gate, w_up, w_down):
    T = h.shape[0]
    tm = MOE_ROW_TILE
    wr = jnp.zeros((D_MODEL, LANES), _f32)
    wr_hi, wr_lo = _split2(wr.at[:, :N_GROUPS].set(w_group).at[:, N_GROUPS:N_GROUPS + N_EXPERTS].set(w_router))
    br = jnp.zeros((1, LANES), _f32)
    br = br.at[0, :N_GROUPS].set(b_group).at[0, N_GROUPS:N_GROUPS + N_EXPERTS].set(b_router)
    row = pl.BlockSpec((tm, D_MODEL), lambda i, g: (i, 0))
    wspec_in = pl.BlockSpec((EXP_PER_GROUP, D_MODEL, D_EXPERT), lambda i, g: (g, 0, 0))
    wspec_out = pl.BlockSpec((EXP_PER_GROUP, D_EXPERT, D_MODEL), lambda i, g: (g, 0, 0))
    return pl.pallas_call(
        _moe_kernel,
        grid=(T // tm, N_GROUPS),
        in_specs=[row, _const_spec((1, D_MODEL)), _const_spec((D_MODEL, LANES)), _const_spec((D_MODEL, LANES)),
                  _const_spec((1, LANES)), wspec_in, wspec_in, wspec_out],
        out_specs=row,
        out_shape=jax.ShapeDtypeStruct((T, D_MODEL), _f32),
        scratch_shapes=[pltpu.VMEM((tm, D_MODEL), _bf16), pltpu.VMEM((tm, LANES), _f32)],
        compiler_params=_params("parallel", "arbitrary"),
        name="hier_moe",
    )(h, gain.reshape(1, D_MODEL), wr_hi, wr_lo, br, w_gate.astype(_bf16), w_up.astype(_bf16), w_down.astype(_bf16))


SEG_W = 256


def _segment_ones():
    return jnp.kron(jnp.eye(SEG_W // F_HD, dtype=_f32), jnp.ones((F_HD, F_HD), _f32)).astype(_bf16)


def _heads_rmsnorm(x, seg, g):
    outs = []
    for c in range(x.shape[1] // SEG_W):
        xs = x[:, c * SEG_W:(c + 1) * SEG_W]
        sq_hi, sq_lo = _split2(xs * xs)
        ms = (_dot(sq_hi, seg) + _dot(sq_lo, seg)) * (1.0 / F_HD)
        outs.append(xs * lax.rsqrt(ms + EPS) * g[:, c * SEG_W:(c + 1) * SEG_W])
    return jnp.concatenate(outs, axis=1)


def _kv_kernel(h_ref, g_ref, wk_ref, wvt_ref, wf_hi_ref, wf_lo_ref, bf_ref, gk_ref, seg_ref,
               k_ref, vt_ref, f1_ref, f2_ref, f3_ref, carry_ref):
    j = pl.program_id(1)

    @pl.when(j == 0)
    def _():
        carry_ref[...] = jnp.zeros_like(carry_ref)

    x = h_ref[...]
    rows = x.shape[0]
    xn = _rmsnorm(x, g_ref[...])
    xb = xn.astype(_bf16)
    vt = _dot_nt(wvt_ref[...], xb).astype(_bf16)
    for h in range(F_HEADS):
        vt_ref[h, :F_HD, :] = vt[h * F_HD:(h + 1) * F_HD, :]
        vt_ref[h, F_HD:, :] = jnp.ones((V_ROWS - F_HD, rows), _bf16)
    k_ref[...] = _heads_rmsnorm(_dot(xb, wk_ref[...]), seg_ref[...], gk_ref[...]).astype(_bf16)
    lf = _log_sigmoid(_dot_x3(xn, wf_hi_ref[...], wf_lo_ref[...]) + bf_ref[...])
    row_in_batch = j * rows + lax.broadcasted_iota(jnp.int32, lf.shape, 0)
    lf = jnp.where(row_in_batch < PAD_FRONT, 0.0, lf)
    r = lax.broadcasted_iota(jnp.int32, (LANES, LANES), 0)
    c = lax.broadcasted_iota(jnp.int32, (LANES, LANES), 1)
    tril = jnp.where(c <= r, 1.0, 0.0).astype(_bf16)
    carry = carry_ref[0:1, :]
    for sb in range(rows // LANES):
        sl = slice(sb * LANES, (sb + 1) * LANES)
        cs = _dot_ones3(tril, lf[sl, :]) + carry
        carry = cs[LANES - 1:LANES, :]
        f = ((-LOG2E) * cs)[:, :F_HEADS]
        f1 = f.astype(_bf16)
        r1 = f - f1.astype(_f32)
        f2 = r1.astype(_bf16)
        f1_ref[sl, :] = f1
        f2_ref[sl, :] = f2
        f3_ref[sl, :] = (r1 - f2.astype(_f32)).astype(_bf16)
    carry_ref[...] = jnp.broadcast_to(carry, carry_ref.shape)


def _shared_kv(h, gain, w_kv, b_f, g_k, batch, batch_rows):
    tm = ROW_TILE
    spb = batch_rows // tm
    T = batch * batch_rows
    wk = w_kv[:, :D_MODEL].astype(_bf16)
    wvt = w_kv[:, D_MODEL:2 * D_MODEL].T.astype(_bf16)
    wf_hi, wf_lo = _split2(jnp.zeros((D_MODEL, LANES), _f32).at[:, :F_HEADS].set(w_kv[:, 2 * D_MODEL:]))
    bf = jnp.zeros((1, LANES), _f32).at[0, :F_HEADS].set(b_f)
    row = lambda n: pl.BlockSpec((tm, n), lambda b, j: (b * spb + j, 0))
    kn, vt, f1, f2, f3 = pl.pallas_call(
        _kv_kernel,
        grid=(batch, spb),
        in_specs=[row(D_MODEL), _const_spec((1, D_MODEL)), _const_spec((D_MODEL, D_MODEL)),
                  _const_spec((D_MODEL, D_MODEL)), _const_spec((D_MODEL, LANES)), _const_spec((D_MODEL, LANES)),
                  _const_spec((1, LANES)), _const_spec((1, D_MODEL)), _const_spec((SEG_W, SEG_W))],
        out_specs=[row(D_MODEL), pl.BlockSpec((None, F_HEADS, V_ROWS, tm), lambda b, j: (b, 0, 0, j)),
                   row(F_HEADS), row(F_HEADS), row(F_HEADS)],
        out_shape=[jax.ShapeDtypeStruct((T, D_MODEL), _bf16),
                   jax.ShapeDtypeStruct((batch, F_HEADS, V_ROWS, batch_rows), _bf16)]
        + [jax.ShapeDtypeStruct((T, F_HEADS), _bf16)] * 3,
        scratch_shapes=[pltpu.VMEM((8, LANES), _f32)],
        compiler_params=_params("parallel", "arbitrary"),
        name="shared_kv",
    )(h, gain.reshape(1, D_MODEL), wk, wvt, wf_hi, wf_lo, bf, jnp.tile(g_k, F_HEADS).reshape(1, D_MODEL),
      _segment_ones())
    extra = jnp.stack([f1, f2, f3], axis=-1)
    pad = jnp.zeros((T, F_HEADS, LANES - F_HD - 3), _bf16)
    ka = jnp.concatenate([kn.reshape(T, F_HEADS, F_HD), extra, pad], axis=-1)
    return ka.reshape(batch, batch_rows, F_HEADS * LANES), vt


def _q_kernel(h_ref, g_ref, wq_ref, gq_ref, seg_ref, q_ref):
    xb = _rmsnorm(h_ref[...], g_ref[...]).astype(_bf16)
    qn = _heads_rmsnorm(_dot(xb, wq_ref[...]), seg_ref[...], gq_ref[...])
    q_ref[...] = (qn * (F_HD ** -0.5 * LOG2E)).astype(_bf16)


def _q_proj(h, gain, w_q, g_q, batch, seq, batch_rows, row0):
    tm = ROW_TILE
    spb = seq // tm
    T = batch * seq
    qn = pl.pallas_call(
        _q_kernel,
        grid=(batch, spb),
        in_specs=[pl.BlockSpec((tm, D_MODEL), lambda b, j: (b * (batch_rows // tm) + row0 // tm + j, 0)),
                  _const_spec((1, D_MODEL)), _const_spec((D_MODEL, D_MODEL)), _const_spec((1, D_MODEL)),
                  _const_spec((SEG_W, SEG_W))],
        out_specs=pl.BlockSpec((tm, D_MODEL), lambda b, j: (b * spb + j, 0)),
        out_shape=jax.ShapeDtypeStruct((T, D_MODEL), _bf16),
        compiler_params=_params("parallel", "parallel"),
        name="attn_q_proj",
    )(h, gain.reshape(1, D_MODEL), w_q.astype(_bf16), jnp.tile(g_q, F_HEADS).reshape(1, D_MODEL), _segment_ones())
    extra = jnp.ones((T, F_HEADS, 3), _bf16)
    pad = jnp.zeros((T, F_HEADS, LANES - F_HD - 3), _bf16)
    qa = jnp.concatenate([qn.reshape(T, F_HEADS, F_HD), extra, pad], axis=-1)
    return qa.reshape(batch, seq, F_HEADS * LANES)


def _attn_kernel(q_ref, k_ref, vt_ref, out_ref, sa_ref, sb_ref, xa_ref, xb_ref, m_ref, acc_ref):
    iq = pl.program_id(2)
    tq, tk = ATT_TQ, ATT_TK
    m_ref[...] = jnp.full(m_ref.shape, NEG, _f32)
    acc_ref[...] = jnp.zeros_like(acc_ref)

    def scores(ks, rows, hh):
        return _dot_nt(k_ref[pl.ds(ks, rows), hh * LANES:(hh + 1) * LANES], q_ref[:, hh * LANES:(hh + 1) * LANES])

    def issue_head(jk, s_ref, x_ref, hh):
        st = scores(pl.multiple_of(jk * tk, tk), tk, hh)
        s_ref[hh] = st
        x_ref[hh] = jnp.max(st, axis=0, keepdims=True)

    def issue(jk, s_ref, x_ref):
        for hh in range(2):
            issue_head(jk, s_ref, x_ref, hh)

    def consume(st, st_max, ks, rows, hh):
        m_prev = m_ref[hh]
        m_new = jnp.maximum(m_prev, st_max)
        alpha = jnp.exp2(m_prev - m_new)
        pt = jnp.exp2((st - m_new).astype(_bf16))
        acc_ref[hh] = alpha * acc_ref[hh] + _dot(vt_ref[hh, :, pl.ds(ks, rows)], pt)
        m_ref[hh] = m_new

    def consume_masked(st, mask, ks, rows, hh):
        st = jnp.where(mask, st, NEG)
        consume(st, jnp.max(st, axis=0, keepdims=True), ks, rows, hh)

    def consume_block(jk, s_ref, x_ref, mask=None):
        ks = pl.multiple_of(jk * tk, tk)
        for hh in range(2):
            if mask is None:
                consume(s_ref[hh], x_ref[hh], ks, tk, hh)
            else:
                consume_masked(s_ref[hh], mask, ks, tk, hh)

    def step(j_issue, si_ref, xi_ref, j_cons, sc_ref, xc_ref):
        ks = pl.multiple_of(j_cons * tk, tk)
        for hh in range(2):
            issue_head(j_issue, si_ref, xi_ref, hh)
            consume(sc_ref[hh], xc_ref[hh], ks, tk, hh)

    k_idx = lax.broadcasted_iota(jnp.int32, (tk, tq), 0)
    q_idx = lax.broadcasted_iota(jnp.int32, (tk, tq), 1)
    causal = k_idx <= q_idx

    meta_rows = LANES
    key_id = lax.broadcasted_iota(jnp.int32, (meta_rows, tq), 0)
    st_meta = [scores(REAL0 - meta_rows, meta_rows, hh) for hh in range(2)]
    issue(1, sa_ref, xa_ref)
    for hh in range(2):
        consume_masked(st_meta[hh], key_id >= meta_rows - N_META, REAL0 - meta_rows, meta_rows, hh)

    n_pairs = iq // 2

    def body(t, carry):
        j = 2 * t + 1
        step(j + 1, sb_ref, xb_ref, j, sa_ref, xa_ref)
        step(j + 2, sa_ref, xa_ref, j + 1, sb_ref, xb_ref)
        return carry

    lax.fori_loop(0, n_pairs, body, 0)
    j_last = 2 * n_pairs + 1

    @pl.when(iq % 2 == 0)
    def _():
        consume_block(j_last, sa_ref, xa_ref, causal)

    @pl.when(iq % 2 == 1)
    def _():
        issue(j_last + 1, sb_ref, xb_ref)
        consume_block(j_last, sa_ref, xa_ref)
        consume_block(j_last + 1, sb_ref, xb_ref, causal)

    ot = jnp.concatenate([acc_ref[hh, :F_HD, :] / acc_ref[hh, F_HD:F_HD + 1, :] for hh in range(2)], axis=0)
    out_ref[...] = ot.T.astype(out_ref.dtype)


def _attention(q, k, vt, batch, seq, batch_rows):
    assert ATT_TQ == ATT_TK and REAL0 % ATT_TK == 0
    pairs = F_HEADS // 2
    nq = seq // ATT_TQ
    return pl.pallas_call(
        _attn_kernel,
        grid=(batch, pairs, nq),
        in_specs=[pl.BlockSpec((None, ATT_TQ, 2 * LANES), lambda b, p, i: (b, i, p)),
                  pl.BlockSpec((None, batch_rows, 2 * LANES), lambda b, p, i: (b, 0, p)),
                  pl.BlockSpec((None, 2, V_ROWS, batch_rows), lambda b, p, i: (b, p, 0, 0))],
        out_specs=pl.BlockSpec((None, ATT_TQ, LANES), lambda b, p, i: (b, i, p)),
        out_shape=jax.ShapeDtypeStruct((batch, seq, D_MODEL), _bf16),
        scratch_shapes=[pltpu.VMEM((2, ATT_TK, ATT_TQ), _f32), pltpu.VMEM((2, ATT_TK, ATT_TQ), _f32),
                        pltpu.VMEM((2, 1, ATT_TQ), _f32), pltpu.VMEM((2, 1, ATT_TQ), _f32),
                        pltpu.VMEM((2, 1, ATT_TQ), _f32), pltpu.VMEM((2, V_ROWS, ATT_TQ), _f32)],
        compiler_params=_params("parallel", "parallel", "arbitrary"),
        name="fox_attention",
    )(q, k, vt)


def kernel(x, meta_tokens, norm_mix, norm_ffn, m_w_in, m_b_gate, m_g_out, m_w_out, kv_norm, kv_w, kv_b_f, k_norm,
           f_w_q, f_q_norm, f_w_o, moe_w_group, moe_b_group, moe_w_router, moe_b_router, moe_w_gate, moe_w_up,
           moe_w_down):
    B, S, D = x.shape
    depth = norm_mix.shape[0]
    n_a = m_w_in.shape[0]
    LP = REAL0 + S
    assert D == D_MODEL and S % ATT_TQ == 0 and LP % ROW_TILE == 0 and meta_tokens.shape[0] == N_META
    assert (B * LP) % MOE_ROW_TILE == 0 and (B * S) % MOE_ROW_TILE == 0
    assert depth > n_a and REAL0 % M_CHUNK == 0 and MLSTM_ROWS % M_CHUNK == 0
    meta = jnp.broadcast_to(meta_tokens[None].astype(x.dtype), (B, N_META, D))
    h = jnp.concatenate([jnp.zeros((B, PAD_FRONT, D), x.dtype), meta, x], axis=1).reshape(B * LP, D)
    k_sh = vt_sh = None
    for l in range(depth):
        if l < n_a:
            q, kt, v, o, u, bcol = _mlstm_in_proj(h, norm_mix[l], m_w_in[l], m_b_gate[l], LP)
            y = _mlstm_core(q, kt, v, o, u.T, bcol, m_g_out[l].reshape(-1), B, LP)
            h = _linear_residual(y, m_w_out[l], h, B, LP, LP, 0)
        else:
            if l == n_a:
                k_sh, vt_sh = _shared_kv(h, kv_norm, kv_w, kv_b_f, k_norm, B, LP)
                rows_per_batch, row0 = LP, REAL0
            j = l - n_a
            q = _q_proj(h, norm_mix[l], f_w_q[j], f_q_norm[j], B, S, rows_per_batch, row0)
            att = _attention(q, k_sh, vt_sh, B, S, LP)
            h = _linear_residual(att.reshape(B * S, D), f_w_o[j], h, B, S, rows_per_batch, row0)
            rows_per_batch, row0 = S, 0
        h = _moe(h, norm_ffn[l], moe_w_group[l], moe_b_group[l], moe_w_router[l], moe_b_router[l],
                 moe_w_gate[l], moe_w_up[l], moe_w_down[l])
    return h.reshape(B, S, D)
```

```python
import functools

import jax
import numpy as np
import jax.numpy as jnp
from jax import lax
from jax.experimental import pallas as pl
from jax.experimental.pallas import tpu as pltpu

D_MODEL = 1024
N_META = 16
M_HEADS = 4
M_DV = D_MODEL // M_HEADS
M_DK = M_DV // 2
M_CHUNK = 256
GATE_CAP = 15.0
F_HEADS = 16
F_HD = D_MODEL // F_HEADS
N_GROUPS = 4
EXP_PER_GROUP = 4
N_EXPERTS = N_GROUPS * EXP_PER_GROUP
D_EXPERT = D_MODEL // 4
EPS = 1e-6
NEG = -1e30

LANES = 128
ROW_TILE = 512
MOE_ROW_TILE = 1024
REAL0 = 512
PAD_FRONT = REAL0 - N_META
MLSTM_ROWS = 256
V_ROWS = F_HD + 16
ATT_TQ = 512
ATT_TK = 512
LOG2E = 1.4426950408889634
VMEM_LIMIT = 56 * 1024 * 1024

_f32 = jnp.float32
_bf16 = jnp.bfloat16


def _dot(a, b):
    return jnp.dot(a, b, preferred_element_type=_f32)


def _split2(w):
    hi = w.astype(_bf16)
    return hi, (w - hi.astype(_f32)).astype(_bf16)


def _dot_x3(x, w_hi, w_lo):
    x_hi, x_lo = _split2(x)
    return _dot(x_hi, w_hi) + _dot(x_lo, w_hi) + _dot(x_hi, w_lo)


def _dot_ones3(ones_mat, x):
    x1 = x.astype(_bf16)
    r1 = x - x1.astype(_f32)
    x2 = r1.astype(_bf16)
    x3 = (r1 - x2.astype(_f32)).astype(_bf16)
    return _dot(ones_mat, x1) + _dot(ones_mat, x2) + _dot(ones_mat, x3)


def _dot_nt(a, b):
    return lax.dot_general(a, b, (((1,), (1,)), ((), ())), preferred_element_type=_f32)


def _rmsnorm(x, g):
    return x * lax.rsqrt(jnp.mean(x * x, axis=-1, keepdims=True) + EPS) * g


def _log_sigmoid(x):
    return jnp.minimum(x, 0.0) - jnp.log1p(jnp.exp(-jnp.abs(x)))


def _params(*sem):
    return pltpu.CompilerParams(dimension_semantics=sem, vmem_limit_bytes=VMEM_LIMIT)


def _const_spec(shape):
    return pl.BlockSpec(shape, lambda *_: (0,) * len(shape))


def _mlstm_in_kernel(blocks_per_batch, h_ref, g_ref, wq_ref, wkt_ref, wv_ref, wo_ref, wg_hi_ref, wg_lo_ref,
                     bgate_ref, q_ref, kt_ref, v_ref, o_ref, u_ref, b_ref):
    i = pl.program_id(0)
    x = h_ref[...]
    xn = _rmsnorm(x, g_ref[...])
    xb = xn.astype(_bf16)
    q_ref[...] = (_dot(xb, wq_ref[...]) * (M_DK ** -0.5)).astype(_bf16)
    kt_ref[...] = _dot_nt(wkt_ref[...], xb).astype(_bf16)
    v_ref[...] = _dot(xb, wv_ref[...]).astype(_bf16)
    o_ref[...] = jax.nn.sigmoid(_dot(xb, wo_ref[...])).astype(_bf16)
    gates = _dot_x3(xn, wg_hi_ref[...], wg_lo_ref[...]) + bgate_ref[...]
    gates = GATE_CAP * jnp.tanh(gates * (1.0 / GATE_CAP))
    rows = x.shape[0]
    row_in_batch = (i % blocks_per_batch) * rows + lax.broadcasted_iota(jnp.int32, (rows, LANES), 0)
    is_pad = row_in_batch < PAD_FRONT
    li = jnp.where(is_pad, NEG, gates)
    lf = jnp.where(is_pad, 0.0, _log_sigmoid(gates))
    r = lax.broadcasted_iota(jnp.int32, (rows, rows), 0)
    c = lax.broadcasted_iota(jnp.int32, (rows, rows), 1)
    chunk_tril = jnp.where((r // M_CHUNK == c // M_CHUNK) & (c <= r), 1.0, 0.0).astype(_bf16)
    b = _dot_ones3(chunk_tril, lf)
    b = pltpu.roll(b, LANES - M_HEADS, 1)
    u_ref[...] = (li - b)[:, :8]
    b_ref[...] = b[:, :8]


def _mlstm_in_proj(hp, gain, w_in, b_gate, batch_rows):
    T = hp.shape[0]
    qk_w = M_HEADS * M_DK
    wq = w_in[:, :qk_w].astype(_bf16)
    wkt = w_in[:, qk_w:2 * qk_w].T.astype(_bf16)
    wv = w_in[:, 2 * qk_w:2 * qk_w + D_MODEL].astype(_bf16)
    wo = w_in[:, 2 * qk_w + D_MODEL:2 * qk_w + 2 * D_MODEL].astype(_bf16)
    n_gate = 2 * M_HEADS
    wg_hi, wg_lo = _split2(jnp.zeros((D_MODEL, LANES), _f32).at[:, :n_gate].set(w_in[:, 2 * qk_w + 2 * D_MODEL:]))
    bgate = jnp.zeros((1, LANES), _f32).at[0, :n_gate].set(b_gate)
    tm = ROW_TILE
    row = lambda n: pl.BlockSpec((tm, n), lambda i: (i, 0))
    return pl.pallas_call(
        functools.partial(_mlstm_in_kernel, batch_rows // tm),
        grid=(T // tm,),
        in_specs=[row(D_MODEL), _const_spec((1, D_MODEL)), _const_spec((D_MODEL, qk_w)),
                  _const_spec((qk_w, D_MODEL)), _const_spec((D_MODEL, D_MODEL)), _const_spec((D_MODEL, D_MODEL)),
                  _const_spec((D_MODEL, LANES)), _const_spec((D_MODEL, LANES)), _const_spec((1, LANES))],
        out_specs=[row(qk_w), pl.BlockSpec((qk_w, tm), lambda i: (0, i)), row(D_MODEL), row(D_MODEL),
                   row(8), row(8)],
        out_shape=[jax.ShapeDtypeStruct((T, qk_w), _bf16), jax.ShapeDtypeStruct((qk_w, T), _bf16),
                   jax.ShapeDtypeStruct((T, D_MODEL), _bf16), jax.ShapeDtypeStruct((T, D_MODEL), _bf16),
                   jax.ShapeDtypeStruct((T, 8), _f32), jax.ShapeDtypeStruct((T, 8), _f32)],
        compiler_params=_params("parallel"),
        name="mlstm_in_proj",
    )(hp, gain.reshape(1, D_MODEL), wq, wkt, wv, wo, wg_hi, wg_lo, bgate)


def _mlstm_core_kernel(q_ref, kt_ref, v_ref, o_ref, ut_ref, b_ref, gout_ref, out_ref, state_ref, m_ref):
    C = M_CHUNK
    n_chunks = q_ref.shape[0] // C

    @pl.when(pl.program_id(1) == 0)
    def _():
        state_ref[...] = jnp.zeros_like(state_ref)
        m_ref[...] = jnp.zeros_like(m_ref)

    r = lax.broadcasted_iota(jnp.int32, (C, C), 0)
    c = lax.broadcasted_iota(jnp.int32, (C, C), 1)
    causal = c <= r
    ones_col = jnp.where(lax.broadcasted_iota(jnp.int32, (C, LANES), 1) == 0, 1.0, 0.0).astype(_bf16)

    for h in range(M_HEADS):
        S = state_ref[h]
        m = m_ref[h][0:1, 0:1]
        for ci in range(n_chunks):
            rows = slice(ci * C, (ci + 1) * C)
            qc = q_ref[rows, h * M_DK:(h + 1) * M_DK]
            ktc = kt_ref[h * M_DK:(h + 1) * M_DK, rows]
            vc = v_ref[rows, h * M_DV:(h + 1) * M_DV]
            u_row = ut_ref[h:h + 1, rows]
            b_col = b_ref[rows, h:h + 1]
            v_aug = jnp.concatenate([vc, ones_col], axis=1)
            cu_col = jnp.max(jnp.where(causal, u_row, NEG), axis=1, keepdims=True)
            M = jnp.maximum(cu_col, m)
            E = jnp.where(causal, jnp.exp(jnp.where(causal, u_row - M, 0.0)), 0.0)
            inter = jnp.exp(m - M)
            s_mat = (_dot(qc, ktc) * E).astype(_bf16)
            tot = _dot(s_mat, v_aug) + inter * _dot(qc, S.astype(_bf16))
            num = tot[:, :M_DV]
            den = tot[:, M_DV:M_DV + 1]
            hout = num / jnp.maximum(jnp.abs(den), jnp.exp(-(b_col + M)))
            y = hout * lax.rsqrt(jnp.mean(hout * hout, axis=-1, keepdims=True) + EPS)
            y = y * gout_ref[:, h * M_DV:(h + 1) * M_DV] * o_ref[rows, h * M_DV:(h + 1) * M_DV].astype(_f32)
            out_ref[rows, h * M_DV:(h + 1) * M_DV] = y.astype(out_ref.dtype)
            g_tot = b_col[C - 1:C, :]
            cu_last = cu_col[C - 1:C, :]
            wts_row = jnp.exp(u_row - cu_last)
            m_loc = g_tot + cu_last
            m_new = jnp.maximum(g_tot + m, m_loc)
            ktw = (ktc.astype(_f32) * wts_row).astype(_bf16)
            kv = _dot(ktw, v_aug)
            S = jnp.exp(g_tot + m - m_new) * S + jnp.exp(m_loc - m_new) * kv
            m = m_new
        state_ref[h] = S
        m_ref[h] = jnp.broadcast_to(m, m_ref.shape[1:])


def _mlstm_core(q, kt, v, o, ut, bcol, g_out, batch, batch_rows):
    R = MLSTM_ROWS
    spb = batch_rows // R
    row = lambda n: pl.BlockSpec((R, n), lambda b, s: (b * spb + s, 0))
    col = lambda n: pl.BlockSpec((n, R), lambda b, s: (0, b * spb + s))
    qk_w = M_HEADS * M_DK
    return pl.pallas_call(
        _mlstm_core_kernel,
        grid=(batch, spb),
        in_specs=[row(qk_w), col(qk_w), row(D_MODEL), row(D_MODEL), col(8), row(8), _const_spec((1, D_MODEL))],
        out_specs=row(D_MODEL),
        out_shape=jax.ShapeDtypeStruct((batch * batch_rows, D_MODEL), _bf16),
        scratch_shapes=[pltpu.VMEM((M_HEADS, M_DK, M_DV + LANES), _f32), pltpu.VMEM((M_HEADS, 8, LANES), _f32)],
        compiler_params=_params("parallel", "arbitrary"),
        name="mlstm_core",
    )(q, kt, v, o, ut, bcol, g_out.reshape(1, D_MODEL))


def _linear_residual_kernel(y_ref, w_ref, res_ref, out_ref):
    out_ref[...] = res_ref[...] + _dot(y_ref[...], w_ref[...])


def _linear_residual(y, w, res, batch, seq, res_batch_rows, res_row0):
    tm = ROW_TILE
    spb = seq // tm
    assert res_batch_rows % tm == 0 and res_row0 % tm == 0
    row = pl.BlockSpec((tm, D_MODEL), lambda b, j: (b * spb + j, 0))
    res_row = pl.BlockSpec((tm, D_MODEL), lambda b, j: (b * (res_batch_rows // tm) + res_row0 // tm + j, 0))
    return pl.pallas_call(
        _linear_residual_kernel,
        grid=(batch, spb),
        in_specs=[row, _const_spec((D_MODEL, D_MODEL)), res_row],
        out_specs=row,
        out_shape=jax.ShapeDtypeStruct((batch * seq, D_MODEL), _f32),
        compiler_params=_params("parallel", "parallel"),
        name="linear_residual",
    )(y, w.astype(_bf16), res)


def _route(logits):
    lane = lax.broadcasted_iota(jnp.int32, logits.shape, 1)
    is_g = lane < N_GROUPS
    gl = jnp.where(is_g, logits, -jnp.inf)
    gmax = jnp.max(gl, axis=1, keepdims=True)
    top_g = jnp.min(jnp.where(gl == gmax, lane, LANES), axis=1, keepdims=True)
    p_g = 1.0 / jnp.sum(jnp.where(is_g, jnp.exp(gl - gmax), 0.0), axis=1, keepdims=True)
    e_lane = lane - N_GROUPS
    in_group = (e_lane >= top_g * EXP_PER_GROUP) & (e_lane < (top_g + 1) * EXP_PER_GROUP)
    el = jnp.where(in_group, logits, -jnp.inf)
    v1 = jnp.max(el, axis=1, keepdims=True)
    i1 = jnp.min(jnp.where(el == v1, lane, LANES), axis=1, keepdims=True)
    el2 = jnp.where(lane == i1, -jnp.inf, el)
    v2 = jnp.max(el2, axis=1, keepdims=True)
    i2 = jnp.min(jnp.where(el2 == v2, lane, LANES), axis=1, keepdims=True)
    e2 = jnp.exp(v2 - v1)
    p1 = p_g / (1.0 + e2)
    p2 = p_g * e2 / (1.0 + e2)
    return jnp.where(lane == i1, p1, jnp.where(lane == i2, p2, 0.0))


def _moe_kernel(h_ref, g_ref, wr_hi_ref, wr_lo_ref, br_ref, wg_ref, wu_ref, wd_ref, out_ref, xn_ref, gate_ref):
    grp = pl.program_id(1)

    @pl.when(grp == 0)
    def _():
        x = h_ref[...]
        xn = _rmsnorm(x, g_ref[...])
        xn_ref[...] = xn.astype(_bf16)
        gate_ref[...] = _route(_dot_x3(xn, wr_hi_ref[...], wr_lo_ref[...]) + br_ref[...])
        out_ref[...] = x

    xb = xn_ref[...]
    gates = gate_ref[...]
    lane = lax.broadcasted_iota(jnp.int32, gates.shape, 1)
    acc = jnp.zeros(out_ref.shape, _f32)
    for e in range(EXP_PER_GROUP):
        gate_e = jnp.sum(jnp.where(lane == N_GROUPS + grp * EXP_PER_GROUP + e, gates, 0.0), axis=1, keepdims=True)
        hg = _dot(xb, wg_ref[e])
        hu = _dot(xb, wu_ref[e])
        act = (hg * jax.nn.sigmoid(hg) * hu * gate_e).astype(_bf16)
        acc = acc + _dot(act, wd_ref[e])
    out_ref[...] += acc


def _moe(h, gain, w_group, b_group, w_router, b_router, w_gate, w_up, w_down):
    T = h.shape[0]
    tm = MOE_ROW_TILE
    wr = jnp.zeros((D_MODEL, LANES), _f32)
    wr_hi, wr_lo = _split2(wr.at[:, :N_GROUPS].set(w_group).at[:, N_GROUPS:N_GROUPS + N_EXPERTS].set(w_router))
    br = jnp.zeros((1, LANES), _f32)
    br = br.at[0, :N_GROUPS].set(b_group).at[0, N_GROUPS:N_GROUPS + N_EXPERTS].set(b_router)
    row = pl.BlockSpec((tm, D_MODEL), lambda i, g: (i, 0))
    wspec_in = pl.BlockSpec((EXP_PER_GROUP, D_MODEL, D_EXPERT), lambda i, g: (g, 0, 0))
    wspec_out = pl.BlockSpec((EXP_PER_GROUP, D_EXPERT, D_MODEL), lambda i, g: (g, 0, 0))
    return pl.pallas_call(
        _moe_kernel,
        grid=(T // tm, N_GROUPS),
        in_specs=[row, _const_spec((1, D_MODEL)), _const_spec((D_MODEL, LANES)), _const_spec((D_MODEL, LANES)),
                  _const_spec((1, LANES)), wspec_in, wspec_in, wspec_out],
        out_specs=row,
        out_shape=jax.ShapeDtypeStruct((T, D_MODEL), _f32),
        scratch_shapes=[pltpu.VMEM((tm, D_MODEL), _bf16), pltpu.VMEM((tm, LANES), _f32)],
        compiler_params=_params("parallel", "arbitrary"),
        name="hier_moe",
    )(h, gain.reshape(1, D_MODEL), wr_hi, wr_lo, br, w_gate.astype(_bf16), w_up.astype(_bf16), w_down.astype(_bf16))


SEG_W = 256


def _segment_ones():
    return jnp.kron(jnp.eye(SEG_W // F_HD, dtype=_f32), jnp.ones((F_HD, F_HD), _f32)).astype(_bf16)


def _head_place_matrices():
    pairs = F_HEADS // 2
    place = np.zeros((pairs, 2 * LANES, 2 * LANES), np.float32)
    i = np.arange(F_HD)
    for p in range(pairs):
        place[p, i, i] = 1.0
        place[p, F_HD + i, LANES + i] = 1.0
        for hh in range(2):
            for t in range(3):
                place[p, LANES + F_HEADS * t + 2 * p + hh, hh * LANES + F_HD + t] = 1.0
    return jnp.asarray(place, _bf16)


def _heads_rmsnorm(x, seg, g):
    outs = []
    for c in range(x.shape[1] // SEG_W):
        xs = x[:, c * SEG_W:(c + 1) * SEG_W]
        sq_hi, sq_lo = _split2(xs * xs)
        ms = (_dot(sq_hi, seg) + _dot(sq_lo, seg)) * (1.0 / F_HD)
        outs.append(xs * lax.rsqrt(ms + EPS) * g[:, c * SEG_W:(c + 1) * SEG_W])
    return jnp.concatenate(outs, axis=1)


def _kv_kernel(h_ref, g_ref, wk_ref, wvt_ref, wf_hi_ref, wf_lo_ref, bf_ref, gk_ref, seg_ref, place_ref,
               k_ref, vt_ref, carry_ref):
    j = pl.program_id(1)

    @pl.when(j == 0)
    def _():
        carry_ref[...] = jnp.zeros_like(carry_ref)

    x = h_ref[...]
    rows = x.shape[0]
    xn = _rmsnorm(x, g_ref[...])
    xb = xn.astype(_bf16)
    vt = _dot_nt(wvt_ref[...], xb).astype(_bf16)
    for h in range(F_HEADS):
        vt_ref[h, :F_HD, :] = vt[h * F_HD:(h + 1) * F_HD, :]
        vt_ref[h, F_HD:, :] = jnp.ones((V_ROWS - F_HD, rows), _bf16)
    kn = _heads_rmsnorm(_dot(xb, wk_ref[...]), seg_ref[...], gk_ref[...]).astype(_bf16)
    lf = _log_sigmoid(_dot_x3(xn, wf_hi_ref[...], wf_lo_ref[...]) + bf_ref[...])
    row_in_batch = j * rows + lax.broadcasted_iota(jnp.int32, lf.shape, 0)
    lf = jnp.where(row_in_batch < PAD_FRONT, 0.0, lf)
    r = lax.broadcasted_iota(jnp.int32, (LANES, LANES), 0)
    c = lax.broadcasted_iota(jnp.int32, (LANES, LANES), 1)
    tril = jnp.where(c <= r, 1.0, 0.0).astype(_bf16)
    carry = carry_ref[0:1, :]
    terms = []
    for sb in range(rows // LANES):
        cs = _dot_ones3(tril, lf[sb * LANES:(sb + 1) * LANES, :]) + carry
        carry = cs[LANES - 1:LANES, :]
        f = (-LOG2E) * cs
        f1 = f.astype(_bf16).astype(_f32)
        f2 = (f - f1).astype(_bf16).astype(_f32)
        f3 = (f - f1) - f2
        terms.append(jnp.where(c < F_HEADS, f1, jnp.where(c < 2 * F_HEADS, pltpu.roll(f2, F_HEADS, 1),
                                                          jnp.where(c < 3 * F_HEADS, pltpu.roll(f3, 2 * F_HEADS, 1), 0.0))))
    carry_ref[...] = jnp.broadcast_to(carry, carry_ref.shape)
    fterms = jnp.concatenate(terms, axis=0).astype(_bf16)
    for p in range(F_HEADS // 2):
        lhs = jnp.concatenate([kn[:, p * LANES:(p + 1) * LANES], fterms], axis=1)
        k_ref[:, 2 * p * LANES:2 * (p + 1) * LANES] = _dot(lhs, place_ref[p]).astype(_bf16)


def _shared_kv(h, gain, w_kv, b_f, g_k, batch, batch_rows):
    tm = ROW_TILE
    spb = batch_rows // tm
    T = batch * batch_rows
    wk = w_kv[:, :D_MODEL].astype(_bf16)
    wvt = w_kv[:, D_MODEL:2 * D_MODEL].T.astype(_bf16)
    wf_hi, wf_lo = _split2(jnp.zeros((D_MODEL, LANES), _f32).at[:, :F_HEADS].set(w_kv[:, 2 * D_MODEL:]))
    bf = jnp.zeros((1, LANES), _f32).at[0, :F_HEADS].set(b_f)
    row = lambda n: pl.BlockSpec((tm, n), lambda b, j: (b * spb + j, 0))
    wide = F_HEADS * LANES
    ka, vt = pl.pallas_call(
        _kv_kernel,
        grid=(batch, spb),
        in_specs=[row(D_MODEL), _const_spec((1, D_MODEL)), _const_spec((D_MODEL, D_MODEL)),
                  _const_spec((D_MODEL, D_MODEL)), _const_spec((D_MODEL, LANES)), _const_spec((D_MODEL, LANES)),
                  _const_spec((1, LANES)), _const_spec((1, D_MODEL)), _const_spec((SEG_W, SEG_W)),
                  _const_spec((F_HEADS // 2, 2 * LANES, 2 * LANES))],
        out_specs=[row(wide), pl.BlockSpec((None, F_HEADS, V_ROWS, tm), lambda b, j: (b, 0, 0, j))],
        out_shape=[jax.ShapeDtypeStruct((T, wide), _bf16),
                   jax.ShapeDtypeStruct((batch, F_HEADS, V_ROWS, batch_rows), _bf16)],
        scratch_shapes=[pltpu.VMEM((8, LANES), _f32)],
        compiler_params=_params("parallel", "arbitrary"),
        name="shared_kv",
    )(h, gain.reshape(1, D_MODEL), wk, wvt, wf_hi, wf_lo, bf, jnp.tile(g_k, F_HEADS).reshape(1, D_MODEL),
      _segment_ones(), _head_place_matrices())
    return ka.reshape(batch, batch_rows, wide), vt


def _q_kernel(h_ref, g_ref, wq_ref, gq_ref, seg_ref, place_ref, q_ref):
    xb = _rmsnorm(h_ref[...], g_ref[...]).astype(_bf16)
    qn = _heads_rmsnorm(_dot(xb, wq_ref[...]), seg_ref[...], gq_ref[...])
    qn = (qn * (F_HD ** -0.5 * LOG2E)).astype(_bf16)
    lane = lax.broadcasted_iota(jnp.int32, (1, 2 * LANES), 1) % LANES
    ones = jnp.where((lane >= F_HD) & (lane < F_HD + 3), 1.0, 0.0)
    for p in range(F_HEADS // 2):
        spread = _dot(qn[:, p * LANES:(p + 1) * LANES], place_ref[0, :LANES, :])
        q_ref[:, 2 * p * LANES:2 * (p + 1) * LANES] = (spread + ones).astype(_bf16)


def _q_proj(h, gain, w_q, g_q, batch, seq, batch_rows, row0):
    tm = ROW_TILE
    spb = seq // tm
    wide = F_HEADS * LANES
    qa = pl.pallas_call(
        _q_kernel,
        grid=(batch, spb),
        in_specs=[pl.BlockSpec((tm, D_MODEL), lambda b, j: (b * (batch_rows // tm) + row0 // tm + j, 0)),
                  _const_spec((1, D_MODEL)), _const_spec((D_MODEL, D_MODEL)), _const_spec((1, D_MODEL)),
                  _const_spec((SEG_W, SEG_W)), _const_spec((F_HEADS // 2, 2 * LANES, 2 * LANES))],
        out_specs=pl.BlockSpec((tm, wide), lambda b, j: (b * spb + j, 0)),
        out_shape=jax.ShapeDtypeStruct((batch * seq, wide), _bf16),
        compiler_params=_params("parallel", "parallel"),
        name="attn_q_proj",
    )(h, gain.reshape(1, D_MODEL), w_q.astype(_bf16), jnp.tile(g_q, F_HEADS).reshape(1, D_MODEL), _segment_ones(),
      _head_place_matrices())
    return qa.reshape(batch, seq, wide)


def _attn_kernel(q_ref, k_ref, vt_ref, out_ref, sa_ref, sb_ref, xa_ref, xb_ref, m_ref, acc_ref):
    iq = pl.program_id(2)
    tq, tk = ATT_TQ, ATT_TK
    m_ref[...] = jnp.full(m_ref.shape, NEG, _f32)
    acc_ref[...] = jnp.zeros_like(acc_ref)

    def scores(ks, rows, hh):
        return _dot_nt(k_ref[pl.ds(ks, rows), hh * LANES:(hh + 1) * LANES], q_ref[:, hh * LANES:(hh + 1) * LANES])

    def issue_head(jk, s_ref, x_ref, hh):
        st = scores(pl.multiple_of(jk * tk, tk), tk, hh)
        s_ref[hh] = st
        x_ref[hh] = jnp.max(st, axis=0, keepdims=True)

    def issue(jk, s_ref, x_ref):
        for hh in range(2):
            issue_head(jk, s_ref, x_ref, hh)

    def consume(st, st_max, ks, rows, hh):
        m_prev = m_ref[hh]
        m_new = jnp.maximum(m_prev, st_max)
        alpha = jnp.exp2(m_prev - m_new)
        pt = jnp.exp2((st - m_new).astype(_bf16))
        acc_ref[hh] = alpha * acc_ref[hh] + _dot(vt_ref[hh, :, pl.ds(ks, rows)], pt)
        m_ref[hh] = m_new

    def consume_masked(st, mask, ks, rows, hh):
        st = jnp.where(mask, st, NEG)
        consume(st, jnp.max(st, axis=0, keepdims=True), ks, rows, hh)

    def consume_block(jk, s_ref, x_ref, mask=None):
        ks = pl.multiple_of(jk * tk, tk)
        for hh in range(2):
            if mask is None:
                consume(s_ref[hh], x_ref[hh], ks, tk, hh)
            else:
                consume_masked(s_ref[hh], mask, ks, tk, hh)

    def step(j_issue, si_ref, xi_ref, j_cons, sc_ref, xc_ref):
        ks = pl.multiple_of(j_cons * tk, tk)
        for hh in range(2):
            issue_head(j_issue, si_ref, xi_ref, hh)
            consume(sc_ref[hh], xc_ref[hh], ks, tk, hh)

    k_idx = lax.broadcasted_iota(jnp.int32, (tk, tq), 0)
    q_idx = lax.broadcasted_iota(jnp.int32, (tk, tq), 1)
    causal = k_idx <= q_idx

    meta_rows = LANES
    key_id = lax.broadcasted_iota(jnp.int32, (meta_rows, tq), 0)
    st_meta = [scores(REAL0 - meta_rows, meta_rows, hh) for hh in range(2)]
    issue(1, sa_ref, xa_ref)
    for hh in range(2):
        consume_masked(st_meta[hh], key_id >= meta_rows - N_META, REAL0 - meta_rows, meta_rows, hh)

    n_pairs = iq // 2

    def body(t, carry):
        j = 2 * t + 1
        step(j + 1, sb_ref, xb_ref, j, sa_ref, xa_ref)
        step(j + 2, sa_ref, xa_ref, j + 1, sb_ref, xb_ref)
        return carry

    lax.fori_loop(0, n_pairs, body, 0)
    j_last = 2 * n_pairs + 1

    @pl.when(iq % 2 == 0)
    def _():
        consume_block(j_last, sa_ref, xa_ref, causal)

    @pl.when(iq % 2 == 1)
    def _():
        issue(j_last + 1, sb_ref, xb_ref)
        consume_block(j_last, sa_ref, xa_ref)
        consume_block(j_last + 1, sb_ref, xb_ref, causal)

    ot = jnp.concatenate([acc_ref[hh, :F_HD, :] / acc_ref[hh, F_HD:F_HD + 1, :] for hh in range(2)], axis=0)
    out_ref[...] = ot.T.astype(out_ref.dtype)


def _attention(q, k, vt, batch, seq, batch_rows):
    assert ATT_TQ == ATT_TK and REAL0 % ATT_TK == 0
    pairs = F_HEADS // 2
    nq = seq // ATT_TQ
    return pl.pallas_call(
        _attn_kernel,
        grid=(batch, pairs, nq),
        in_specs=[pl.BlockSpec((None, ATT_TQ, 2 * LANES), lambda b, p, i: (b, i, p)),
                  pl.BlockSpec((None, batch_rows, 2 * LANES), lambda b, p, i: (b, 0, p)),
                  pl.BlockSpec((None, 2, V_ROWS, batch_rows), lambda b, p, i: (b, p, 0, 0))],
        out_specs=pl.BlockSpec((None, ATT_TQ, LANES), lambda b, p, i: (b, i, p)),
        out_shape=jax.ShapeDtypeStruct((batch, seq, D_MODEL), _bf16),
        scratch_shapes=[pltpu.VMEM((2, ATT_TK, ATT_TQ), _f32), pltpu.VMEM((2, ATT_TK, ATT_TQ), _f32),
                        pltpu.VMEM((2, 1, ATT_TQ), _f32), pltpu.VMEM((2, 1, ATT_TQ), _f32),
                        pltpu.VMEM((2, 1, ATT_TQ), _f32), pltpu.VMEM((2, V_ROWS, ATT_TQ), _f32)],
        compiler_params=_params("parallel", "parallel", "arbitrary"),
        name="fox_attention",
    )(q, k, vt)


def kernel(x, meta_tokens, norm_mix, norm_ffn, m_w_in, m_b_gate, m_g_out, m_w_out, kv_norm, kv_w, kv_b_f, k_norm,
           f_w_q, f_q_norm, f_w_o, moe_w_group, moe_b_group, moe_w_router, moe_b_router, moe_w_gate, moe_w_up,
           moe_w_down):
    B, S, D = x.shape
    depth = norm_mix.shape[0]
    n_a = m_w_in.shape[0]
    LP = REAL0 + S
    assert D == D_MODEL and S % ATT_TQ == 0 and LP % ROW_TILE == 0 and meta_tokens.shape[0] == N_META
    assert (B * LP) % MOE_ROW_TILE == 0 and (B * S) % MOE_ROW_TILE == 0
    assert depth > n_a and REAL0 % M_CHUNK == 0 and MLSTM_ROWS % M_CHUNK == 0
    meta = jnp.broadcast_to(meta_tokens[None].astype(x.dtype), (B, N_META, D))
    h = jnp.concatenate([jnp.zeros((B, PAD_FRONT, D), x.dtype), meta, x], axis=1).reshape(B * LP, D)
    k_sh = vt_sh = None
    for l in range(depth):
        if l < n_a:
            q, kt, v, o, u, bcol = _mlstm_in_proj(h, norm_mix[l], m_w_in[l], m_b_gate[l], LP)
            y = _mlstm_core(q, kt, v, o, u.T, bcol, m_g_out[l].reshape(-1), B, LP)
            h = _linear_residual(y, m_w_out[l], h, B, LP, LP, 0)
        else:
            if l == n_a:
                k_sh, vt_sh = _shared_kv(h, kv_norm, kv_w, kv_b_f, k_norm, B, LP)
                rows_per_batch, row0 = LP, REAL0
            j = l - n_a
            q = _q_proj(h, norm_mix[l], f_w_q[j], f_q_norm[j], B, S, rows_per_batch, row0)
            att = _attention(q, k_sh, vt_sh, B, S, LP)
            h = _linear_residual(att.reshape(B * S, D), f_w_o[j], h, B, S, rows_per_batch, row0)
            rows_per_batch, row0 = S, 0
        h = _moe(h, norm_ffn[l], moe_w_group[l], moe_b_group[l], moe_w_router[l], moe_b_router[l],
                 moe_w_gate[l], moe_w_up[l], moe_w_down[l])
    return h.reshape(B, S, D)
```

```python
import functools

import jax
import numpy as np
import jax.numpy as jnp
from jax import lax
from jax.experimental import pallas as pl
from jax.experimental.pallas import tpu as pltpu

D_MODEL = 1024
N_META = 16
M_HEADS = 4
M_DV = D_MODEL // M_HEADS
M_DK = M_DV // 2
M_CHUNK = 256
GATE_CAP = 15.0
F_HEADS = 16
F_HD = D_MODEL // F_HEADS
N_GROUPS = 4
EXP_PER_GROUP = 4
N_EXPERTS = N_GROUPS * EXP_PER_GROUP
D_EXPERT = D_MODEL // 4
EPS = 1e-6
NEG = -1e30

LANES = 128
ROW_TILE = 512
MOE_ROW_TILE = 1024
REAL0 = 512
PAD_FRONT = REAL0 - N_META
MLSTM_ROWS = 256
V_ROWS = F_HD + 16
ATT_TQ = 512
ATT_TK = 512
LOG2E = 1.4426950408889634
VMEM_LIMIT = 56 * 1024 * 1024

_f32 = jnp.float32
_bf16 = jnp.bfloat16


def _dot(a, b):
    return jnp.dot(a, b, preferred_element_type=_f32)


def _split2(w):
    hi = w.astype(_bf16)
    return hi, (w - hi.astype(_f32)).astype(_bf16)


def _dot_x3(x, w_hi, w_lo):
    x_hi, x_lo = _split2(x)
    return _dot(x_hi, w_hi) + _dot(x_lo, w_hi) + _dot(x_hi, w_lo)


def _dot_ones3(ones_mat, x):
    x1 = x.astype(_bf16)
    r1 = x - x1.astype(_f32)
    x2 = r1.astype(_bf16)
    x3 = (r1 - x2.astype(_f32)).astype(_bf16)
    return _dot(ones_mat, x1) + _dot(ones_mat, x2) + _dot(ones_mat, x3)


def _dot_nt(a, b):
    return lax.dot_general(a, b, (((1,), (1,)), ((), ())), preferred_element_type=_f32)


def _rmsnorm(x, g):
    return x * lax.rsqrt(jnp.mean(x * x, axis=-1, keepdims=True) + EPS) * g


def _log_sigmoid(x):
    return jnp.minimum(x, 0.0) - jnp.log1p(jnp.exp(-jnp.abs(x)))


def _params(*sem):
    return pltpu.CompilerParams(dimension_semantics=sem, vmem_limit_bytes=VMEM_LIMIT)


def _const_spec(shape):
    return pl.BlockSpec(shape, lambda *_: (0,) * len(shape))


def _mlstm_in_kernel(blocks_per_batch, h_ref, g_ref, wq_ref, wkt_ref, wv_ref, wo_ref, wg_hi_ref, wg_lo_ref,
                     bgate_ref, q_ref, kt_ref, v_ref, o_ref, u_ref, b_ref):
    i = pl.program_id(0)
    x = h_ref[...]
    xn = _rmsnorm(x, g_ref[...])
    xb = xn.astype(_bf16)
    q_ref[...] = (_dot(xb, wq_ref[...]) * (M_DK ** -0.5)).astype(_bf16)
    kt_ref[...] = _dot_nt(wkt_ref[...], xb).astype(_bf16)
    v_ref[...] = _dot(xb, wv_ref[...]).astype(_bf16)
    o_ref[...] = jax.nn.sigmoid(_dot(xb, wo_ref[...])).astype(_bf16)
    gates = _dot_x3(xn, wg_hi_ref[...], wg_lo_ref[...]) + bgate_ref[...]
    gates = GATE_CAP * jnp.tanh(gates * (1.0 / GATE_CAP))
    rows = x.shape[0]
    row_in_batch = (i % blocks_per_batch) * rows + lax.broadcasted_iota(jnp.int32, (rows, LANES), 0)
    is_pad = row_in_batch < PAD_FRONT
    li = jnp.where(is_pad, NEG, gates)
    lf = jnp.where(is_pad, 0.0, _log_sigmoid(gates))
    r = lax.broadcasted_iota(jnp.int32, (rows, rows), 0)
    c = lax.broadcasted_iota(jnp.int32, (rows, rows), 1)
    chunk_tril = jnp.where((r // M_CHUNK == c // M_CHUNK) & (c <= r), 1.0, 0.0).astype(_bf16)
    b = _dot_ones3(chunk_tril, lf)
    b = pltpu.roll(b, LANES - M_HEADS, 1)
    u_ref[...] = (li - b)[:, :8]
    b_ref[...] = b[:, :8]


def _mlstm_in_proj(hp, gain, w_in, b_gate, batch_rows):
    T = hp.shape[0]
    qk_w = M_HEADS * M_DK
    wq = w_in[:, :qk_w].astype(_bf16)
    wkt = w_in[:, qk_w:2 * qk_w].T.astype(_bf16)
    wv = w_in[:, 2 * qk_w:2 * qk_w + D_MODEL].astype(_bf16)
    wo = w_in[:, 2 * qk_w + D_MODEL:2 * qk_w + 2 * D_MODEL].astype(_bf16)
    n_gate = 2 * M_HEADS
    wg_hi, wg_lo = _split2(jnp.zeros((D_MODEL, LANES), _f32).at[:, :n_gate].set(w_in[:, 2 * qk_w + 2 * D_MODEL:]))
    bgate = jnp.zeros((1, LANES), _f32).at[0, :n_gate].set(b_gate)
    tm = ROW_TILE
    row = lambda n: pl.BlockSpec((tm, n), lambda i: (i, 0))
    return pl.pallas_call(
        functools.partial(_mlstm_in_kernel, batch_rows // tm),
        grid=(T // tm,),
        in_specs=[row(D_MODEL), _const_spec((1, D_MODEL)), _const_spec((D_MODEL, qk_w)),
                  _const_spec((qk_w, D_MODEL)), _const_spec((D_MODEL, D_MODEL)), _const_spec((D_MODEL, D_MODEL)),
                  _const_spec((D_MODEL, LANES)), _const_spec((D_MODEL, LANES)), _const_spec((1, LANES))],
        out_specs=[row(qk_w), pl.BlockSpec((qk_w, tm), lambda i: (0, i)), row(D_MODEL), row(D_MODEL),
                   row(8), row(8)],
        out_shape=[jax.ShapeDtypeStruct((T, qk_w), _bf16), jax.ShapeDtypeStruct((qk_w, T), _bf16),
                   jax.ShapeDtypeStruct((T, D_MODEL), _bf16), jax.ShapeDtypeStruct((T, D_MODEL), _bf16),
                   jax.ShapeDtypeStruct((T, 8), _f32), jax.ShapeDtypeStruct((T, 8), _f32)],
        compiler_params=_params("parallel"),
        name="mlstm_in_proj",
    )(hp, gain.reshape(1, D_MODEL), wq, wkt, wv, wo, wg_hi, wg_lo, bgate)


def _mlstm_core_kernel(q_ref, kt_ref, v_ref, o_ref, ut_ref, b_ref, gout_ref, out_ref, state_ref, m_ref):
    C = M_CHUNK
    n_chunks = q_ref.shape[0] // C

    @pl.when(pl.program_id(1) == 0)
    def _():
        state_ref[...] = jnp.zeros_like(state_ref)
        m_ref[...] = jnp.zeros_like(m_ref)

    r = lax.broadcasted_iota(jnp.int32, (C, C), 0)
    c = lax.broadcasted_iota(jnp.int32, (C, C), 1)
    causal = c <= r
    ones_col = jnp.where(lax.broadcasted_iota(jnp.int32, (C, LANES), 1) == 0, 1.0, 0.0).astype(_bf16)

    for h in range(M_HEADS):
        S = state_ref[h]
        m = m_ref[h][0:1, 0:1]
        for ci in range(n_chunks):
            rows = slice(ci * C, (ci + 1) * C)
            qc = q_ref[rows, h * M_DK:(h + 1) * M_DK]
            ktc = kt_ref[h * M_DK:(h + 1) * M_DK, rows]
            vc = v_ref[rows, h * M_DV:(h + 1) * M_DV]
            u_row = ut_ref[h:h + 1, rows]
            b_col = b_ref[rows, h:h + 1]
            v_aug = jnp.concatenate([vc, ones_col], axis=1)
            cu_col = jnp.max(jnp.where(causal, u_row, NEG), axis=1, keepdims=True)
            M = jnp.maximum(cu_col, m)
            E = jnp.where(causal, jnp.exp(jnp.where(causal, u_row - M, 0.0)), 0.0)
            inter = jnp.exp(m - M)
            s_mat = (_dot(qc, ktc) * E).astype(_bf16)
            tot = _dot(s_mat, v_aug) + inter * _dot(qc, S.astype(_bf16))
            num = tot[:, :M_DV]
            den = tot[:, M_DV:M_DV + 1]
            hout = num / jnp.maximum(jnp.abs(den), jnp.exp(-(b_col + M)))
            y = hout * lax.rsqrt(jnp.mean(hout * hout, axis=-1, keepdims=True) + EPS)
            y = y * gout_ref[:, h * M_DV:(h + 1) * M_DV] * o_ref[rows, h * M_DV:(h + 1) * M_DV].astype(_f32)
            out_ref[rows, h * M_DV:(h + 1) * M_DV] = y.astype(out_ref.dtype)
            g_tot = b_col[C - 1:C, :]
            cu_last = cu_col[C - 1:C, :]
            wts_row = jnp.exp(u_row - cu_last)
            m_loc = g_tot + cu_last
            m_new = jnp.maximum(g_tot + m, m_loc)
            ktw = (ktc.astype(_f32) * wts_row).astype(_bf16)
            kv = _dot(ktw, v_aug)
            S = jnp.exp(g_tot + m - m_new) * S + jnp.exp(m_loc - m_new) * kv
            m = m_new
        state_ref[h] = S
        m_ref[h] = jnp.broadcast_to(m, m_ref.shape[1:])


def _mlstm_core(q, kt, v, o, ut, bcol, g_out, batch, batch_rows):
    R = MLSTM_ROWS
    spb = batch_rows // R
    row = lambda n: pl.BlockSpec((R, n), lambda b, s: (b * spb + s, 0))
    col = lambda n: pl.BlockSpec((n, R), lambda b, s: (0, b * spb + s))
    qk_w = M_HEADS * M_DK
    return pl.pallas_call(
        _mlstm_core_kernel,
        grid=(batch, spb),
        in_specs=[row(qk_w), col(qk_w), row(D_MODEL), row(D_MODEL), col(8), row(8), _const_spec((1, D_MODEL))],
        out_specs=row(D_MODEL),
        out_shape=jax.ShapeDtypeStruct((batch * batch_rows, D_MODEL), _bf16),
        scratch_shapes=[pltpu.VMEM((M_HEADS, M_DK, M_DV + LANES), _f32), pltpu.VMEM((M_HEADS, 8, LANES), _f32)],
        compiler_params=_params("parallel", "arbitrary"),
        name="mlstm_core",
    )(q, kt, v, o, ut, bcol, g_out.reshape(1, D_MODEL))


def _route(logits):
    lane = lax.broadcasted_iota(jnp.int32, logits.shape, 1)
    is_g = lane < N_GROUPS
    gl = jnp.where(is_g, logits, -jnp.inf)
    gmax = jnp.max(gl, axis=1, keepdims=True)
    top_g = jnp.min(jnp.where(gl == gmax, lane, LANES), axis=1, keepdims=True)
    p_g = 1.0 / jnp.sum(jnp.where(is_g, jnp.exp(gl - gmax), 0.0), axis=1, keepdims=True)
    e_lane = lane - N_GROUPS
    in_group = (e_lane >= top_g * EXP_PER_GROUP) & (e_lane < (top_g + 1) * EXP_PER_GROUP)
    el = jnp.where(in_group, logits, -jnp.inf)
    v1 = jnp.max(el, axis=1, keepdims=True)
    i1 = jnp.min(jnp.where(el == v1, lane, LANES), axis=1, keepdims=True)
    el2 = jnp.where(lane == i1, -jnp.inf, el)
    v2 = jnp.max(el2, axis=1, keepdims=True)
    i2 = jnp.min(jnp.where(el2 == v2, lane, LANES), axis=1, keepdims=True)
    e2 = jnp.exp(v2 - v1)
    p1 = p_g / (1.0 + e2)
    p2 = p_g * e2 / (1.0 + e2)
    return jnp.where(lane == i1, p1, jnp.where(lane == i2, p2, 0.0))


def _moe_kernel(y_ref, wo_ref, res_lo_ref, res_hi_ref, g_ref, wr_hi_ref, wr_lo_ref, br_ref, wg_ref, wu_ref, wd_ref,
                out_ref, xn_ref, gate_ref):
    grp = pl.program_id(1)

    @pl.when(grp == 0)
    def _():
        res = jnp.concatenate([res_lo_ref[...], res_hi_ref[...]], axis=0)
        x = res + _dot(y_ref[...], wo_ref[...])
        xn = _rmsnorm(x, g_ref[...])
        xn_ref[...] = xn.astype(_bf16)
        gate_ref[...] = _route(_dot_x3(xn, wr_hi_ref[...], wr_lo_ref[...]) + br_ref[...])
        out_ref[...] = x

    xb = xn_ref[...]
    gates = gate_ref[...]
    lane = lax.broadcasted_iota(jnp.int32, gates.shape, 1)
    acc = jnp.zeros(out_ref.shape, _f32)
    for e in range(EXP_PER_GROUP):
        gate_e = jnp.sum(jnp.where(lane == N_GROUPS + grp * EXP_PER_GROUP + e, gates, 0.0), axis=1, keepdims=True)
        hg = _dot(xb, wg_ref[e])
        hu = _dot(xb, wu_ref[e])
        act = (hg * jax.nn.sigmoid(hg) * hu * gate_e).astype(_bf16)
        acc = acc + _dot(act, wd_ref[e])
    out_ref[...] += acc


def _mix_out_moe(y, w_out, res, res_batch_rows, res_row0, seq, gain, w_group, b_group, w_router, b_router,
                 w_gate, w_up, w_down):
    T = y.shape[0]
    tm = MOE_ROW_TILE
    half = tm // 2
    assert T % tm == 0 and seq % half == 0 and res_batch_rows % half == 0 and res_row0 % half == 0
    wr = jnp.zeros((D_MODEL, LANES), _f32)
    wr_hi, wr_lo = _split2(wr.at[:, :N_GROUPS].set(w_group).at[:, N_GROUPS:N_GROUPS + N_EXPERTS].set(w_router))
    br = jnp.zeros((1, LANES), _f32)
    br = br.at[0, :N_GROUPS].set(b_group).at[0, N_GROUPS:N_GROUPS + N_EXPERTS].set(b_router)
    row = pl.BlockSpec((tm, D_MODEL), lambda i, g: (i, 0))

    def res_half(k):
        def index(i, g):
            r = 2 * i + k
            b = r // (seq // half)
            return (b * (res_batch_rows // half) + res_row0 // half + r - b * (seq // half), 0)
        return pl.BlockSpec((half, D_MODEL), index)

    wspec_in = pl.BlockSpec((EXP_PER_GROUP, D_MODEL, D_EXPERT), lambda i, g: (g, 0, 0))
    wspec_out = pl.BlockSpec((EXP_PER_GROUP, D_EXPERT, D_MODEL), lambda i, g: (g, 0, 0))
    return pl.pallas_call(
        _moe_kernel,
        grid=(T // tm, N_GROUPS),
        in_specs=[row, _const_spec((D_MODEL, D_MODEL)), res_half(0), res_half(1), _const_spec((1, D_MODEL)),
                  _const_spec((D_MODEL, LANES)), _const_spec((D_MODEL, LANES)), _const_spec((1, LANES)),
                  wspec_in, wspec_in, wspec_out],
        out_specs=row,
        out_shape=jax.ShapeDtypeStruct((T, D_MODEL), _f32),
        scratch_shapes=[pltpu.VMEM((tm, D_MODEL), _bf16), pltpu.VMEM((tm, LANES), _f32)],
        compiler_params=_params("parallel", "arbitrary"),
        name="mix_out_moe",
    )(y, w_out.astype(_bf16), res, res, gain.reshape(1, D_MODEL), wr_hi, wr_lo, br, w_gate.astype(_bf16),
      w_up.astype(_bf16), w_down.astype(_bf16))


SEG_W = 256


def _segment_ones():
    return jnp.kron(jnp.eye(SEG_W // F_HD, dtype=_f32), jnp.ones((F_HD, F_HD), _f32)).astype(_bf16)


def _head_place_matrices():
    pairs = F_HEADS // 2
    place = np.zeros((pairs, 2 * LANES, 2 * LANES), np.float32)
    i = np.arange(F_HD)
    for p in range(pairs):
        place[p, i, i] = 1.0
        place[p, F_HD + i, LANES + i] = 1.0
        for hh in range(2):
            for t in range(3):
                place[p, LANES + F_HEADS * t + 2 * p + hh, hh * LANES + F_HD + t] = 1.0
    return jnp.asarray(place, _bf16)


def _heads_rmsnorm(x, seg, g):
    outs = []
    for c in range(x.shape[1] // SEG_W):
        xs = x[:, c * SEG_W:(c + 1) * SEG_W]
        sq_hi, sq_lo = _split2(xs * xs)
        ms = (_dot(sq_hi, seg) + _dot(sq_lo, seg)) * (1.0 / F_HD)
        outs.append(xs * lax.rsqrt(ms + EPS) * g[:, c * SEG_W:(c + 1) * SEG_W])
    return jnp.concatenate(outs, axis=1)


def _kv_kernel(h_ref, g_ref, wk_ref, wvt_ref, wf_hi_ref, wf_lo_ref, bf_ref, gk_ref, seg_ref, place_ref,
               k_ref, vt_ref, carry_ref):
    j = pl.program_id(1)

    @pl.when(j == 0)
    def _():
        carry_ref[...] = jnp.zeros_like(carry_ref)

    x = h_ref[...]
    rows = x.shape[0]
    xn = _rmsnorm(x, g_ref[...])
    xb = xn.astype(_bf16)
    vt = _dot_nt(wvt_ref[...], xb).astype(_bf16)
    for h in range(F_HEADS):
        vt_ref[h, :F_HD, :] = vt[h * F_HD:(h + 1) * F_HD, :]
        vt_ref[h, F_HD:, :] = jnp.ones((V_ROWS - F_HD, rows), _bf16)
    kn = _heads_rmsnorm(_dot(xb, wk_ref[...]), seg_ref[...], gk_ref[...]).astype(_bf16)
    lf = _log_sigmoid(_dot_x3(xn, wf_hi_ref[...], wf_lo_ref[...]) + bf_ref[...])
    row_in_batch = j * rows + lax.broadcasted_iota(jnp.int32, lf.shape, 0)
    lf = jnp.where(row_in_batch < PAD_FRONT, 0.0, lf)
    r = lax.broadcasted_iota(jnp.int32, (LANES, LANES), 0)
    c = lax.broadcasted_iota(jnp.int32, (LANES, LANES), 1)
    tril = jnp.where(c <= r, 1.0, 0.0).astype(_bf16)
    carry = carry_ref[0:1, :]
    terms = []
    for sb in range(rows // LANES):
        cs = _dot_ones3(tril, lf[sb * LANES:(sb + 1) * LANES, :]) + carry
        carry = cs[LANES - 1:LANES, :]
        f = (-LOG2E) * cs
        f1 = f.astype(_bf16).astype(_f32)
        f2 = (f - f1).astype(_bf16).astype(_f32)
        f3 = (f - f1) - f2
        terms.append(jnp.where(c < F_HEADS, f1, jnp.where(c < 2 * F_HEADS, pltpu.roll(f2, F_HEADS, 1),
                                                          jnp.where(c < 3 * F_HEADS, pltpu.roll(f3, 2 * F_HEADS, 1), 0.0))))
    carry_ref[...] = jnp.broadcast_to(carry, carry_ref.shape)
    fterms = jnp.concatenate(terms, axis=0).astype(_bf16)
    for p in range(F_HEADS // 2):
        lhs = jnp.concatenate([kn[:, p * LANES:(p + 1) * LANES], fterms], axis=1)
        k_ref[:, 2 * p * LANES:2 * (p + 1) * LANES] = _dot(lhs, place_ref[p]).astype(_bf16)


def _shared_kv(h, gain, w_kv, b_f, g_k, batch, batch_rows):
    tm = ROW_TILE
    spb = batch_rows // tm
    T = batch * batch_rows
    wk = w_kv[:, :D_MODEL].astype(_bf16)
    wvt = w_kv[:, D_MODEL:2 * D_MODEL].T.astype(_bf16)
    wf_hi, wf_lo = _split2(jnp.zeros((D_MODEL, LANES), _f32).at[:, :F_HEADS].set(w_kv[:, 2 * D_MODEL:]))
    bf = jnp.zeros((1, LANES), _f32).at[0, :F_HEADS].set(b_f)
    row = lambda n: pl.BlockSpec((tm, n), lambda b, j: (b * spb + j, 0))
    wide = F_HEADS * LANES
    ka, vt = pl.pallas_call(
        _kv_kernel,
        grid=(batch, spb),
        in_specs=[row(D_MODEL), _const_spec((1, D_MODEL)), _const_spec((D_MODEL, D_MODEL)),
                  _const_spec((D_MODEL, D_MODEL)), _const_spec((D_MODEL, LANES)), _const_spec((D_MODEL, LANES)),
                  _const_spec((1, LANES)), _const_spec((1, D_MODEL)), _const_spec((SEG_W, SEG_W)),
                  _const_spec((F_HEADS // 2, 2 * LANES, 2 * LANES))],
        out_specs=[row(wide), pl.BlockSpec((None, F_HEADS, V_ROWS, tm), lambda b, j: (b, 0, 0, j))],
        out_shape=[jax.ShapeDtypeStruct((T, wide), _bf16),
                   jax.ShapeDtypeStruct((batch, F_HEADS, V_ROWS, batch_rows), _bf16)],
        scratch_shapes=[pltpu.VMEM((8, LANES), _f32)],
        compiler_params=_params("parallel", "arbitrary"),
        name="shared_kv",
    )(h, gain.reshape(1, D_MODEL), wk, wvt, wf_hi, wf_lo, bf, jnp.tile(g_k, F_HEADS).reshape(1, D_MODEL),
      _segment_ones(), _head_place_matrices())
    return ka.reshape(batch, batch_rows, wide), vt


def _q_kernel(h_ref, g_ref, wq_ref, gq_ref, seg_ref, place_ref, q_ref):
    xb = _rmsnorm(h_ref[...], g_ref[...]).astype(_bf16)
    qn = _heads_rmsnorm(_dot(xb, wq_ref[...]), seg_ref[...], gq_ref[...])
    qn = (qn * (F_HD ** -0.5 * LOG2E)).astype(_bf16)
    lane = lax.broadcasted_iota(jnp.int32, (1, 2 * LANES), 1) % LANES
    ones = jnp.where((lane >= F_HD) & (lane < F_HD + 3), 1.0, 0.0)
    for p in range(F_HEADS // 2):
        spread = _dot(qn[:, p * LANES:(p + 1) * LANES], place_ref[0, :LANES, :])
        q_ref[:, 2 * p * LANES:2 * (p + 1) * LANES] = (spread + ones).astype(_bf16)


def _q_proj(h, gain, w_q, g_q, batch, seq, batch_rows, row0):
    tm = ROW_TILE
    spb = seq // tm
    wide = F_HEADS * LANES
    qa = pl.pallas_call(
        _q_kernel,
        grid=(batch, spb),
        in_specs=[pl.BlockSpec((tm, D_MODEL), lambda b, j: (b * (batch_rows // tm) + row0 // tm + j, 0)),
                  _const_spec((1, D_MODEL)), _const_spec((D_MODEL, D_MODEL)), _const_spec((1, D_MODEL)),
                  _const_spec((SEG_W, SEG_W)), _const_spec((F_HEADS // 2, 2 * LANES, 2 * LANES))],
        out_specs=pl.BlockSpec((tm, wide), lambda b, j: (b * spb + j, 0)),
        out_shape=jax.ShapeDtypeStruct((batch * seq, wide), _bf16),
        compiler_params=_params("parallel", "parallel"),
        name="attn_q_proj",
    )(h, gain.reshape(1, D_MODEL), w_q.astype(_bf16), jnp.tile(g_q, F_HEADS).reshape(1, D_MODEL), _segment_ones(),
      _head_place_matrices())
    return qa.reshape(batch, seq, wide)


def _attn_kernel(q_ref, k_ref, vt_ref, out_ref, sa_ref, sb_ref, xa_ref, xb_ref, m_ref, acc_ref):
    iq = pl.program_id(2)
    tq, tk = ATT_TQ, ATT_TK
    m_ref[...] = jnp.full(m_ref.shape, NEG, _f32)
    acc_ref[...] = jnp.zeros_like(acc_ref)

    def scores(ks, rows, hh):
        return _dot_nt(k_ref[pl.ds(ks, rows), hh * LANES:(hh + 1) * LANES], q_ref[:, hh * LANES:(hh + 1) * LANES])

    def issue_head(jk, s_ref, x_ref, hh):
        st = scores(pl.multiple_of(jk * tk, tk), tk, hh)
        s_ref[hh] = st
        x_ref[hh] = jnp.max(st, axis=0, keepdims=True)

    def issue(jk, s_ref, x_ref):
        for hh in range(2):
            issue_head(jk, s_ref, x_ref, hh)

    def consume(st, st_max, ks, rows, hh):
        m_prev = m_ref[hh]
        m_new = jnp.maximum(m_prev, st_max)
        alpha = jnp.exp2(m_prev - m_new)
        tile = 2 * LANES
        for qh in range(tq // tile):
            ql = slice(qh * tile, (qh + 1) * tile)
            pv = None
            for kh in range(rows // tile) if rows >= tile else range(1):
                kr = min(tile, rows)
                pt = jnp.exp2((st[kh * kr:(kh + 1) * kr, ql] - m_new[:, ql]).astype(_bf16))
                d = _dot(vt_ref[hh, :, pl.ds(ks + kh * kr, kr)], pt)
                pv = d if pv is None else pv + d
            acc_ref[hh, :, ql] = alpha[:, ql] * acc_ref[hh, :, ql] + pv
        m_ref[hh] = m_new

    def consume_masked(st, mask, ks, rows, hh):
        st = jnp.where(mask, st, NEG)
        consume(st, jnp.max(st, axis=0, keepdims=True), ks, rows, hh)

    def consume_block(jk, s_ref, x_ref, mask=None):
        ks = pl.multiple_of(jk * tk, tk)
        for hh in range(2):
            if mask is None:
                consume(s_ref[hh], x_ref[hh], ks, tk, hh)
            else:
                consume_masked(s_ref[hh], mask, ks, tk, hh)

    def step(j_issue, si_ref, xi_ref, j_cons, sc_ref, xc_ref):
        ks = pl.multiple_of(j_cons * tk, tk)
        for hh in range(2):
            issue_head(j_issue, si_ref, xi_ref, hh)
            consume(sc_ref[hh], xc_ref[hh], ks, tk, hh)

    k_idx = lax.broadcasted_iota(jnp.int32, (tk, tq), 0)
    q_idx = lax.broadcasted_iota(jnp.int32, (tk, tq), 1)
    causal = k_idx <= q_idx

    meta_rows = LANES
    key_id = lax.broadcasted_iota(jnp.int32, (meta_rows, tq), 0)
    st_meta = [scores(REAL0 - meta_rows, meta_rows, hh) for hh in range(2)]
    issue(1, sa_ref, xa_ref)
    for hh in range(2):
        consume_masked(st_meta[hh], key_id >= meta_rows - N_META, REAL0 - meta_rows, meta_rows, hh)

    n_pairs = iq // 2

    def body(t, carry):
        j = 2 * t + 1
        step(j + 1, sb_ref, xb_ref, j, sa_ref, xa_ref)
        step(j + 2, sa_ref, xa_ref, j + 1, sb_ref, xb_ref)
        return carry

    lax.fori_loop(0, n_pairs, body, 0)
    j_last = 2 * n_pairs + 1

    @pl.when(iq % 2 == 0)
    def _():
        consume_block(j_last, sa_ref, xa_ref, causal)

    @pl.when(iq % 2 == 1)
    def _():
        issue(j_last + 1, sb_ref, xb_ref)
        consume_block(j_last, sa_ref, xa_ref)
        consume_block(j_last + 1, sb_ref, xb_ref, causal)

    ot = jnp.concatenate([acc_ref[hh, :F_HD, :] / acc_ref[hh, F_HD:F_HD + 1, :] for hh in range(2)], axis=0)
    out_ref[...] = ot.T.astype(out_ref.dtype)


def _attention(q, k, vt, batch, seq, batch_rows):
    assert ATT_TQ == ATT_TK and REAL0 % ATT_TK == 0
    pairs = F_HEADS // 2
    nq = seq // ATT_TQ
    return pl.pallas_call(
        _attn_kernel,
        grid=(batch, pairs, nq),
        in_specs=[pl.BlockSpec((None, ATT_TQ, 2 * LANES), lambda b, p, i: (b, i, p)),
                  pl.BlockSpec((None, batch_rows, 2 * LANES), lambda b, p, i: (b, 0, p)),
                  pl.BlockSpec((None, 2, V_ROWS, batch_rows), lambda b, p, i: (b, p, 0, 0))],
        out_specs=pl.BlockSpec((None, ATT_TQ, LANES), lambda b, p, i: (b, i, p)),
        out_shape=jax.ShapeDtypeStruct((batch, seq, D_MODEL), _bf16),
        scratch_shapes=[pltpu.VMEM((2, ATT_TK, ATT_TQ), _f32), pltpu.VMEM((2, ATT_TK, ATT_TQ), _f32),
                        pltpu.VMEM((2, 1, ATT_TQ), _f32), pltpu.VMEM((2, 1, ATT_TQ), _f32),
                        pltpu.VMEM((2, 1, ATT_TQ), _f32), pltpu.VMEM((2, V_ROWS, ATT_TQ), _f32)],
        compiler_params=_params("parallel", "parallel", "arbitrary"),
        name="fox_attention",
    )(q, k, vt)


def kernel(x, meta_tokens, norm_mix, norm_ffn, m_w_in, m_b_gate, m_g_out, m_w_out, kv_norm, kv_w, kv_b_f, k_norm,
           f_w_q, f_q_norm, f_w_o, moe_w_group, moe_b_group, moe_w_router, moe_b_router, moe_w_gate, moe_w_up,
           moe_w_down):
    B, S, D = x.shape
    depth = norm_mix.shape[0]
    n_a = m_w_in.shape[0]
    LP = REAL0 + S
    assert D == D_MODEL and S % ATT_TQ == 0 and LP % ROW_TILE == 0 and meta_tokens.shape[0] == N_META
    assert (B * LP) % MOE_ROW_TILE == 0 and (B * S) % MOE_ROW_TILE == 0
    assert depth > n_a and REAL0 % M_CHUNK == 0 and MLSTM_ROWS % M_CHUNK == 0
    meta = jnp.broadcast_to(meta_tokens[None].astype(x.dtype), (B, N_META, D))
    h = jnp.concatenate([jnp.zeros((B, PAD_FRONT, D), x.dtype), meta, x], axis=1).reshape(B * LP, D)
    k_sh = vt_sh = None
    rows_per_batch, row0, seq = LP, 0, LP
    for l in range(depth):
        if l < n_a:
            q, kt, v, o, u, bcol = _mlstm_in_proj(h, norm_mix[l], m_w_in[l], m_b_gate[l], LP)
            y = _mlstm_core(q, kt, v, o, u.T, bcol, m_g_out[l].reshape(-1), B, LP)
            w_out = m_w_out[l]
        else:
            if l == n_a:
                k_sh, vt_sh = _shared_kv(h, kv_norm, kv_w, kv_b_f, k_norm, B, LP)
                row0, seq = REAL0, S
            j = l - n_a
            q = _q_proj(h, norm_mix[l], f_w_q[j], f_q_norm[j], B, S, rows_per_batch, row0)
            y = _attention(q, k_sh, vt_sh, B, S, LP).reshape(B * S, D)
            w_out = f_w_o[j]
        h = _mix_out_moe(y, w_out, h, rows_per_batch, row0, seq, norm_ffn[l], moe_w_group[l], moe_b_group[l],
                         moe_w_router[l], moe_b_router[l], moe_w_gate[l], moe_w_up[l], moe_w_down[l])
        rows_per_batch, row0 = seq, 0
    return h.reshape(B, S, D)
```

```python
import functools

import jax
import numpy as np
import jax.numpy as jnp
from jax import lax
from jax.experimental import pallas as pl
from jax.experimental.pallas import tpu as pltpu

D_MODEL = 1024
N_META = 16
M_HEADS = 4
M_DV = D_MODEL // M_HEADS
M_DK = M_DV // 2
M_CHUNK = 256
GATE_CAP = 15.0
F_HEADS = 16
F_HD = D_MODEL // F_HEADS
N_GROUPS = 4
EXP_PER_GROUP = 4
N_EXPERTS = N_GROUPS * EXP_PER_GROUP
D_EXPERT = D_MODEL // 4
EPS = 1e-6
NEG = -1e30

LANES = 128
ROW_TILE = 512
IN_ROW_TILE = 1024
MOE_ROW_TILE = 1024
REAL0 = 512
PAD_FRONT = REAL0 - N_META
MLSTM_ROWS = 512
V_ROWS = F_HD + 16
ATT_TQ = 512
ATT_TK = 512
LOG2E = 1.4426950408889634
VMEM_LIMIT = 56 * 1024 * 1024

_f32 = jnp.float32
_bf16 = jnp.bfloat16


def _dot(a, b):
    return jnp.dot(a, b, preferred_element_type=_f32)


def _split2(w):
    hi = w.astype(_bf16)
    return hi, (w - hi.astype(_f32)).astype(_bf16)


def _dot_x3(x, w_hi, w_lo):
    x_hi, x_lo = _split2(x)
    return _dot(x_hi, w_hi) + _dot(x_lo, w_hi) + _dot(x_hi, w_lo)


def _dot_ones3(ones_mat, x):
    x1 = x.astype(_bf16)
    r1 = x - x1.astype(_f32)
    x2 = r1.astype(_bf16)
    x3 = (r1 - x2.astype(_f32)).astype(_bf16)
    return _dot(ones_mat, x1) + _dot(ones_mat, x2) + _dot(ones_mat, x3)


def _dot_nt(a, b):
    return lax.dot_general(a, b, (((1,), (1,)), ((), ())), preferred_element_type=_f32)


def _rmsnorm(x, g):
    return x * lax.rsqrt(jnp.mean(x * x, axis=-1, keepdims=True) + EPS) * g


def _log_sigmoid(x):
    return jnp.minimum(x, 0.0) - jnp.log1p(jnp.exp(-jnp.abs(x)))


def _params(*sem):
    return pltpu.CompilerParams(dimension_semantics=sem, vmem_limit_bytes=VMEM_LIMIT)


def _const_spec(shape):
    return pl.BlockSpec(shape, lambda *_: (0,) * len(shape))


def _mlstm_in_kernel(n_batch, batch_rows, x_lo_ref, x_hi_ref, head_ref, g_ref, wq_ref, wkt_ref, wv_ref, wo_ref,
                     wg_hi_ref, wg_lo_ref, bgate_ref, h_ref, q_ref, kt_ref, v_ref, o_ref, u_ref, b_ref):
    i = pl.program_id(0)
    half = x_lo_ref.shape[0]
    halves_per_batch = batch_rows // half
    parts = []
    for k, x_ref in enumerate((x_lo_ref, x_hi_ref)):
        is_head = lax.rem(2 * i + k, halves_per_batch) == 0
        parts.append(jnp.where(is_head, head_ref[...], x_ref[...]))
    x = jnp.concatenate(parts, axis=0)
    h_ref[...] = x
    xn = _rmsnorm(x, g_ref[...])
    xb = xn.astype(_bf16)
    q_ref[...] = (_dot(xb, wq_ref[...]) * (M_DK ** -0.5)).astype(_bf16)
    kt_ref[...] = _dot_nt(wkt_ref[...], xb).astype(_bf16)
    v_ref[...] = _dot(xb, wv_ref[...]).astype(_bf16)
    o_ref[...] = jax.nn.sigmoid(_dot(xb, wo_ref[...])).astype(_bf16)
    gates = _dot_x3(xn, wg_hi_ref[...], wg_lo_ref[...]) + bgate_ref[...]
    gates = GATE_CAP * jnp.tanh(gates * (1.0 / GATE_CAP))
    rows = x.shape[0]
    row = i * rows + lax.broadcasted_iota(jnp.int32, (rows, LANES), 0)
    is_pad = row < PAD_FRONT
    for bb in range(1, n_batch):
        is_pad = is_pad | ((row >= bb * batch_rows) & (row < bb * batch_rows + PAD_FRONT))
    li = jnp.where(is_pad, NEG, gates)
    lf = jnp.where(is_pad, 0.0, _log_sigmoid(gates))
    r = lax.broadcasted_iota(jnp.int32, (M_CHUNK, M_CHUNK), 0)
    c = lax.broadcasted_iota(jnp.int32, (M_CHUNK, M_CHUNK), 1)
    tril = jnp.where(c <= r, 1.0, 0.0).astype(_bf16)
    b = jnp.concatenate([_dot_ones3(tril, lf[k * M_CHUNK:(k + 1) * M_CHUNK, :]) for k in range(rows // M_CHUNK)], axis=0)
    b = pltpu.roll(b, LANES - M_HEADS, 1)
    u_ref[...] = (li - b)[:, :8]
    b_ref[...] = b[:, :8]


def _mlstm_in_proj(x, head, gain, w_in, b_gate, batch_rows):
    B, S, _ = x.shape
    T = B * batch_rows
    tm = IN_ROW_TILE
    half = tm // 2
    assert half == REAL0 and T % tm == 0 and S % half == 0 and batch_rows == REAL0 + S
    qk_w = M_HEADS * M_DK
    wq = w_in[:, :qk_w].astype(_bf16)
    wkt = w_in[:, qk_w:2 * qk_w].T.astype(_bf16)
    wv = w_in[:, 2 * qk_w:2 * qk_w + D_MODEL].astype(_bf16)
    wo = w_in[:, 2 * qk_w + D_MODEL:2 * qk_w + 2 * D_MODEL].astype(_bf16)
    n_gate = 2 * M_HEADS
    wg_hi, wg_lo = _split2(jnp.zeros((D_MODEL, LANES), _f32).at[:, :n_gate].set(w_in[:, 2 * qk_w + 2 * D_MODEL:]))
    bgate = jnp.zeros((1, LANES), _f32).at[0, :n_gate].set(b_gate)
    row = lambda n: pl.BlockSpec((tm, n), lambda i: (i, 0))

    def x_half(k):
        def index(i):
            r = 2 * i + k
            b = r // (batch_rows // half)
            j = r - b * (batch_rows // half)
            return (b * (S // half) + jnp.maximum(j - 1, 0), 0)
        return pl.BlockSpec((half, D_MODEL), index)

    x2 = x.reshape(B * S, D_MODEL)
    return pl.pallas_call(
        functools.partial(_mlstm_in_kernel, B, batch_rows),
        grid=(T // tm,),
        in_specs=[x_half(0), x_half(1), _const_spec((half, D_MODEL)), _const_spec((1, D_MODEL)),
                  _const_spec((D_MODEL, qk_w)), _const_spec((qk_w, D_MODEL)), _const_spec((D_MODEL, D_MODEL)),
                  _const_spec((D_MODEL, D_MODEL)), _const_spec((D_MODEL, LANES)), _const_spec((D_MODEL, LANES)),
                  _const_spec((1, LANES))],
        out_specs=[row(D_MODEL), row(qk_w), pl.BlockSpec((qk_w, tm), lambda i: (0, i)), row(D_MODEL), row(D_MODEL),
                   row(8), row(8)],
        out_shape=[jax.ShapeDtypeStruct((T, D_MODEL), _f32),
                   jax.ShapeDtypeStruct((T, qk_w), _bf16), jax.ShapeDtypeStruct((qk_w, T), _bf16),
                   jax.ShapeDtypeStruct((T, D_MODEL), _bf16), jax.ShapeDtypeStruct((T, D_MODEL), _bf16),
                   jax.ShapeDtypeStruct((T, 8), _f32), jax.ShapeDtypeStruct((T, 8), _f32)],
        compiler_params=_params("parallel"),
        name="mlstm_in_proj",
    )(x2, x2, head, gain.reshape(1, D_MODEL), wq, wkt, wv, wo, wg_hi, wg_lo, bgate)


def _mlstm_core_kernel(q_ref, kt_ref, v_ref, o_ref, ut_ref, b_ref, gout_ref, out_ref, state_ref, m_ref):
    C = M_CHUNK
    n_chunks = q_ref.shape[0] // C

    @pl.when(pl.program_id(1) == 0)
    def _():
        state_ref[...] = jnp.zeros_like(state_ref)
        m_ref[...] = jnp.zeros_like(m_ref)

    r = lax.broadcasted_iota(jnp.int32, (C, C), 0)
    c = lax.broadcasted_iota(jnp.int32, (C, C), 1)
    causal = c <= r
    ones_col = jnp.where(lax.broadcasted_iota(jnp.int32, (C, LANES), 1) == 0, 1.0, 0.0).astype(_bf16)

    for h in range(M_HEADS):
        S = state_ref[h]
        m = m_ref[h][0:1, 0:1]
        for ci in range(n_chunks):
            rows = slice(ci * C, (ci + 1) * C)
            qc = q_ref[rows, h * M_DK:(h + 1) * M_DK]
            ktc = kt_ref[h * M_DK:(h + 1) * M_DK, rows]
            vc = v_ref[rows, h * M_DV:(h + 1) * M_DV]
            u_row = ut_ref[h:h + 1, rows]
            b_col = b_ref[rows, h:h + 1]
            v_aug = jnp.concatenate([vc, ones_col], axis=1)
            cu_col = jnp.max(jnp.where(causal, u_row, NEG), axis=1, keepdims=True)
            M = jnp.maximum(cu_col, m)
            E = jnp.where(causal, jnp.exp(jnp.where(causal, u_row - M, 0.0)), 0.0)
            inter = jnp.exp(m - M)
            s_mat = (_dot(qc, ktc) * E).astype(_bf16)
            tot = _dot(s_mat, v_aug) + inter * _dot(qc, S.astype(_bf16))
            num = tot[:, :M_DV]
            den = tot[:, M_DV:M_DV + 1]
            hout = num / jnp.maximum(jnp.abs(den), jnp.exp(-(b_col + M)))
            y = hout * lax.rsqrt(jnp.mean(hout * hout, axis=-1, keepdims=True) + EPS)
            y = y * gout_ref[:, h * M_DV:(h + 1) * M_DV] * o_ref[rows, h * M_DV:(h + 1) * M_DV].astype(_f32)
            out_ref[rows, h * M_DV:(h + 1) * M_DV] = y.astype(out_ref.dtype)
            g_tot = b_col[C - 1:C, :]
            cu_last = cu_col[C - 1:C, :]
            wts_row = jnp.exp(u_row - cu_last)
            m_loc = g_tot + cu_last
            m_new = jnp.maximum(g_tot + m, m_loc)
            ktw = (ktc.astype(_f32) * wts_row).astype(_bf16)
            kv = _dot(ktw, v_aug)
            S = jnp.exp(g_tot + m - m_new) * S + jnp.exp(m_loc - m_new) * kv
            m = m_new
        state_ref[h] = S
        m_ref[h] = jnp.broadcast_to(m, m_ref.shape[1:])


def _mlstm_core(q, kt, v, o, ut, bcol, g_out, batch, batch_rows):
    R = MLSTM_ROWS
    spb = batch_rows // R
    row = lambda n: pl.BlockSpec((R, n), lambda b, s: (b * spb + s, 0))
    col = lambda n: pl.BlockSpec((n, R), lambda b, s: (0, b * spb + s))
    qk_w = M_HEADS * M_DK
    return pl.pallas_call(
        _mlstm_core_kernel,
        grid=(batch, spb),
        in_specs=[row(qk_w), col(qk_w), row(D_MODEL), row(D_MODEL), col(8), row(8), _const_spec((1, D_MODEL))],
        out_specs=row(D_MODEL),
        out_shape=jax.ShapeDtypeStruct((batch * batch_rows, D_MODEL), _bf16),
        scratch_shapes=[pltpu.VMEM((M_HEADS, M_DK, M_DV + LANES), _f32), pltpu.VMEM((M_HEADS, 8, LANES), _f32)],
        compiler_params=_params("parallel", "arbitrary"),
        name="mlstm_core",
    )(q, kt, v, o, ut, bcol, g_out.reshape(1, D_MODEL))


def _route(logits):
    lane = lax.broadcasted_iota(jnp.int32, logits.shape, 1)
    is_g = lane < N_GROUPS
    gl = jnp.where(is_g, logits, -jnp.inf)
    gmax = jnp.max(gl, axis=1, keepdims=True)
    top_g = jnp.min(jnp.where(gl == gmax, lane, LANES), axis=1, keepdims=True)
    p_g = 1.0 / jnp.sum(jnp.where(is_g, jnp.exp(gl - gmax), 0.0), axis=1, keepdims=True)
    e_lane = lane - N_GROUPS
    in_group = (e_lane >= top_g * EXP_PER_GROUP) & (e_lane < (top_g + 1) * EXP_PER_GROUP)
    el = jnp.where(in_group, logits, -jnp.inf)
    v1 = jnp.max(el, axis=1, keepdims=True)
    i1 = jnp.min(jnp.where(el == v1, lane, LANES), axis=1, keepdims=True)
    el2 = jnp.where(lane == i1, -jnp.inf, el)
    v2 = jnp.max(el2, axis=1, keepdims=True)
    i2 = jnp.min(jnp.where(el2 == v2, lane, LANES), axis=1, keepdims=True)
    e2 = jnp.exp(v2 - v1)
    p1 = p_g / (1.0 + e2)
    p2 = p_g * e2 / (1.0 + e2)
    return jnp.where(lane == i1, p1, jnp.where(lane == i2, p2, 0.0))


def _moe_kernel(y_ref, wo_ref, res_lo_ref, res_hi_ref, g_ref, wr_hi_ref, wr_lo_ref, br_ref, wg_ref, wu_ref, wd_ref,
                out_ref, xn_ref, gate_ref):
    grp = pl.program_id(1)

    @pl.when(grp == 0)
    def _():
        res = jnp.concatenate([res_lo_ref[...], res_hi_ref[...]], axis=0)
        x = res + _dot(y_ref[...], wo_ref[...])
        xn = _rmsnorm(x, g_ref[...])
        xn_ref[...] = xn.astype(_bf16)
        gate_ref[...] = _route(_dot_x3(xn, wr_hi_ref[...], wr_lo_ref[...]) + br_ref[...])
        out_ref[...] = x

    xb = xn_ref[...]
    gates = gate_ref[...]
    lane = lax.broadcasted_iota(jnp.int32, gates.shape, 1)
    acc = jnp.zeros(out_ref.shape, _f32)
    for e in range(EXP_PER_GROUP):
        gate_e = jnp.sum(jnp.where(lane == N_GROUPS + grp * EXP_PER_GROUP + e, gates, 0.0), axis=1, keepdims=True)
        hg = _dot(xb, wg_ref[e])
        hu = _dot(xb, wu_ref[e])
        act = (hg * jax.nn.sigmoid(hg) * hu * gate_e).astype(_bf16)
        acc = acc + _dot(act, wd_ref[e])
    out_ref[...] += acc


def _mix_out_moe(y, w_out, res, res_batch_rows, res_row0, seq, gain, w_group, b_group, w_router, b_router,
                 w_gate, w_up, w_down):
    T = y.shape[0]
    tm = MOE_ROW_TILE
    half = tm // 2
    assert T % tm == 0 and seq % half == 0 and res_batch_rows % half == 0 and res_row0 % half == 0
    wr = jnp.zeros((D_MODEL, LANES), _f32)
    wr_hi, wr_lo = _split2(wr.at[:, :N_GROUPS].set(w_group).at[:, N_GROUPS:N_GROUPS + N_EXPERTS].set(w_router))
    br = jnp.zeros((1, LANES), _f32)
    br = br.at[0, :N_GROUPS].set(b_group).at[0, N_GROUPS:N_GROUPS + N_EXPERTS].set(b_router)
    row = pl.BlockSpec((tm, D_MODEL), lambda i, g: (i, 0))

    def res_half(k):
        def index(i, g):
            r = 2 * i + k
            b = r // (seq // half)
            return (b * (res_batch_rows // half) + res_row0 // half + r - b * (seq // half), 0)
        return pl.BlockSpec((half, D_MODEL), index)

    wspec_in = pl.BlockSpec((EXP_PER_GROUP, D_MODEL, D_EXPERT), lambda i, g: (g, 0, 0))
    wspec_out = pl.BlockSpec((EXP_PER_GROUP, D_EXPERT, D_MODEL), lambda i, g: (g, 0, 0))
    return pl.pallas_call(
        _moe_kernel,
        grid=(T // tm, N_GROUPS),
        in_specs=[row, _const_spec((D_MODEL, D_MODEL)), res_half(0), res_half(1), _const_spec((1, D_MODEL)),
                  _const_spec((D_MODEL, LANES)), _const_spec((D_MODEL, LANES)), _const_spec((1, LANES)),
                  wspec_in, wspec_in, wspec_out],
        out_specs=row,
        out_shape=jax.ShapeDtypeStruct((T, D_MODEL), _f32),
        scratch_shapes=[pltpu.VMEM((tm, D_MODEL), _bf16), pltpu.VMEM((tm, LANES), _f32)],
        compiler_params=_params("parallel", "arbitrary"),
        name="mix_out_moe",
    )(y, w_out.astype(_bf16), res, res, gain.reshape(1, D_MODEL), wr_hi, wr_lo, br, w_gate.astype(_bf16),
      w_up.astype(_bf16), w_down.astype(_bf16))


SEG_W = 256


def _segment_ones():
    return jnp.kron(jnp.eye(SEG_W // F_HD, dtype=_f32), jnp.ones((F_HD, F_HD), _f32)).astype(_bf16)


def _head_place_matrices():
    pairs = F_HEADS // 2
    place = np.zeros((pairs, 2 * LANES, 2 * LANES), np.float32)
    i = np.arange(F_HD)
    for p in range(pairs):
        place[p, i, i] = 1.0
        place[p, F_HD + i, LANES + i] = 1.0
        for hh in range(2):
            for t in range(3):
                place[p, LANES + F_HEADS * t + 2 * p + hh, hh * LANES + F_HD + t] = 1.0
    return jnp.asarray(place, _bf16)


def _heads_rmsnorm(x, seg, g):
    outs = []
    for c in range(x.shape[1] // SEG_W):
        xs = x[:, c * SEG_W:(c + 1) * SEG_W]
        sq_hi, sq_lo = _split2(xs * xs)
        ms = (_dot(sq_hi, seg) + _dot(sq_lo, seg)) * (1.0 / F_HD)
        outs.append(xs * lax.rsqrt(ms + EPS) * g[:, c * SEG_W:(c + 1) * SEG_W])
    return jnp.concatenate(outs, axis=1)


def _kv_kernel(h_ref, g_ref, wk_ref, wvt_ref, wf_hi_ref, wf_lo_ref, bf_ref, gk_ref, seg_ref, place_ref,
               k_ref, vt_ref, carry_ref):
    j = pl.program_id(1)

    @pl.when(j == 0)
    def _():
        carry_ref[...] = jnp.zeros_like(carry_ref)

    x = h_ref[...]
    rows = x.shape[0]
    xn = _rmsnorm(x, g_ref[...])
    xb = xn.astype(_bf16)
    vt = _dot_nt(wvt_ref[...], xb).astype(_bf16)
    for h in range(F_HEADS):
        vt_ref[h, :F_HD, :] = vt[h * F_HD:(h + 1) * F_HD, :]
        vt_ref[h, F_HD:, :] = jnp.ones((V_ROWS - F_HD, rows), _bf16)
    kn = _heads_rmsnorm(_dot(xb, wk_ref[...]), seg_ref[...], gk_ref[...]).astype(_bf16)
    lf = _log_sigmoid(_dot_x3(xn, wf_hi_ref[...], wf_lo_ref[...]) + bf_ref[...])
    row_in_batch = j * rows + lax.broadcasted_iota(jnp.int32, lf.shape, 0)
    lf = jnp.where(row_in_batch < PAD_FRONT, 0.0, lf)
    r = lax.broadcasted_iota(jnp.int32, (LANES, LANES), 0)
    c = lax.broadcasted_iota(jnp.int32, (LANES, LANES), 1)
    tril = jnp.where(c <= r, 1.0, 0.0).astype(_bf16)
    carry = carry_ref[0:1, :]
    terms = []
    for sb in range(rows // LANES):
        cs = _dot_ones3(tril, lf[sb * LANES:(sb + 1) * LANES, :]) + carry
        carry = cs[LANES - 1:LANES, :]
        f = (-LOG2E) * cs
        f1 = f.astype(_bf16).astype(_f32)
        f2 = (f - f1).astype(_bf16).astype(_f32)
        f3 = (f - f1) - f2
        terms.append(jnp.where(c < F_HEADS, f1, jnp.where(c < 2 * F_HEADS, pltpu.roll(f2, F_HEADS, 1),
                                                          jnp.where(c < 3 * F_HEADS, pltpu.roll(f3, 2 * F_HEADS, 1), 0.0))))
    carry_ref[...] = jnp.broadcast_to(carry, carry_ref.shape)
    fterms = jnp.concatenate(terms, axis=0).astype(_bf16)
    for p in range(F_HEADS // 2):
        lhs = jnp.concatenate([kn[:, p * LANES:(p + 1) * LANES], fterms], axis=1)
        k_ref[:, 2 * p * LANES:2 * (p + 1) * LANES] = _dot(lhs, place_ref[p]).astype(_bf16)


def _shared_kv(h, gain, w_kv, b_f, g_k, batch, batch_rows):
    tm = ROW_TILE
    spb = batch_rows // tm
    T = batch * batch_rows
    wk = w_kv[:, :D_MODEL].astype(_bf16)
    wvt = w_kv[:, D_MODEL:2 * D_MODEL].T.astype(_bf16)
    wf_hi, wf_lo = _split2(jnp.zeros((D_MODEL, LANES), _f32).at[:, :F_HEADS].set(w_kv[:, 2 * D_MODEL:]))
    bf = jnp.zeros((1, LANES), _f32).at[0, :F_HEADS].set(b_f)
    row = lambda n: pl.BlockSpec((tm, n), lambda b, j: (b * spb + j, 0))
    wide = F_HEADS * LANES
    ka, vt = pl.pallas_call(
        _kv_kernel,
        grid=(batch, spb),
        in_specs=[row(D_MODEL), _const_spec((1, D_MODEL)), _const_spec((D_MODEL, D_MODEL)),
                  _const_spec((D_MODEL, D_MODEL)), _const_spec((D_MODEL, LANES)), _const_spec((D_MODEL, LANES)),
                  _const_spec((1, LANES)), _const_spec((1, D_MODEL)), _const_spec((SEG_W, SEG_W)),
                  _const_spec((F_HEADS // 2, 2 * LANES, 2 * LANES))],
        out_specs=[row(wide), pl.BlockSpec((None, F_HEADS, V_ROWS, tm), lambda b, j: (b, 0, 0, j))],
        out_shape=[jax.ShapeDtypeStruct((T, wide), _bf16),
                   jax.ShapeDtypeStruct((batch, F_HEADS, V_ROWS, batch_rows), _bf16)],
        scratch_shapes=[pltpu.VMEM((8, LANES), _f32)],
        compiler_params=_params("parallel", "arbitrary"),
        name="shared_kv",
    )(h, gain.reshape(1, D_MODEL), wk, wvt, wf_hi, wf_lo, bf, jnp.tile(g_k, F_HEADS).reshape(1, D_MODEL),
      _segment_ones(), _head_place_matrices())
    return ka.reshape(batch, batch_rows, wide), vt


def _q_kernel(h_ref, g_ref, wq_ref, gq_ref, seg_ref, place_ref, q_ref):
    xb = _rmsnorm(h_ref[...], g_ref[...]).astype(_bf16)
    qn = _heads_rmsnorm(_dot(xb, wq_ref[...]), seg_ref[...], gq_ref[...])
    qn = (qn * (F_HD ** -0.5 * LOG2E)).astype(_bf16)
    row = lax.broadcasted_iota(jnp.int32, (2 * LANES, 1), 0) % LANES
    ones = jnp.where((row >= F_HD) & (row < F_HD + 3), 1.0, 0.0)
    spread_t = place_ref[0, :LANES, :].T
    for p in range(F_HEADS // 2):
        spread = _dot_nt(spread_t, qn[:, p * LANES:(p + 1) * LANES])
        q_ref[2 * p * LANES:2 * (p + 1) * LANES, :] = (spread + ones).astype(_bf16)


def _q_proj(h, gain, w_q, g_q, batch, seq, batch_rows, row0):
    tm = ROW_TILE
    spb = seq // tm
    wide = F_HEADS * LANES
    qa = pl.pallas_call(
        _q_kernel,
        grid=(batch, spb),
        in_specs=[pl.BlockSpec((tm, D_MODEL), lambda b, j: (b * (batch_rows // tm) + row0 // tm + j, 0)),
                  _const_spec((1, D_MODEL)), _const_spec((D_MODEL, D_MODEL)), _const_spec((1, D_MODEL)),
                  _const_spec((SEG_W, SEG_W)), _const_spec((F_HEADS // 2, 2 * LANES, 2 * LANES))],
        out_specs=pl.BlockSpec((None, wide, tm), lambda b, j: (b, 0, j)),
        out_shape=jax.ShapeDtypeStruct((batch, wide, seq), _bf16),
        compiler_params=_params("parallel", "parallel"),
        name="attn_q_proj",
    )(h, gain.reshape(1, D_MODEL), w_q.astype(_bf16), jnp.tile(g_q, F_HEADS).reshape(1, D_MODEL), _segment_ones(),
      _head_place_matrices())
    return qa


def _attn_kernel(q_ref, k_ref, vt_ref, out_ref, sa_ref, sb_ref, xa_ref, xb_ref, m_ref, acc_ref):
    iq = pl.program_id(2)
    tq, tk = ATT_TQ, ATT_TK
    m_ref[...] = jnp.full(m_ref.shape, NEG, _f32)
    acc_ref[...] = jnp.zeros_like(acc_ref)

    def scores(ks, rows, hh):
        return _dot(k_ref[pl.ds(ks, rows), hh * LANES:(hh + 1) * LANES], q_ref[hh * LANES:(hh + 1) * LANES, :])

    def issue_head(jk, s_ref, x_ref, hh):
        st = scores(pl.multiple_of(jk * tk, tk), tk, hh)
        s_ref[hh] = st
        x_ref[hh] = jnp.max(st, axis=0, keepdims=True)

    def issue(jk, s_ref, x_ref):
        for hh in range(2):
            issue_head(jk, s_ref, x_ref, hh)

    def consume(st, st_max, ks, rows, hh):
        m_prev = m_ref[hh]
        m_new = jnp.maximum(m_prev, st_max)
        alpha = jnp.exp2(m_prev - m_new)
        tile = 2 * LANES
        for qh in range(tq // tile):
            ql = slice(qh * tile, (qh + 1) * tile)
            pv = None
            for kh in range(rows // tile) if rows >= tile else range(1):
                kr = min(tile, rows)
                pt = jnp.exp2((st[kh * kr:(kh + 1) * kr, ql] - m_new[:, ql]).astype(_bf16))
                d = _dot(vt_ref[hh, :, pl.ds(ks + kh * kr, kr)], pt)
                pv = d if pv is None else pv + d
            acc_ref[hh, :, ql] = alpha[:, ql] * acc_ref[hh, :, ql] + pv
        m_ref[hh] = m_new

    def consume_masked(st, mask, ks, rows, hh):
        st = jnp.where(mask, st, NEG)
        consume(st, jnp.max(st, axis=0, keepdims=True), ks, rows, hh)

    def consume_block(jk, s_ref, x_ref, mask=None):
        ks = pl.multiple_of(jk * tk, tk)
        for hh in range(2):
            if mask is None:
                consume(s_ref[hh], x_ref[hh], ks, tk, hh)
            else:
                consume_masked(s_ref[hh], mask, ks, tk, hh)

    def step(j_issue, si_ref, xi_ref, j_cons, sc_ref, xc_ref):
        ks = pl.multiple_of(j_cons * tk, tk)
        for hh in range(2):
            issue_head(j_issue, si_ref, xi_ref, hh)
            consume(sc_ref[hh], xc_ref[hh], ks, tk, hh)

    k_idx = lax.broadcasted_iota(jnp.int32, (tk, tq), 0)
    q_idx = lax.broadcasted_iota(jnp.int32, (tk, tq), 1)
    causal = k_idx <= q_idx

    meta_rows = LANES
    key_id = lax.broadcasted_iota(jnp.int32, (meta_rows, tq), 0)
    st_meta = [scores(REAL0 - meta_rows, meta_rows, hh) for hh in range(2)]
    issue(1, sa_ref, xa_ref)
    for hh in range(2):
        consume_masked(st_meta[hh], key_id >= meta_rows - N_META, REAL0 - meta_rows, meta_rows, hh)

    n_pairs = iq // 2

    def body(t, carry):
        j = 2 * t + 1
        step(j + 1, sb_ref, xb_ref, j, sa_ref, xa_ref)
        step(j + 2, sa_ref, xa_ref, j + 1, sb_ref, xb_ref)
        return carry

    lax.fori_loop(0, n_pairs, body, 0)
    j_last = 2 * n_pairs + 1

    @pl.when(iq % 2 == 0)
    def _():
        consume_block(j_last, sa_ref, xa_ref, causal)

    @pl.when(iq % 2 == 1)
    def _():
        issue(j_last + 1, sb_ref, xb_ref)
        consume_block(j_last, sa_ref, xa_ref)
        consume_block(j_last + 1, sb_ref, xb_ref, causal)

    ot = jnp.concatenate([acc_ref[hh, :F_HD, :] / acc_ref[hh, F_HD:F_HD + 1, :] for hh in range(2)], axis=0)
    out_ref[...] = ot.T.astype(out_ref.dtype)


def _attention(q, k, vt, batch, seq, batch_rows):
    assert ATT_TQ == ATT_TK and REAL0 % ATT_TK == 0
    pairs = F_HEADS // 2
    nq = seq // ATT_TQ
    return pl.pallas_call(
        _attn_kernel,
        grid=(batch, pairs, nq),
        in_specs=[pl.BlockSpec((None, 2 * LANES, ATT_TQ), lambda b, p, i: (b, p, i)),
                  pl.BlockSpec((None, batch_rows, 2 * LANES), lambda b, p, i: (b, 0, p)),
                  pl.BlockSpec((None, 2, V_ROWS, batch_rows), lambda b, p, i: (b, p, 0, 0))],
        out_specs=pl.BlockSpec((None, ATT_TQ, LANES), lambda b, p, i: (b, i, p)),
        out_shape=jax.ShapeDtypeStruct((batch, seq, D_MODEL), _bf16),
        scratch_shapes=[pltpu.VMEM((2, ATT_TK, ATT_TQ), _f32), pltpu.VMEM((2, ATT_TK, ATT_TQ), _f32),
                        pltpu.VMEM((2, 1, ATT_TQ), _f32), pltpu.VMEM((2, 1, ATT_TQ), _f32),
                        pltpu.VMEM((2, 1, ATT_TQ), _f32), pltpu.VMEM((2, V_ROWS, ATT_TQ), _f32)],
        compiler_params=_params("parallel", "parallel", "arbitrary"),
        name="fox_attention",
    )(q, k, vt)


def kernel(x, meta_tokens, norm_mix, norm_ffn, m_w_in, m_b_gate, m_g_out, m_w_out, kv_norm, kv_w, kv_b_f, k_norm,
           f_w_q, f_q_norm, f_w_o, moe_w_group, moe_b_group, moe_w_router, moe_b_router, moe_w_gate, moe_w_up,
           moe_w_down):
    B, S, D = x.shape
    depth = norm_mix.shape[0]
    n_a = m_w_in.shape[0]
    LP = REAL0 + S
    assert D == D_MODEL and S % ATT_TQ == 0 and LP % ROW_TILE == 0 and meta_tokens.shape[0] == N_META
    assert (B * LP) % MOE_ROW_TILE == 0 and (B * S) % MOE_ROW_TILE == 0
    assert depth > n_a and REAL0 % M_CHUNK == 0 and MLSTM_ROWS % M_CHUNK == 0
    assert n_a == 1
    head = jnp.concatenate([jnp.zeros((PAD_FRONT, D), x.dtype), meta_tokens.astype(x.dtype)], axis=0)
    k_sh = vt_sh = None
    rows_per_batch, row0, seq = LP, 0, LP
    for l in range(depth):
        if l < n_a:
            h, q, kt, v, o, u, bcol = _mlstm_in_proj(x, head, norm_mix[l], m_w_in[l], m_b_gate[l], LP)
            y = _mlstm_core(q, kt, v, o, u.T, bcol, m_g_out[l].reshape(-1), B, LP)
            w_out = m_w_out[l]
        else:
            if l == n_a:
                k_sh, vt_sh = _shared_kv(h, kv_norm, kv_w, kv_b_f, k_norm, B, LP)
                row0, seq = REAL0, S
            j = l - n_a
            q = _q_proj(h, norm_mix[l], f_w_q[j], f_q_norm[j], B, S, rows_per_batch, row0)
            y = _attention(q, k_sh, vt_sh, B, S, LP).reshape(B * S, D)
            w_out = f_w_o[j]
        h = _mix_out_moe(y, w_out, h, rows_per_batch, row0, seq, norm_ffn[l], moe_w_group[l], moe_b_group[l],
                         moe_w_router[l], moe_b_router[l], moe_w_gate[l], moe_w_up[l], moe_w_down[l])
        rows_per_batch, row0 = seq, 0
    return h.reshape(B, S, D)
```

```python
import functools

import jax
import numpy as np
import jax.numpy as jnp
from jax import lax
from jax.experimental import pallas as pl
from jax.experimental.pallas import tpu as pltpu

D_MODEL = 1024
N_META = 16
M_HEADS = 4
M_DV = D_MODEL // M_HEADS
M_DK = M_DV // 2
M_CHUNK = 256
GATE_CAP = 15.0
F_HEADS = 16
F_HD = D_MODEL // F_HEADS
N_GROUPS = 4
EXP_PER_GROUP = 4
N_EXPERTS = N_GROUPS * EXP_PER_GROUP
D_EXPERT = D_MODEL // 4
EPS = 1e-6
NEG = -1e30

LANES = 128
ROW_TILE = 512
IN_ROW_TILE = 1024
MOE_ROW_TILE = 1024
REAL0 = 512
PAD_FRONT = REAL0 - N_META
MLSTM_ROWS = 512
V_ROWS = F_HD + 16
ATT_TQ = 512
ATT_TK = 512
LOG2E = 1.4426950408889634
VMEM_LIMIT = 56 * 1024 * 1024

_f32 = jnp.float32
_bf16 = jnp.bfloat16


def _dot(a, b):
    return jnp.dot(a, b, preferred_element_type=_f32)


def _split2(w):
    hi = w.astype(_bf16)
    return hi, (w - hi.astype(_f32)).astype(_bf16)


def _dot_x3(x, w_hi, w_lo):
    x_hi, x_lo = _split2(x)
    return _dot(x_hi, w_hi) + _dot(x_lo, w_hi) + _dot(x_hi, w_lo)


def _dot_ones3(ones_mat, x):
    x1 = x.astype(_bf16)
    r1 = x - x1.astype(_f32)
    x2 = r1.astype(_bf16)
    x3 = (r1 - x2.astype(_f32)).astype(_bf16)
    return _dot(ones_mat, x1) + _dot(ones_mat, x2) + _dot(ones_mat, x3)


def _dot_nt(a, b):
    return lax.dot_general(a, b, (((1,), (1,)), ((), ())), preferred_element_type=_f32)


def _rmsnorm(x, g):
    return x * lax.rsqrt(jnp.mean(x * x, axis=-1, keepdims=True) + EPS) * g


def _log_sigmoid(x):
    return jnp.minimum(x, 0.0) - jnp.log1p(jnp.exp(-jnp.abs(x)))


def _params(*sem):
    return pltpu.CompilerParams(dimension_semantics=sem, vmem_limit_bytes=VMEM_LIMIT)


def _const_spec(shape):
    return pl.BlockSpec(shape, lambda *_: (0,) * len(shape))


def _mlstm_in_kernel(n_batch, batch_rows, x_lo_ref, x_hi_ref, head_ref, g_ref, wq_ref, wkt_ref, wv_ref, wo_ref,
                     wg_hi_ref, wg_lo_ref, bgate_ref, h_ref, q_ref, kt_ref, v_ref, o_ref, u_ref, b_ref):
    i = pl.program_id(0)
    half = x_lo_ref.shape[0]
    halves_per_batch = batch_rows // half
    parts = []
    for k, x_ref in enumerate((x_lo_ref, x_hi_ref)):
        is_head = lax.rem(2 * i + k, halves_per_batch) == 0
        parts.append(jnp.where(is_head, head_ref[...], x_ref[...]))
    x = jnp.concatenate(parts, axis=0)
    h_ref[...] = x
    xn = _rmsnorm(x, g_ref[...])
    xb = xn.astype(_bf16)
    q_ref[...] = (_dot(xb, wq_ref[...]) * (M_DK ** -0.5)).astype(_bf16)
    kt_ref[...] = _dot_nt(wkt_ref[...], xb).astype(_bf16)
    v_ref[...] = _dot(xb, wv_ref[...]).astype(_bf16)
    o_ref[...] = jax.nn.sigmoid(_dot(xb, wo_ref[...])).astype(_bf16)
    gates = _dot_x3(xn, wg_hi_ref[...], wg_lo_ref[...]) + bgate_ref[...]
    gates = GATE_CAP * jnp.tanh(gates * (1.0 / GATE_CAP))
    rows = x.shape[0]
    row = i * rows + lax.broadcasted_iota(jnp.int32, (rows, LANES), 0)
    is_pad = row < PAD_FRONT
    for bb in range(1, n_batch):
        is_pad = is_pad | ((row >= bb * batch_rows) & (row < bb * batch_rows + PAD_FRONT))
    li = jnp.where(is_pad, NEG, gates)
    lf = jnp.where(is_pad, 0.0, _log_sigmoid(gates))
    r = lax.broadcasted_iota(jnp.int32, (M_CHUNK, M_CHUNK), 0)
    c = lax.broadcasted_iota(jnp.int32, (M_CHUNK, M_CHUNK), 1)
    tril = jnp.where(c <= r, 1.0, 0.0).astype(_bf16)
    b = jnp.concatenate([_dot_ones3(tril, lf[k * M_CHUNK:(k + 1) * M_CHUNK, :]) for k in range(rows // M_CHUNK)], axis=0)
    b = pltpu.roll(b, LANES - M_HEADS, 1)
    u_ref[...] = (li - b)[:, :8]
    b_ref[...] = b[:, :8]


def _mlstm_in_proj(x, head, gain, w_in, b_gate, batch_rows):
    B, S, _ = x.shape
    T = B * batch_rows
    tm = IN_ROW_TILE
    half = tm // 2
    assert half == REAL0 and T % tm == 0 and S % half == 0 and batch_rows == REAL0 + S
    qk_w = M_HEADS * M_DK
    wq = w_in[:, :qk_w].astype(_bf16)
    wkt = w_in[:, qk_w:2 * qk_w].T.astype(_bf16)
    wv = w_in[:, 2 * qk_w:2 * qk_w + D_MODEL].astype(_bf16)
    wo = w_in[:, 2 * qk_w + D_MODEL:2 * qk_w + 2 * D_MODEL].astype(_bf16)
    n_gate = 2 * M_HEADS
    wg_hi, wg_lo = _split2(jnp.zeros((D_MODEL, LANES), _f32).at[:, :n_gate].set(w_in[:, 2 * qk_w + 2 * D_MODEL:]))
    bgate = jnp.zeros((1, LANES), _f32).at[0, :n_gate].set(b_gate)
    row = lambda n: pl.BlockSpec((tm, n), lambda i: (i, 0))

    def x_half(k):
        def index(i):
            r = 2 * i + k
            b = r // (batch_rows // half)
            j = r - b * (batch_rows // half)
            return (b * (S // half) + jnp.maximum(j - 1, 0), 0)
        return pl.BlockSpec((half, D_MODEL), index)

    x2 = x.reshape(B * S, D_MODEL)
    return pl.pallas_call(
        functools.partial(_mlstm_in_kernel, B, batch_rows),
        grid=(T // tm,),
        in_specs=[x_half(0), x_half(1), _const_spec((half, D_MODEL)), _const_spec((1, D_MODEL)),
                  _const_spec((D_MODEL, qk_w)), _const_spec((qk_w, D_MODEL)), _const_spec((D_MODEL, D_MODEL)),
                  _const_spec((D_MODEL, D_MODEL)), _const_spec((D_MODEL, LANES)), _const_spec((D_MODEL, LANES)),
                  _const_spec((1, LANES))],
        out_specs=[row(D_MODEL), row(qk_w), pl.BlockSpec((qk_w, tm), lambda i: (0, i)), row(D_MODEL), row(D_MODEL),
                   row(8), row(8)],
        out_shape=[jax.ShapeDtypeStruct((T, D_MODEL), _f32),
                   jax.ShapeDtypeStruct((T, qk_w), _bf16), jax.ShapeDtypeStruct((qk_w, T), _bf16),
                   jax.ShapeDtypeStruct((T, D_MODEL), _bf16), jax.ShapeDtypeStruct((T, D_MODEL), _bf16),
                   jax.ShapeDtypeStruct((T, 8), _f32), jax.ShapeDtypeStruct((T, 8), _f32)],
        compiler_params=_params("parallel"),
        name="mlstm_in_proj",
    )(x2, x2, head, gain.reshape(1, D_MODEL), wq, wkt, wv, wo, wg_hi, wg_lo, bgate)


def _mlstm_core_kernel(q_ref, kt_ref, v_ref, o_ref, ut_ref, b_ref, gout_ref, out_ref, state_ref, m_ref):
    C = M_CHUNK
    n_chunks = q_ref.shape[0] // C

    @pl.when(pl.program_id(1) == 0)
    def _():
        state_ref[...] = jnp.zeros_like(state_ref)
        m_ref[...] = jnp.zeros_like(m_ref)

    r = lax.broadcasted_iota(jnp.int32, (C, C), 0)
    c = lax.broadcasted_iota(jnp.int32, (C, C), 1)
    causal = c <= r
    ones_col = jnp.where(lax.broadcasted_iota(jnp.int32, (C, LANES), 1) == 0, 1.0, 0.0).astype(_bf16)

    H = range(M_HEADS)
    S = [state_ref[h] for h in H]
    m = [m_ref[h][0:1, 0:1] for h in H]
    for ci in range(n_chunks):
        rows = slice(ci * C, (ci + 1) * C)
        qc = [q_ref[rows, h * M_DK:(h + 1) * M_DK] for h in H]
        ktc = [kt_ref[h * M_DK:(h + 1) * M_DK, rows] for h in H]
        v_aug = [jnp.concatenate([v_ref[rows, h * M_DV:(h + 1) * M_DV], ones_col], axis=1) for h in H]
        u_row = [ut_ref[h:h + 1, rows] for h in H]
        b_col = [b_ref[rows, h:h + 1] for h in H]
        qk = [_dot(qc[h], ktc[h]) for h in H]
        qs = [_dot(qc[h], S[h].astype(_bf16)) for h in H]
        cu_col = [jnp.max(jnp.where(causal, u_row[h], NEG), axis=1, keepdims=True) for h in H]
        M = [jnp.maximum(cu_col[h], m[h]) for h in H]
        s_mat = [(qk[h] * jnp.where(causal, jnp.exp(u_row[h] - M[h]), 0.0)).astype(_bf16) for h in H]
        tot = [_dot(s_mat[h], v_aug[h]) + jnp.exp(m[h] - M[h]) * qs[h] for h in H]
        for h in H:
            g_tot = b_col[h][C - 1:C, :]
            cu_last = cu_col[h][C - 1:C, :]
            m_loc = g_tot + cu_last
            m_new = jnp.maximum(g_tot + m[h], m_loc)
            ktw = (ktc[h].astype(_f32) * jnp.exp(u_row[h] - cu_last)).astype(_bf16)
            kv = _dot(ktw, v_aug[h])
            S[h] = jnp.exp(g_tot + m[h] - m_new) * S[h] + jnp.exp(m_loc - m_new) * kv
            m[h] = m_new
        for h in H:
            num = tot[h][:, :M_DV]
            den = tot[h][:, M_DV:M_DV + 1]
            hout = num / jnp.maximum(jnp.abs(den), jnp.exp(-(b_col[h] + M[h])))
            y = hout * lax.rsqrt(jnp.mean(hout * hout, axis=-1, keepdims=True) + EPS)
            y = y * gout_ref[:, h * M_DV:(h + 1) * M_DV] * o_ref[rows, h * M_DV:(h + 1) * M_DV].astype(_f32)
            out_ref[rows, h * M_DV:(h + 1) * M_DV] = y.astype(out_ref.dtype)
    for h in H:
        state_ref[h] = S[h]
        m_ref[h] = jnp.broadcast_to(m[h], m_ref.shape[1:])


def _mlstm_core(q, kt, v, o, ut, bcol, g_out, batch, batch_rows):
    R = MLSTM_ROWS
    spb = batch_rows // R
    row = lambda n: pl.BlockSpec((R, n), lambda b, s: (b * spb + s, 0))
    col = lambda n: pl.BlockSpec((n, R), lambda b, s: (0, b * spb + s))
    qk_w = M_HEADS * M_DK
    return pl.pallas_call(
        _mlstm_core_kernel,
        grid=(batch, spb),
        in_specs=[row(qk_w), col(qk_w), row(D_MODEL), row(D_MODEL), col(8), row(8), _const_spec((1, D_MODEL))],
        out_specs=row(D_MODEL),
        out_shape=jax.ShapeDtypeStruct((batch * batch_rows, D_MODEL), _bf16),
        scratch_shapes=[pltpu.VMEM((M_HEADS, M_DK, M_DV + LANES), _f32), pltpu.VMEM((M_HEADS, 8, LANES), _f32)],
        compiler_params=_params("parallel", "arbitrary"),
        name="mlstm_core",
    )(q, kt, v, o, ut, bcol, g_out.reshape(1, D_MODEL))


def _route(logits):
    lane = lax.broadcasted_iota(jnp.int32, logits.shape, 1)
    is_g = lane < N_GROUPS
    gl = jnp.where(is_g, logits, -jnp.inf)
    gmax = jnp.max(gl, axis=1, keepdims=True)
    top_g = jnp.min(jnp.where(gl == gmax, lane, LANES), axis=1, keepdims=True)
    p_g = 1.0 / jnp.sum(jnp.where(is_g, jnp.exp(gl - gmax), 0.0), axis=1, keepdims=True)
    e_lane = lane - N_GROUPS
    in_group = (e_lane >= top_g * EXP_PER_GROUP) & (e_lane < (top_g + 1) * EXP_PER_GROUP)
    el = jnp.where(in_group, logits, -jnp.inf)
    v1 = jnp.max(el, axis=1, keepdims=True)
    i1 = jnp.min(jnp.where(el == v1, lane, LANES), axis=1, keepdims=True)
    el2 = jnp.where(lane == i1, -jnp.inf, el)
    v2 = jnp.max(el2, axis=1, keepdims=True)
    i2 = jnp.min(jnp.where(el2 == v2, lane, LANES), axis=1, keepdims=True)
    e2 = jnp.exp(v2 - v1)
    p1 = p_g / (1.0 + e2)
    p2 = p_g * e2 / (1.0 + e2)
    return jnp.where(lane == i1, p1, jnp.where(lane == i2, p2, 0.0))


def _moe_kernel(y_ref, wo_ref, res_lo_ref, res_hi_ref, g_ref, wr_hi_ref, wr_lo_ref, br_ref, wg_ref, wu_ref, wd_ref,
                out_ref, xn_ref, gate_ref):
    grp = pl.program_id(1)

    @pl.when(grp == 0)
    def _():
        res = jnp.concatenate([res_lo_ref[...], res_hi_ref[...]], axis=0)
        x = res + _dot(y_ref[...], wo_ref[...])
        xn = _rmsnorm(x, g_ref[...])
        xn_ref[...] = xn.astype(_bf16)
        gate_ref[...] = _route(_dot_x3(xn, wr_hi_ref[...], wr_lo_ref[...]) + br_ref[...])
        out_ref[...] = x

    xb = xn_ref[...]
    gates = gate_ref[...]
    lane = lax.broadcasted_iota(jnp.int32, gates.shape, 1)
    acc = jnp.zeros(out_ref.shape, _f32)
    for e in range(EXP_PER_GROUP):
        gate_e = jnp.sum(jnp.where(lane == N_GROUPS + grp * EXP_PER_GROUP + e, gates, 0.0), axis=1, keepdims=True)
        hg = _dot(xb, wg_ref[e])
        hu = _dot(xb, wu_ref[e])
        act = (hg * jax.nn.sigmoid(hg) * hu * gate_e).astype(_bf16)
        acc = acc + _dot(act, wd_ref[e])
    out_ref[...] += acc


def _mix_out_moe(y, w_out, res, res_batch_rows, res_row0, seq, gain, w_group, b_group, w_router, b_router,
                 w_gate, w_up, w_down):
    T = y.shape[0]
    tm = MOE_ROW_TILE
    half = tm // 2
    assert T % tm == 0 and seq % half == 0 and res_batch_rows % half == 0 and res_row0 % half == 0
    wr = jnp.zeros((D_MODEL, LANES), _f32)
    wr_hi, wr_lo = _split2(wr.at[:, :N_GROUPS].set(w_group).at[:, N_GROUPS:N_GROUPS + N_EXPERTS].set(w_router))
    br = jnp.zeros((1, LANES), _f32)
    br = br.at[0, :N_GROUPS].set(b_group).at[0, N_GROUPS:N_GROUPS + N_EXPERTS].set(b_router)
    row = pl.BlockSpec((tm, D_MODEL), lambda i, g: (i, 0))

    def res_half(k):
        def index(i, g):
            r = 2 * i + k
            b = r // (seq // half)
            return (b * (res_batch_rows // half) + res_row0 // half + r - b * (seq // half), 0)
        return pl.BlockSpec((half, D_MODEL), index)

    wspec_in = pl.BlockSpec((EXP_PER_GROUP, D_MODEL, D_EXPERT), lambda i, g: (g, 0, 0))
    wspec_out = pl.BlockSpec((EXP_PER_GROUP, D_EXPERT, D_MODEL), lambda i, g: (g, 0, 0))
    return pl.pallas_call(
        _moe_kernel,
        grid=(T // tm, N_GROUPS),
        in_specs=[row, _const_spec((D_MODEL, D_MODEL)), res_half(0), res_half(1), _const_spec((1, D_MODEL)),
                  _const_spec((D_MODEL, LANES)), _const_spec((D_MODEL, LANES)), _const_spec((1, LANES)),
                  wspec_in, wspec_in, wspec_out],
        out_specs=row,
        out_shape=jax.ShapeDtypeStruct((T, D_MODEL), _f32),
        scratch_shapes=[pltpu.VMEM((tm, D_MODEL), _bf16), pltpu.VMEM((tm, LANES), _f32)],
        compiler_params=_params("parallel", "arbitrary"),
        name="mix_out_moe",
    )(y, w_out.astype(_bf16), res, res, gain.reshape(1, D_MODEL), wr_hi, wr_lo, br, w_gate.astype(_bf16),
      w_up.astype(_bf16), w_down.astype(_bf16))


SEG_W = 256


def _segment_ones():
    return jnp.kron(jnp.eye(SEG_W // F_HD, dtype=_f32), jnp.ones((F_HD, F_HD), _f32)).astype(_bf16)


def _head_place_matrices():
    pairs = F_HEADS // 2
    place = np.zeros((pairs, 2 * LANES, 2 * LANES), np.float32)
    i = np.arange(F_HD)
    for p in range(pairs):
        place[p, i, i] = 1.0
        place[p, F_HD + i, LANES + i] = 1.0
        for hh in range(2):
            for t in range(3):
                place[p, LANES + F_HEADS * t + 2 * p + hh, hh * LANES + F_HD + t] = 1.0
    return jnp.asarray(place, _bf16)


def _heads_rmsnorm(x, seg, g):
    outs = []
    for c in range(x.shape[1] // SEG_W):
        xs = x[:, c * SEG_W:(c + 1) * SEG_W]
        sq_hi, sq_lo = _split2(xs * xs)
        ms = (_dot(sq_hi, seg) + _dot(sq_lo, seg)) * (1.0 / F_HD)
        outs.append(xs * lax.rsqrt(ms + EPS) * g[:, c * SEG_W:(c + 1) * SEG_W])
    return jnp.concatenate(outs, axis=1)


def _kv_kernel(h_ref, g_ref, wk_ref, wvt_ref, wf_hi_ref, wf_lo_ref, bf_ref, gk_ref, seg_ref, place_ref,
               k_ref, vt_ref, carry_ref):
    j = pl.program_id(1)

    @pl.when(j == 0)
    def _():
        carry_ref[...] = jnp.zeros_like(carry_ref)

    x = h_ref[...]
    rows = x.shape[0]
    xn = _rmsnorm(x, g_ref[...])
    xb = xn.astype(_bf16)
    vt = _dot_nt(wvt_ref[...], xb).astype(_bf16)
    for h in range(F_HEADS):
        vt_ref[h, :F_HD, :] = vt[h * F_HD:(h + 1) * F_HD, :]
        vt_ref[h, F_HD:, :] = jnp.ones((V_ROWS - F_HD, rows), _bf16)
    kn = _heads_rmsnorm(_dot(xb, wk_ref[...]), seg_ref[...], gk_ref[...]).astype(_bf16)
    lf = _log_sigmoid(_dot_x3(xn, wf_hi_ref[...], wf_lo_ref[...]) + bf_ref[...])
    row_in_batch = j * rows + lax.broadcasted_iota(jnp.int32, lf.shape, 0)
    lf = jnp.where(row_in_batch < PAD_FRONT, 0.0, lf)
    r = lax.broadcasted_iota(jnp.int32, (LANES, LANES), 0)
    c = lax.broadcasted_iota(jnp.int32, (LANES, LANES), 1)
    tril = jnp.where(c <= r, 1.0, 0.0).astype(_bf16)
    carry = carry_ref[0:1, :]
    terms = []
    for sb in range(rows // LANES):
        cs = _dot_ones3(tril, lf[sb * LANES:(sb + 1) * LANES, :]) + carry
        carry = cs[LANES - 1:LANES, :]
        f = (-LOG2E) * cs
        f1 = f.astype(_bf16).astype(_f32)
        f2 = (f - f1).astype(_bf16).astype(_f32)
        f3 = (f - f1) - f2
        terms.append(jnp.where(c < F_HEADS, f1, jnp.where(c < 2 * F_HEADS, pltpu.roll(f2, F_HEADS, 1),
                                                          jnp.where(c < 3 * F_HEADS, pltpu.roll(f3, 2 * F_HEADS, 1), 0.0))))
    carry_ref[...] = jnp.broadcast_to(carry, carry_ref.shape)
    fterms = jnp.concatenate(terms, axis=0).astype(_bf16)
    for p in range(F_HEADS // 2):
        lhs = jnp.concatenate([kn[:, p * LANES:(p + 1) * LANES], fterms], axis=1)
        k_ref[:, 2 * p * LANES:2 * (p + 1) * LANES] = _dot(lhs, place_ref[p]).astype(_bf16)


def _shared_kv(h, gain, w_kv, b_f, g_k, batch, batch_rows):
    tm = ROW_TILE
    spb = batch_rows // tm
    T = batch * batch_rows
    wk = w_kv[:, :D_MODEL].astype(_bf16)
    wvt = w_kv[:, D_MODEL:2 * D_MODEL].T.astype(_bf16)
    wf_hi, wf_lo = _split2(jnp.zeros((D_MODEL, LANES), _f32).at[:, :F_HEADS].set(w_kv[:, 2 * D_MODEL:]))
    bf = jnp.zeros((1, LANES), _f32).at[0, :F_HEADS].set(b_f)
    row = lambda n: pl.BlockSpec((tm, n), lambda b, j: (b * spb + j, 0))
    wide = F_HEADS * LANES
    ka, vt = pl.pallas_call(
        _kv_kernel,
        grid=(batch, spb),
        in_specs=[row(D_MODEL), _const_spec((1, D_MODEL)), _const_spec((D_MODEL, D_MODEL)),
                  _const_spec((D_MODEL, D_MODEL)), _const_spec((D_MODEL, LANES)), _const_spec((D_MODEL, LANES)),
                  _const_spec((1, LANES)), _const_spec((1, D_MODEL)), _const_spec((SEG_W, SEG_W)),
                  _const_spec((F_HEADS // 2, 2 * LANES, 2 * LANES))],
        out_specs=[row(wide), pl.BlockSpec((None, F_HEADS, V_ROWS, tm), lambda b, j: (b, 0, 0, j))],
        out_shape=[jax.ShapeDtypeStruct((T, wide), _bf16),
                   jax.ShapeDtypeStruct((batch, F_HEADS, V_ROWS, batch_rows), _bf16)],
        scratch_shapes=[pltpu.VMEM((8, LANES), _f32)],
        compiler_params=_params("parallel", "arbitrary"),
        name="shared_kv",
    )(h, gain.reshape(1, D_MODEL), wk, wvt, wf_hi, wf_lo, bf, jnp.tile(g_k, F_HEADS).reshape(1, D_MODEL),
      _segment_ones(), _head_place_matrices())
    return ka.reshape(batch, batch_rows, wide), vt


def _q_kernel(h_ref, g_ref, wq_ref, gq_ref, seg_ref, place_ref, q_ref):
    xb = _rmsnorm(h_ref[...], g_ref[...]).astype(_bf16)
    qn = _heads_rmsnorm(_dot(xb, wq_ref[...]), seg_ref[...], gq_ref[...])
    qn = (qn * (F_HD ** -0.5 * LOG2E)).astype(_bf16)
    row = lax.broadcasted_iota(jnp.int32, (2 * LANES, 1), 0) % LANES
    ones = jnp.where((row >= F_HD) & (row < F_HD + 3), 1.0, 0.0)
    spread_t = place_ref[0, :LANES, :].T
    for p in range(F_HEADS // 2):
        spread = _dot_nt(spread_t, qn[:, p * LANES:(p + 1) * LANES])
        q_ref[2 * p * LANES:2 * (p + 1) * LANES, :] = (spread + ones).astype(_bf16)


def _q_proj(h, gain, w_q, g_q, batch, seq, batch_rows, row0):
    tm = ROW_TILE
    spb = seq // tm
    wide = F_HEADS * LANES
    qa = pl.pallas_call(
        _q_kernel,
        grid=(batch, spb),
        in_specs=[pl.BlockSpec((tm, D_MODEL), lambda b, j: (b * (batch_rows // tm) + row0 // tm + j, 0)),
                  _const_spec((1, D_MODEL)), _const_spec((D_MODEL, D_MODEL)), _const_spec((1, D_MODEL)),
                  _const_spec((SEG_W, SEG_W)), _const_spec((F_HEADS // 2, 2 * LANES, 2 * LANES))],
        out_specs=pl.BlockSpec((None, wide, tm), lambda b, j: (b, 0, j)),
        out_shape=jax.ShapeDtypeStruct((batch, wide, seq), _bf16),
        compiler_params=_params("parallel", "parallel"),
        name="attn_q_proj",
    )(h, gain.reshape(1, D_MODEL), w_q.astype(_bf16), jnp.tile(g_q, F_HEADS).reshape(1, D_MODEL), _segment_ones(),
      _head_place_matrices())
    return qa


def _attn_kernel(q_ref, k_ref, vt_ref, out_ref, sa_ref, sb_ref, xa_ref, xb_ref, m_ref, acc_ref):
    iq = pl.program_id(2)
    tq, tk = ATT_TQ, ATT_TK
    m_ref[...] = jnp.full(m_ref.shape, NEG, _f32)
    acc_ref[...] = jnp.zeros_like(acc_ref)

    def scores(ks, rows, hh):
        return _dot(k_ref[pl.ds(ks, rows), hh * LANES:(hh + 1) * LANES], q_ref[hh * LANES:(hh + 1) * LANES, :])

    def issue_head(jk, s_ref, x_ref, hh):
        st = scores(pl.multiple_of(jk * tk, tk), tk, hh)
        s_ref[hh] = st
        x_ref[hh] = jnp.max(st, axis=0, keepdims=True)

    def issue(jk, s_ref, x_ref):
        for hh in range(2):
            issue_head(jk, s_ref, x_ref, hh)

    def consume(st, st_max, ks, rows, hh, above_diagonal=False):
        m_prev = m_ref[hh]
        m_new = jnp.maximum(m_prev, st_max)
        alpha = jnp.exp2(m_prev - m_new)
        tile = 2 * LANES
        for qh in range(tq // tile):
            ql = slice(qh * tile, (qh + 1) * tile)
            pv = None
            for kh in range(rows // tile) if rows >= tile else range(1):
                if above_diagonal and kh > qh:
                    continue
                kr = min(tile, rows)
                pt = jnp.exp2((st[kh * kr:(kh + 1) * kr, ql] - m_new[:, ql]).astype(_bf16))
                d = _dot(vt_ref[hh, :, pl.ds(ks + kh * kr, kr)], pt)
                pv = d if pv is None else pv + d
            acc_ref[hh, :, ql] = alpha[:, ql] * acc_ref[hh, :, ql] + pv
        m_ref[hh] = m_new

    def consume_masked(st, mask, ks, rows, hh):
        st = jnp.where(mask, st, NEG)
        consume(st, jnp.max(st, axis=0, keepdims=True), ks, rows, hh, above_diagonal=True)

    def consume_block(jk, s_ref, x_ref, mask=None):
        ks = pl.multiple_of(jk * tk, tk)
        for hh in range(2):
            if mask is None:
                consume(s_ref[hh], x_ref[hh], ks, tk, hh)
            else:
                consume_masked(s_ref[hh], mask, ks, tk, hh)

    def step(j_issue, si_ref, xi_ref, j_cons, sc_ref, xc_ref):
        ks = pl.multiple_of(j_cons * tk, tk)
        for hh in range(2):
            issue_head(j_issue, si_ref, xi_ref, hh)
            consume(sc_ref[hh], xc_ref[hh], ks, tk, hh)

    k_idx = lax.broadcasted_iota(jnp.int32, (tk, tq), 0)
    q_idx = lax.broadcasted_iota(jnp.int32, (tk, tq), 1)
    causal = k_idx <= q_idx

    st_meta = [scores(PAD_FRONT, N_META, hh) for hh in range(2)]
    issue(1, sa_ref, xa_ref)
    for hh in range(2):
        consume(st_meta[hh], jnp.max(st_meta[hh], axis=0, keepdims=True), PAD_FRONT, N_META, hh)

    n_pairs = iq // 2

    def body(t, carry):
        j = 2 * t + 1
        step(j + 1, sb_ref, xb_ref, j, sa_ref, xa_ref)
        step(j + 2, sa_ref, xa_ref, j + 1, sb_ref, xb_ref)
        return carry

    lax.fori_loop(0, n_pairs, body, 0)
    j_last = 2 * n_pairs + 1

    @pl.when(iq % 2 == 0)
    def _():
        consume_block(j_last, sa_ref, xa_ref, causal)

    @pl.when(iq % 2 == 1)
    def _():
        issue(j_last + 1, sb_ref, xb_ref)
        consume_block(j_last, sa_ref, xa_ref)
        consume_block(j_last + 1, sb_ref, xb_ref, causal)

    ot = jnp.concatenate([acc_ref[hh, :F_HD, :] / acc_ref[hh, F_HD:F_HD + 1, :] for hh in range(2)], axis=0)
    out_ref[...] = ot.T.astype(out_ref.dtype)


def _attention(q, k, vt, batch, seq, batch_rows):
    assert ATT_TQ == ATT_TK and REAL0 % ATT_TK == 0
    pairs = F_HEADS // 2
    nq = seq // ATT_TQ
    return pl.pallas_call(
        _attn_kernel,
        grid=(batch, pairs, nq),
        in_specs=[pl.BlockSpec((None, 2 * LANES, ATT_TQ), lambda b, p, i: (b, p, i)),
                  pl.BlockSpec((None, batch_rows, 2 * LANES), lambda b, p, i: (b, 0, p)),
                  pl.BlockSpec((None, 2, V_ROWS, batch_rows), lambda b, p, i: (b, p, 0, 0))],
        out_specs=pl.BlockSpec((None, ATT_TQ, LANES), lambda b, p, i: (b, i, p)),
        out_shape=jax.ShapeDtypeStruct((batch, seq, D_MODEL), _bf16),
        scratch_shapes=[pltpu.VMEM((2, ATT_TK, ATT_TQ), _f32), pltpu.VMEM((2, ATT_TK, ATT_TQ), _f32),
                        pltpu.VMEM((2, 1, ATT_TQ), _f32), pltpu.VMEM((2, 1, ATT_TQ), _f32),
                        pltpu.VMEM((2, 1, ATT_TQ), _f32), pltpu.VMEM((2, V_ROWS, ATT_TQ), _f32)],
        compiler_params=_params("parallel", "parallel", "arbitrary"),
        name="fox_attention",
    )(q, k, vt)


def kernel(x, meta_tokens, norm_mix, norm_ffn, m_w_in, m_b_gate, m_g_out, m_w_out, kv_norm, kv_w, kv_b_f, k_norm,
           f_w_q, f_q_norm, f_w_o, moe_w_group, moe_b_group, moe_w_router, moe_b_router, moe_w_gate, moe_w_up,
           moe_w_down):
    B, S, D = x.shape
    depth = norm_mix.shape[0]
    n_a = m_w_in.shape[0]
    LP = REAL0 + S
    assert D == D_MODEL and S % ATT_TQ == 0 and LP % ROW_TILE == 0 and meta_tokens.shape[0] == N_META
    assert (B * LP) % MOE_ROW_TILE == 0 and (B * S) % MOE_ROW_TILE == 0
    assert depth > n_a and REAL0 % M_CHUNK == 0 and MLSTM_ROWS % M_CHUNK == 0
    assert n_a == 1
    head = jnp.concatenate([jnp.zeros((PAD_FRONT, D), x.dtype), meta_tokens.astype(x.dtype)], axis=0)
    k_sh = vt_sh = None
    rows_per_batch, row0, seq = LP, 0, LP
    for l in range(depth):
        if l < n_a:
            h, q, kt, v, o, u, bcol = _mlstm_in_proj(x, head, norm_mix[l], m_w_in[l], m_b_gate[l], LP)
            y = _mlstm_core(q, kt, v, o, u.T, bcol, m_g_out[l].reshape(-1), B, LP)
            w_out = m_w_out[l]
        else:
            if l == n_a:
                k_sh, vt_sh = _shared_kv(h, kv_norm, kv_w, kv_b_f, k_norm, B, LP)
                row0, seq = REAL0, S
            j = l - n_a
            q = _q_proj(h, norm_mix[l], f_w_q[j], f_q_norm[j], B, S, rows_per_batch, row0)
            y = _attention(q, k_sh, vt_sh, B, S, LP).reshape(B * S, D)
            w_out = f_w_o[j]
        h = _mix_out_moe(y, w_out, h, rows_per_batch, row0, seq, norm_ffn[l], moe_w_group[l], moe_b_group[l],
                         moe_w_router[l], moe_b_router[l], moe_w_gate[l], moe_w_up[l], moe_w_down[l])
        rows_per_batch, row0 = seq, 0
    return h.reshape(B, S, D)
```

```python
import functools

import jax
import numpy as np
import jax.numpy as jnp
from jax import lax
from jax.experimental import pallas as pl
from jax.experimental.pallas import tpu as pltpu

D_MODEL = 1024
N_META = 16
M_HEADS = 4
M_DV = D_MODEL // M_HEADS
M_DK = M_DV // 2
M_CHUNK = 256
GATE_CAP = 15.0
F_HEADS = 16
F_HD = D_MODEL // F_HEADS
N_GROUPS = 4
EXP_PER_GROUP = 4
N_EXPERTS = N_GROUPS * EXP_PER_GROUP
D_EXPERT = D_MODEL // 4
EPS = 1e-6
NEG = -1e30

LANES = 128
ROW_TILE = 512
IN_ROW_TILE = 1024
MOE_ROW_TILE = 1024
REAL0 = 512
PAD_FRONT = REAL0 - N_META
MLSTM_ROWS = 512
V_ROWS = F_HD + 16
ATT_TQ = 512
ATT_TK = 512
LOG2E = 1.4426950408889634
VMEM_LIMIT = 56 * 1024 * 1024

_f32 = jnp.float32
_bf16 = jnp.bfloat16


def _dot(a, b):
    return jnp.dot(a, b, preferred_element_type=_f32)


def _split2(w):
    hi = w.astype(_bf16)
    return hi, (w - hi.astype(_f32)).astype(_bf16)


def _dot_x3(x, w_hi, w_lo):
    x_hi, x_lo = _split2(x)
    return _dot(x_hi, w_hi) + _dot(x_lo, w_hi) + _dot(x_hi, w_lo)


def _dot_ones3(ones_mat, x):
    x1 = x.astype(_bf16)
    r1 = x - x1.astype(_f32)
    x2 = r1.astype(_bf16)
    x3 = (r1 - x2.astype(_f32)).astype(_bf16)
    return _dot(ones_mat, x1) + _dot(ones_mat, x2) + _dot(ones_mat, x3)


def _dot_nt(a, b):
    return lax.dot_general(a, b, (((1,), (1,)), ((), ())), preferred_element_type=_f32)


def _rmsnorm(x, g):
    return x * lax.rsqrt(jnp.mean(x * x, axis=-1, keepdims=True) + EPS) * g


def _log_sigmoid(x):
    return jnp.minimum(x, 0.0) - jnp.log1p(jnp.exp(-jnp.abs(x)))


def _params(*sem):
    return pltpu.CompilerParams(dimension_semantics=sem, vmem_limit_bytes=VMEM_LIMIT)


def _const_spec(shape):
    return pl.BlockSpec(shape, lambda *_: (0,) * len(shape))


def _mlstm_in_kernel(n_batch, batch_rows, x_lo_ref, x_hi_ref, head_ref, g_ref, wq_ref, wkt_ref, wv_ref, wo_ref,
                     wg_hi_ref, wg_lo_ref, bgate_ref, h_ref, q_ref, kt_ref, v_ref, o_ref, u_ref, b_ref):
    i = pl.program_id(0)
    half = x_lo_ref.shape[0]
    halves_per_batch = batch_rows // half
    parts = []
    for k, x_ref in enumerate((x_lo_ref, x_hi_ref)):
        is_head = lax.rem(2 * i + k, halves_per_batch) == 0
        parts.append(jnp.where(is_head, head_ref[...], x_ref[...]))
    x = jnp.concatenate(parts, axis=0)
    h_ref[...] = x
    xn = _rmsnorm(x, g_ref[...])
    xb = xn.astype(_bf16)
    q_ref[...] = (_dot(xb, wq_ref[...]) * (M_DK ** -0.5)).astype(_bf16)
    kt_ref[...] = _dot_nt(wkt_ref[...], xb).astype(_bf16)
    v_ref[...] = _dot(xb, wv_ref[...]).astype(_bf16)
    o_ref[...] = jax.nn.sigmoid(_dot(xb, wo_ref[...])).astype(_bf16)
    gates = _dot_x3(xn, wg_hi_ref[...], wg_lo_ref[...]) + bgate_ref[...]
    gates = GATE_CAP * jnp.tanh(gates * (1.0 / GATE_CAP))
    rows = x.shape[0]
    row = i * rows + lax.broadcasted_iota(jnp.int32, (rows, LANES), 0)
    is_pad = row < PAD_FRONT
    for bb in range(1, n_batch):
        is_pad = is_pad | ((row >= bb * batch_rows) & (row < bb * batch_rows + PAD_FRONT))
    li = jnp.where(is_pad, NEG, gates)
    lf = jnp.where(is_pad, 0.0, _log_sigmoid(gates))
    r = lax.broadcasted_iota(jnp.int32, (M_CHUNK, M_CHUNK), 0)
    c = lax.broadcasted_iota(jnp.int32, (M_CHUNK, M_CHUNK), 1)
    tril = jnp.where(c <= r, 1.0, 0.0).astype(_bf16)
    b = jnp.concatenate([_dot_ones3(tril, lf[k * M_CHUNK:(k + 1) * M_CHUNK, :]) for k in range(rows // M_CHUNK)], axis=0)
    b = pltpu.roll(b, LANES - M_HEADS, 1)
    u_ref[...] = (li - b)[:, :8]
    b_ref[...] = b[:, :8]


def _mlstm_in_proj(x, head, gain, w_in, b_gate, batch_rows):
    B, S, _ = x.shape
    T = B * batch_rows
    tm = IN_ROW_TILE
    half = tm // 2
    assert half == REAL0 and T % tm == 0 and S % half == 0 and batch_rows == REAL0 + S
    qk_w = M_HEADS * M_DK
    wq = w_in[:, :qk_w].astype(_bf16)
    wkt = w_in[:, qk_w:2 * qk_w].T.astype(_bf16)
    wv = w_in[:, 2 * qk_w:2 * qk_w + D_MODEL].astype(_bf16)
    wo = w_in[:, 2 * qk_w + D_MODEL:2 * qk_w + 2 * D_MODEL].astype(_bf16)
    n_gate = 2 * M_HEADS
    wg_hi, wg_lo = _split2(jnp.zeros((D_MODEL, LANES), _f32).at[:, :n_gate].set(w_in[:, 2 * qk_w + 2 * D_MODEL:]))
    bgate = jnp.zeros((1, LANES), _f32).at[0, :n_gate].set(b_gate)
    row = lambda n: pl.BlockSpec((tm, n), lambda i: (i, 0))

    def x_half(k):
        def index(i):
            r = 2 * i + k
            b = r // (batch_rows // half)
            j = r - b * (batch_rows // half)
            return (b * (S // half) + jnp.maximum(j - 1, 0), 0)
        return pl.BlockSpec((half, D_MODEL), index)

    x2 = x.reshape(B * S, D_MODEL)
    return pl.pallas_call(
        functools.partial(_mlstm_in_kernel, B, batch_rows),
        grid=(T // tm,),
        in_specs=[x_half(0), x_half(1), _const_spec((half, D_MODEL)), _const_spec((1, D_MODEL)),
                  _const_spec((D_MODEL, qk_w)), _const_spec((qk_w, D_MODEL)), _const_spec((D_MODEL, D_MODEL)),
                  _const_spec((D_MODEL, D_MODEL)), _const_spec((D_MODEL, LANES)), _const_spec((D_MODEL, LANES)),
                  _const_spec((1, LANES))],
        out_specs=[row(D_MODEL), row(qk_w), pl.BlockSpec((qk_w, tm), lambda i: (0, i)), row(D_MODEL), row(D_MODEL),
                   row(8), row(8)],
        out_shape=[jax.ShapeDtypeStruct((T, D_MODEL), _f32),
                   jax.ShapeDtypeStruct((T, qk_w), _bf16), jax.ShapeDtypeStruct((qk_w, T), _bf16),
                   jax.ShapeDtypeStruct((T, D_MODEL), _bf16), jax.ShapeDtypeStruct((T, D_MODEL), _bf16),
                   jax.ShapeDtypeStruct((T, 8), _f32), jax.ShapeDtypeStruct((T, 8), _f32)],
        compiler_params=_params("parallel"),
        name="mlstm_in_proj",
    )(x2, x2, head, gain.reshape(1, D_MODEL), wq, wkt, wv, wo, wg_hi, wg_lo, bgate)


def _mlstm_core_kernel(q_ref, kt_ref, v_ref, o_ref, ut_ref, b_ref, gout_ref, out_ref, state_ref, m_ref):
    C = M_CHUNK
    n_chunks = q_ref.shape[0] // C

    @pl.when(pl.program_id(1) == 0)
    def _():
        state_ref[...] = jnp.zeros_like(state_ref)
        m_ref[...] = jnp.zeros_like(m_ref)

    r = lax.broadcasted_iota(jnp.int32, (C, C), 0)
    c = lax.broadcasted_iota(jnp.int32, (C, C), 1)
    causal = c <= r
    ones_col = jnp.where(lax.broadcasted_iota(jnp.int32, (C, LANES), 1) == 0, 1.0, 0.0).astype(_bf16)

    H = range(M_HEADS)
    S = [state_ref[h] for h in H]
    m = [m_ref[h][0:1, 0:1] for h in H]
    for ci in range(n_chunks):
        rows = slice(ci * C, (ci + 1) * C)
        qc = [q_ref[rows, h * M_DK:(h + 1) * M_DK] for h in H]
        ktc = [kt_ref[h * M_DK:(h + 1) * M_DK, rows] for h in H]
        v_aug = [jnp.concatenate([v_ref[rows, h * M_DV:(h + 1) * M_DV], ones_col], axis=1) for h in H]
        u_row = [ut_ref[h:h + 1, rows] for h in H]
        b_col = [b_ref[rows, h:h + 1] for h in H]
        qk = [_dot(qc[h], ktc[h]) for h in H]
        qs = [_dot(qc[h], S[h].astype(_bf16)) for h in H]
        cu_col = [jnp.max(jnp.where(causal, u_row[h], NEG), axis=1, keepdims=True) for h in H]
        M = [jnp.maximum(cu_col[h], m[h]) for h in H]
        s_mat = [(qk[h] * jnp.where(causal, jnp.exp(u_row[h] - M[h]), 0.0)).astype(_bf16) for h in H]
        tot = [_dot(s_mat[h], v_aug[h]) + jnp.exp(m[h] - M[h]) * qs[h] for h in H]
        for h in H:
            g_tot = b_col[h][C - 1:C, :]
            cu_last = cu_col[h][C - 1:C, :]
            m_loc = g_tot + cu_last
            m_new = jnp.maximum(g_tot + m[h], m_loc)
            ktw = (ktc[h].astype(_f32) * jnp.exp(u_row[h] - cu_last)).astype(_bf16)
            kv = _dot(ktw, v_aug[h])
            S[h] = jnp.exp(g_tot + m[h] - m_new) * S[h] + jnp.exp(m_loc - m_new) * kv
            m[h] = m_new
        for h in H:
            num = tot[h][:, :M_DV]
            den = tot[h][:, M_DV:M_DV + 1]
            hout = num / jnp.maximum(jnp.abs(den), jnp.exp(-(b_col[h] + M[h])))
            y = hout * lax.rsqrt(jnp.mean(hout * hout, axis=-1, keepdims=True) + EPS)
            y = y * gout_ref[:, h * M_DV:(h + 1) * M_DV] * o_ref[rows, h * M_DV:(h + 1) * M_DV].astype(_f32)
            out_ref[rows, h * M_DV:(h + 1) * M_DV] = y.astype(out_ref.dtype)
    for h in H:
        state_ref[h] = S[h]
        m_ref[h] = jnp.broadcast_to(m[h], m_ref.shape[1:])


def _mlstm_core(q, kt, v, o, ut, bcol, g_out, batch, batch_rows):
    R = MLSTM_ROWS
    spb = batch_rows // R
    row = lambda n: pl.BlockSpec((R, n), lambda b, s: (b * spb + s, 0))
    col = lambda n: pl.BlockSpec((n, R), lambda b, s: (0, b * spb + s))
    qk_w = M_HEADS * M_DK
    return pl.pallas_call(
        _mlstm_core_kernel,
        grid=(batch, spb),
        in_specs=[row(qk_w), col(qk_w), row(D_MODEL), row(D_MODEL), col(8), row(8), _const_spec((1, D_MODEL))],
        out_specs=row(D_MODEL),
        out_shape=jax.ShapeDtypeStruct((batch * batch_rows, D_MODEL), _bf16),
        scratch_shapes=[pltpu.VMEM((M_HEADS, M_DK, M_DV + LANES), _f32), pltpu.VMEM((M_HEADS, 8, LANES), _f32)],
        compiler_params=_params("parallel", "arbitrary"),
        name="mlstm_core",
    )(q, kt, v, o, ut, bcol, g_out.reshape(1, D_MODEL))


MOE_CHUNK = 288
MOE_MAX_CHUNKS = (MOE_ROW_TILE - 1) // MOE_CHUNK + N_GROUPS
MOE_SORTED_ROWS = 2048
MOE_XG_W = D_MODEL + 2 * LANES


def _route(logits):
    lane = lax.broadcasted_iota(jnp.int32, logits.shape, 1)
    is_g = lane < N_GROUPS
    gl = jnp.where(is_g, logits, -jnp.inf)
    gmax = jnp.max(gl, axis=1, keepdims=True)
    top_g = jnp.min(jnp.where(gl == gmax, lane, LANES), axis=1, keepdims=True)
    p_g = 1.0 / jnp.sum(jnp.where(is_g, jnp.exp(gl - gmax), 0.0), axis=1, keepdims=True)
    e_lane = lane - N_GROUPS
    in_group = (e_lane >= top_g * EXP_PER_GROUP) & (e_lane < (top_g + 1) * EXP_PER_GROUP)
    el = jnp.where(in_group, logits, -jnp.inf)
    v1 = jnp.max(el, axis=1, keepdims=True)
    i1 = jnp.min(jnp.where(el == v1, lane, LANES), axis=1, keepdims=True)
    el2 = jnp.where(lane == i1, -jnp.inf, el)
    v2 = jnp.max(el2, axis=1, keepdims=True)
    i2 = jnp.min(jnp.where(el2 == v2, lane, LANES), axis=1, keepdims=True)
    e2 = jnp.exp(v2 - v1)
    p1 = p_g / (1.0 + e2)
    p2 = p_g * e2 / (1.0 + e2)
    return jnp.where(lane == i1, p1, jnp.where(lane == i2, p2, 0.0)), top_g


def _router_kernel(y_ref, wo_ref, res_lo_ref, res_hi_ref, g_ref, wr_hi_ref, wr_lo_ref, br_ref, tril_ref,
                   h_ref, xg_ref, pos_ref, meta_ref):
    res = jnp.concatenate([res_lo_ref[...], res_hi_ref[...]], axis=0)
    x = res + _dot(y_ref[...], wo_ref[...])
    h_ref[...] = x
    xn = _rmsnorm(x, g_ref[...])
    gates, top_g = _route(_dot_x3(xn, wr_hi_ref[...], wr_lo_ref[...]) + br_ref[...])
    g_hi, g_lo = _split2(gates)
    xg_ref[:, :D_MODEL] = xn.astype(_bf16)
    xg_ref[:, D_MODEL:D_MODEL + LANES] = g_hi
    xg_ref[:, D_MODEL + LANES:] = g_lo
    rows = x.shape[0]
    lane = lax.broadcasted_iota(jnp.int32, (rows, LANES), 1)
    in_g = lane == top_g
    onehot = jnp.where(in_g, 1.0, 0.0)
    incl = _dot(tril_ref[...], onehot.astype(_bf16))
    counts = incl[rows - 1:rows, :]
    nchunk = jnp.floor((counts + (MOE_CHUNK - 1)) * (1.0 / MOE_CHUNK) + 0.5 / MOE_CHUNK)
    first = pltpu.roll(nchunk, 1, 1) + pltpu.roll(nchunk, 2, 1) + pltpu.roll(nchunk, 3, 1)
    base = MOE_CHUNK * first + (incl - onehot)
    pos_ref[...] = jnp.sum(jnp.where(in_g, base, 0.0), axis=1, keepdims=True)
    end = first + nchunk
    lane1 = lax.broadcasted_iota(jnp.int32, (1, LANES), 1).astype(_f32)
    grp_of = jnp.zeros((1, LANES), _f32)
    for g in range(N_GROUPS - 1):
        grp_of = grp_of + jnp.where(lane1 >= end[:, g:g + 1], 1.0, 0.0)
    meta = jnp.where(lane1 == MOE_MAX_CHUNKS, end[:, N_GROUPS - 1:N_GROUPS], grp_of)
    meta_ref[...] = jnp.broadcast_to(meta, meta_ref.shape).astype(jnp.int32)


def _experts_kernel(grp_ref, cnt_ref, xg_ref, posr_ref, posc_ref, h_ref, wg_ref, wu_ref, wd_ref, out_ref, ys_ref):
    i = pl.program_id(0)
    c = pl.program_id(1)
    tm = xg_ref.shape[0]
    chunk_rows = pl.ds(pl.multiple_of(jnp.minimum(c, MOE_MAX_CHUNKS - 1) * MOE_CHUNK, 16), MOE_CHUNK)

    @pl.when(c == 0)
    def _():
        tail = MOE_MAX_CHUNKS * MOE_CHUNK
        ys_ref[tail:, :] = jnp.zeros((MOE_SORTED_ROWS - tail, D_MODEL), _bf16)

    @pl.when(c < cnt_ref[i])
    def _():
        grp = grp_ref[i * MOE_MAX_CHUNKS + c]
        first_row = (c * MOE_CHUNK).astype(_f32)
        r_id = lax.broadcasted_iota(jnp.int32, (MOE_CHUNK, tm), 0).astype(_f32) + first_row
        sel = jnp.where(posr_ref[...] == r_id, 1.0, 0.0).astype(_bf16)
        xg = _dot(sel, xg_ref[...])
        xs = xg[:, :D_MODEL].astype(_bf16)
        gs = xg[:, D_MODEL:D_MODEL + LANES] + xg[:, D_MODEL + LANES:]
        lane = lax.broadcasted_iota(jnp.int32, gs.shape, 1)
        acc = jnp.zeros((MOE_CHUNK, D_MODEL), _f32)
        for e in range(EXP_PER_GROUP):
            gate_e = jnp.sum(jnp.where(lane == N_GROUPS + grp * EXP_PER_GROUP + e, gs, 0.0), axis=1, keepdims=True)
            hg = _dot(xs, wg_ref[e])
            hu = _dot(xs, wu_ref[e])
            act = (hg * jax.nn.sigmoid(hg) * hu * gate_e).astype(_bf16)
            acc = acc + _dot(act, wd_ref[e])
        ys_ref[chunk_rows, :] = acc.astype(_bf16)

    @pl.when((c >= cnt_ref[i]) & (c < MOE_MAX_CHUNKS))
    def _():
        ys_ref[chunk_rows, :] = jnp.zeros((MOE_CHUNK, D_MODEL), _bf16)

    @pl.when(c == MOE_MAX_CHUNKS)
    def _():
        s_id = lax.broadcasted_iota(jnp.int32, (tm, MOE_SORTED_ROWS), 1).astype(_f32)
        sel_t = jnp.where(posc_ref[...] == s_id, 1.0, 0.0).astype(_bf16)
        out_ref[...] = h_ref[...] + _dot(sel_t, ys_ref[...])


def _mix_out_moe(y, w_out, res, res_batch_rows, res_row0, seq, gain, w_group, b_group, w_router, b_router,
                 w_gate, w_up, w_down):
    T = y.shape[0]
    tm = MOE_ROW_TILE
    half = tm // 2
    tiles = T // tm
    assert T % tm == 0 and seq % half == 0 and res_batch_rows % half == 0 and res_row0 % half == 0
    assert MOE_MAX_CHUNKS * MOE_CHUNK <= MOE_SORTED_ROWS and MOE_CHUNK % 16 == 0
    wr = jnp.zeros((D_MODEL, LANES), _f32)
    wr_hi, wr_lo = _split2(wr.at[:, :N_GROUPS].set(w_group).at[:, N_GROUPS:N_GROUPS + N_EXPERTS].set(w_router))
    br = jnp.zeros((1, LANES), _f32)
    br = br.at[0, :N_GROUPS].set(b_group).at[0, N_GROUPS:N_GROUPS + N_EXPERTS].set(b_router)
    tril = jnp.asarray(np.tril(np.ones((tm, tm), np.float32)), _bf16)
    row = lambda n: pl.BlockSpec((tm, n), lambda i: (i, 0))

    def res_half(k):
        def index(i):
            r = 2 * i + k
            b = r // (seq // half)
            return (b * (res_batch_rows // half) + res_row0 // half + r - b * (seq // half), 0)
        return pl.BlockSpec((half, D_MODEL), index)

    h, xg, pos, meta = pl.pallas_call(
        _router_kernel,
        grid=(tiles,),
        in_specs=[row(D_MODEL), _const_spec((D_MODEL, D_MODEL)), res_half(0), res_half(1), _const_spec((1, D_MODEL)),
                  _const_spec((D_MODEL, LANES)), _const_spec((D_MODEL, LANES)), _const_spec((1, LANES)),
                  _const_spec((tm, tm))],
        out_specs=[row(D_MODEL), row(MOE_XG_W), row(1), pl.BlockSpec((8, LANES), lambda i: (i, 0))],
        out_shape=[jax.ShapeDtypeStruct((T, D_MODEL), _f32), jax.ShapeDtypeStruct((T, MOE_XG_W), _bf16),
                   jax.ShapeDtypeStruct((T, 1), _f32), jax.ShapeDtypeStruct((tiles * 8, LANES), jnp.int32)],
        compiler_params=_params("parallel"),
        name="mix_out_router",
    )(y, w_out.astype(_bf16), res, res, gain.reshape(1, D_MODEL), wr_hi, wr_lo, br, tril)
    meta = meta.reshape(tiles, 8, LANES)[:, 0]
    chunk_group = meta[:, :MOE_MAX_CHUNKS].reshape(-1)
    n_chunks = meta[:, MOE_MAX_CHUNKS]

    def wspec(shape):
        def index(i, c, grp, cnt):
            return (grp[i * MOE_MAX_CHUNKS + jnp.minimum(c, cnt[i] - 1)], 0, 0)
        return pl.BlockSpec(shape, index)

    tile = lambda n: pl.BlockSpec((tm, n), lambda i, c, grp, cnt: (i, 0))
    return pl.pallas_call(
        _experts_kernel,
        grid_spec=pltpu.PrefetchScalarGridSpec(
            num_scalar_prefetch=2,
            grid=(tiles, MOE_MAX_CHUNKS + 1),
            in_specs=[tile(MOE_XG_W), pl.BlockSpec((None, 1, tm), lambda i, c, grp, cnt: (i, 0, 0)),
                      tile(1), tile(D_MODEL), wspec((EXP_PER_GROUP, D_MODEL, D_EXPERT)),
                      wspec((EXP_PER_GROUP, D_MODEL, D_EXPERT)), wspec((EXP_PER_GROUP, D_EXPERT, D_MODEL))],
            out_specs=tile(D_MODEL),
            scratch_shapes=[pltpu.VMEM((MOE_SORTED_ROWS, D_MODEL), _bf16)]),
        out_shape=jax.ShapeDtypeStruct((T, D_MODEL), _f32),
        compiler_params=_params("parallel", "arbitrary"),
        name="moe_experts",
    )(chunk_group, n_chunks, xg, pos.reshape(tiles, 1, tm), pos, h, w_gate.astype(_bf16),
      w_up.astype(_bf16), w_down.astype(_bf16))


SEG_W = 256


def _segment_ones():
    return jnp.kron(jnp.eye(SEG_W // F_HD, dtype=_f32), jnp.ones((F_HD, F_HD), _f32)).astype(_bf16)


def _head_place_matrices():
    pairs = F_HEADS // 2
    place = np.zeros((pairs, 2 * LANES, 2 * LANES), np.float32)
    i = np.arange(F_HD)
    for p in range(pairs):
        place[p, i, i] = 1.0
        place[p, F_HD + i, LANES + i] = 1.0
        for hh in range(2):
            for t in range(3):
                place[p, LANES + F_HEADS * t + 2 * p + hh, hh * LANES + F_HD + t] = 1.0
    return jnp.asarray(place, _bf16)


def _heads_rmsnorm(x, seg, g):
    outs = []
    for c in range(x.shape[1] // SEG_W):
        xs = x[:, c * SEG_W:(c + 1) * SEG_W]
        sq_hi, sq_lo = _split2(xs * xs)
        ms = (_dot(sq_hi, seg) + _dot(sq_lo, seg)) * (1.0 / F_HD)
        outs.append(xs * lax.rsqrt(ms + EPS) * g[:, c * SEG_W:(c + 1) * SEG_W])
    return jnp.concatenate(outs, axis=1)


def _kv_kernel(h_ref, g_ref, wk_ref, wvt_ref, wf_hi_ref, wf_lo_ref, bf_ref, gk_ref, seg_ref, place_ref,
               k_ref, vt_ref, carry_ref):
    j = pl.program_id(1)

    @pl.when(j == 0)
    def _():
        carry_ref[...] = jnp.zeros_like(carry_ref)

    x = h_ref[...]
    rows = x.shape[0]
    xn = _rmsnorm(x, g_ref[...])
    xb = xn.astype(_bf16)
    vt = _dot_nt(wvt_ref[...], xb).astype(_bf16)
    for h in range(F_HEADS):
        vt_ref[h, :F_HD, :] = vt[h * F_HD:(h + 1) * F_HD, :]
        vt_ref[h, F_HD:, :] = jnp.ones((V_ROWS - F_HD, rows), _bf16)
    kn = _heads_rmsnorm(_dot(xb, wk_ref[...]), seg_ref[...], gk_ref[...]).astype(_bf16)
    lf = _log_sigmoid(_dot_x3(xn, wf_hi_ref[...], wf_lo_ref[...]) + bf_ref[...])
    row_in_batch = j * rows + lax.broadcasted_iota(jnp.int32, lf.shape, 0)
    lf = jnp.where(row_in_batch < PAD_FRONT, 0.0, lf)
    r = lax.broadcasted_iota(jnp.int32, (LANES, LANES), 0)
    c = lax.broadcasted_iota(jnp.int32, (LANES, LANES), 1)
    tril = jnp.where(c <= r, 1.0, 0.0).astype(_bf16)
    carry = carry_ref[0:1, :]
    terms = []
    for sb in range(rows // LANES):
        cs = _dot_ones3(tril, lf[sb * LANES:(sb + 1) * LANES, :]) + carry
        carry = cs[LANES - 1:LANES, :]
        f = (-LOG2E) * cs
        f1 = f.astype(_bf16).astype(_f32)
        f2 = (f - f1).astype(_bf16).astype(_f32)
        f3 = (f - f1) - f2
        terms.append(jnp.where(c < F_HEADS, f1, jnp.where(c < 2 * F_HEADS, pltpu.roll(f2, F_HEADS, 1),
                                                          jnp.where(c < 3 * F_HEADS, pltpu.roll(f3, 2 * F_HEADS, 1), 0.0))))
    carry_ref[...] = jnp.broadcast_to(carry, carry_ref.shape)
    fterms = jnp.concatenate(terms, axis=0).astype(_bf16)
    for p in range(F_HEADS // 2):
        lhs = jnp.concatenate([kn[:, p * LANES:(p + 1) * LANES], fterms], axis=1)
        k_ref[:, 2 * p * LANES:2 * (p + 1) * LANES] = _dot(lhs, place_ref[p]).astype(_bf16)


def _shared_kv(h, gain, w_kv, b_f, g_k, batch, batch_rows):
    tm = ROW_TILE
    spb = batch_rows // tm
    T = batch * batch_rows
    wk = w_kv[:, :D_MODEL].astype(_bf16)
    wvt = w_kv[:, D_MODEL:2 * D_MODEL].T.astype(_bf16)
    wf_hi, wf_lo = _split2(jnp.zeros((D_MODEL, LANES), _f32).at[:, :F_HEADS].set(w_kv[:, 2 * D_MODEL:]))
    bf = jnp.zeros((1, LANES), _f32).at[0, :F_HEADS].set(b_f)
    row = lambda n: pl.BlockSpec((tm, n), lambda b, j: (b * spb + j, 0))
    wide = F_HEADS * LANES
    ka, vt = pl.pallas_call(
        _kv_kernel,
        grid=(batch, spb),
        in_specs=[row(D_MODEL), _const_spec((1, D_MODEL)), _const_spec((D_MODEL, D_MODEL)),
                  _const_spec((D_MODEL, D_MODEL)), _const_spec((D_MODEL, LANES)), _const_spec((D_MODEL, LANES)),
                  _const_spec((1, LANES)), _const_spec((1, D_MODEL)), _const_spec((SEG_W, SEG_W)),
                  _const_spec((F_HEADS // 2, 2 * LANES, 2 * LANES))],
        out_specs=[row(wide), pl.BlockSpec((None, F_HEADS, V_ROWS, tm), lambda b, j: (b, 0, 0, j))],
        out_shape=[jax.ShapeDtypeStruct((T, wide), _bf16),
                   jax.ShapeDtypeStruct((batch, F_HEADS, V_ROWS, batch_rows), _bf16)],
        scratch_shapes=[pltpu.VMEM((8, LANES), _f32)],
        compiler_params=_params("parallel", "arbitrary"),
        name="shared_kv",
    )(h, gain.reshape(1, D_MODEL), wk, wvt, wf_hi, wf_lo, bf, jnp.tile(g_k, F_HEADS).reshape(1, D_MODEL),
      _segment_ones(), _head_place_matrices())
    return ka.reshape(batch, batch_rows, wide), vt


def _q_kernel(h_ref, g_ref, wq_ref, gq_ref, seg_ref, place_ref, q_ref):
    xb = _rmsnorm(h_ref[...], g_ref[...]).astype(_bf16)
    qn = _heads_rmsnorm(_dot(xb, wq_ref[...]), seg_ref[...], gq_ref[...])
    qn = (qn * (F_HD ** -0.5 * LOG2E)).astype(_bf16)
    row = lax.broadcasted_iota(jnp.int32, (2 * LANES, 1), 0) % LANES
    ones = jnp.where((row >= F_HD) & (row < F_HD + 3), 1.0, 0.0)
    spread_t = place_ref[0, :LANES, :].T
    for p in range(F_HEADS // 2):
        spread = _dot_nt(spread_t, qn[:, p * LANES:(p + 1) * LANES])
        q_ref[2 * p * LANES:2 * (p + 1) * LANES, :] = (spread + ones).astype(_bf16)


def _q_proj(h, gain, w_q, g_q, batch, seq, batch_rows, row0):
    tm = ROW_TILE
    spb = seq // tm
    wide = F_HEADS * LANES
    qa = pl.pallas_call(
        _q_kernel,
        grid=(batch, spb),
        in_specs=[pl.BlockSpec((tm, D_MODEL), lambda b, j: (b * (batch_rows // tm) + row0 // tm + j, 0)),
                  _const_spec((1, D_MODEL)), _const_spec((D_MODEL, D_MODEL)), _const_spec((1, D_MODEL)),
                  _const_spec((SEG_W, SEG_W)), _const_spec((F_HEADS // 2, 2 * LANES, 2 * LANES))],
        out_specs=pl.BlockSpec((None, wide, tm), lambda b, j: (b, 0, j)),
        out_shape=jax.ShapeDtypeStruct((batch, wide, seq), _bf16),
        compiler_params=_params("parallel", "parallel"),
        name="attn_q_proj",
    )(h, gain.reshape(1, D_MODEL), w_q.astype(_bf16), jnp.tile(g_q, F_HEADS).reshape(1, D_MODEL), _segment_ones(),
      _head_place_matrices())
    return qa


def _attn_kernel(q_ref, k_ref, vt_ref, out_ref, sa_ref, sb_ref, xa_ref, xb_ref, m_ref, acc_ref):
    iq = pl.program_id(2)
    tq, tk = ATT_TQ, ATT_TK
    m_ref[...] = jnp.full(m_ref.shape, NEG, _f32)
    acc_ref[...] = jnp.zeros_like(acc_ref)

    def scores(ks, rows, hh):
        return _dot(k_ref[pl.ds(ks, rows), hh * LANES:(hh + 1) * LANES], q_ref[hh * LANES:(hh + 1) * LANES, :])

    def issue_head(jk, s_ref, x_ref, hh):
        st = scores(pl.multiple_of(jk * tk, tk), tk, hh)
        s_ref[hh] = st
        x_ref[hh] = jnp.max(st, axis=0, keepdims=True)

    def issue(jk, s_ref, x_ref):
        for hh in range(2):
            issue_head(jk, s_ref, x_ref, hh)

    def consume(st, st_max, ks, rows, hh, above_diagonal=False):
        m_prev = m_ref[hh]
        m_new = jnp.maximum(m_prev, st_max)
        alpha = jnp.exp2(m_prev - m_new)
        tile = 2 * LANES
        for qh in range(tq // tile):
            ql = slice(qh * tile, (qh + 1) * tile)
            pv = None
            for kh in range(rows // tile) if rows >= tile else range(1):
                if above_diagonal and kh > qh:
                    continue
                kr = min(tile, rows)
                pt = jnp.exp2((st[kh * kr:(kh + 1) * kr, ql] - m_new[:, ql]).astype(_bf16))
                d = _dot(vt_ref[hh, :, pl.ds(ks + kh * kr, kr)], pt)
                pv = d if pv is None else pv + d
            acc_ref[hh, :, ql] = alpha[:, ql] * acc_ref[hh, :, ql] + pv
        m_ref[hh] = m_new

    def consume_masked(st, mask, ks, rows, hh):
        st = jnp.where(mask, st, NEG)
        consume(st, jnp.max(st, axis=0, keepdims=True), ks, rows, hh, above_diagonal=True)

    def consume_block(jk, s_ref, x_ref, mask=None):
        ks = pl.multiple_of(jk * tk, tk)
        for hh in range(2):
            if mask is None:
                consume(s_ref[hh], x_ref[hh], ks, tk, hh)
            else:
                consume_masked(s_ref[hh], mask, ks, tk, hh)

    def step(j_issue, si_ref, xi_ref, j_cons, sc_ref, xc_ref):
        ks = pl.multiple_of(j_cons * tk, tk)
        for hh in range(2):
            issue_head(j_issue, si_ref, xi_ref, hh)
            consume(sc_ref[hh], xc_ref[hh], ks, tk, hh)

    k_idx = lax.broadcasted_iota(jnp.int32, (tk, tq), 0)
    q_idx = lax.broadcasted_iota(jnp.int32, (tk, tq), 1)
    causal = k_idx <= q_idx

    st_meta = [scores(PAD_FRONT, N_META, hh) for hh in range(2)]
    issue(1, sa_ref, xa_ref)
    for hh in range(2):
        consume(st_meta[hh], jnp.max(st_meta[hh], axis=0, keepdims=True), PAD_FRONT, N_META, hh)

    n_pairs = iq // 2

    def body(t, carry):
        j = 2 * t + 1
        step(j + 1, sb_ref, xb_ref, j, sa_ref, xa_ref)
        step(j + 2, sa_ref, xa_ref, j + 1, sb_ref, xb_ref)
        return carry

    lax.fori_loop(0, n_pairs, body, 0)
    j_last = 2 * n_pairs + 1

    @pl.when(iq % 2 == 0)
    def _():
        consume_block(j_last, sa_ref, xa_ref, causal)

    @pl.when(iq % 2 == 1)
    def _():
        issue(j_last + 1, sb_ref, xb_ref)
        consume_block(j_last, sa_ref, xa_ref)
        consume_block(j_last + 1, sb_ref, xb_ref, causal)

    ot = jnp.concatenate([acc_ref[hh, :F_HD, :] / acc_ref[hh, F_HD:F_HD + 1, :] for hh in range(2)], axis=0)
    out_ref[...] = ot.T.astype(out_ref.dtype)


def _attention(q, k, vt, batch, seq, batch_rows):
    assert ATT_TQ == ATT_TK and REAL0 % ATT_TK == 0
    pairs = F_HEADS // 2
    nq = seq // ATT_TQ
    return pl.pallas_call(
        _attn_kernel,
        grid=(batch, pairs, nq),
        in_specs=[pl.BlockSpec((None, 2 * LANES, ATT_TQ), lambda b, p, i: (b, p, i)),
                  pl.BlockSpec((None, batch_rows, 2 * LANES), lambda b, p, i: (b, 0, p)),
                  pl.BlockSpec((None, 2, V_ROWS, batch_rows), lambda b, p, i: (b, p, 0, 0))],
        out_specs=pl.BlockSpec((None, ATT_TQ, LANES), lambda b, p, i: (b, i, p)),
        out_shape=jax.ShapeDtypeStruct((batch, seq, D_MODEL), _bf16),
        scratch_shapes=[pltpu.VMEM((2, ATT_TK, ATT_TQ), _f32), pltpu.VMEM((2, ATT_TK, ATT_TQ), _f32),
                        pltpu.VMEM((2, 1, ATT_TQ), _f32), pltpu.VMEM((2, 1, ATT_TQ), _f32),
                        pltpu.VMEM((2, 1, ATT_TQ), _f32), pltpu.VMEM((2, V_ROWS, ATT_TQ), _f32)],
        compiler_params=_params("parallel", "parallel", "arbitrary"),
        name="fox_attention",
    )(q, k, vt)


def kernel(x, meta_tokens, norm_mix, norm_ffn, m_w_in, m_b_gate, m_g_out, m_w_out, kv_norm, kv_w, kv_b_f, k_norm,
           f_w_q, f_q_norm, f_w_o, moe_w_group, moe_b_group, moe_w_router, moe_b_router, moe_w_gate, moe_w_up,
           moe_w_down):
    B, S, D = x.shape
    depth = norm_mix.shape[0]
    n_a = m_w_in.shape[0]
    LP = REAL0 + S
    assert D == D_MODEL and S % ATT_TQ == 0 and LP % ROW_TILE == 0 and meta_tokens.shape[0] == N_META
    assert (B * LP) % MOE_ROW_TILE == 0 and (B * S) % MOE_ROW_TILE == 0
    assert depth > n_a and REAL0 % M_CHUNK == 0 and MLSTM_ROWS % M_CHUNK == 0
    assert n_a == 1
    head = jnp.concatenate([jnp.zeros((PAD_FRONT, D), x.dtype), meta_tokens.astype(x.dtype)], axis=0)
    k_sh = vt_sh = None
    rows_per_batch, row0, seq = LP, 0, LP
    for l in range(depth):
        if l < n_a:
            h, q, kt, v, o, u, bcol = _mlstm_in_proj(x, head, norm_mix[l], m_w_in[l], m_b_gate[l], LP)
            y = _mlstm_core(q, kt, v, o, u.T, bcol, m_g_out[l].reshape(-1), B, LP)
            w_out = m_w_out[l]
        else:
            if l == n_a:
                k_sh, vt_sh = _shared_kv(h, kv_norm, kv_w, kv_b_f, k_norm, B, LP)
                row0, seq = REAL0, S
            j = l - n_a
            q = _q_proj(h, norm_mix[l], f_w_q[j], f_q_norm[j], B, S, rows_per_batch, row0)
            y = _attention(q, k_sh, vt_sh, B, S, LP).reshape(B * S, D)
            w_out = f_w_o[j]
        h = _mix_out_moe(y, w_out, h, rows_per_batch, row0, seq, norm_ffn[l], moe_w_group[l], moe_b_group[l],
                         moe_w_router[l], moe_b_router[l], moe_w_gate[l], moe_w_up[l], moe_w_down[l])
        rows_per_batch, row0 = seq, 0
    return h.reshape(B, S, D)
```

```python
import functools

import jax
import numpy as np
import jax.numpy as jnp
from jax import lax
from jax.experimental import pallas as pl
from jax.experimental.pallas import tpu as pltpu

D_MODEL = 1024
N_META = 16
M_HEADS = 4
M_DV = D_MODEL // M_HEADS
M_DK = M_DV // 2
M_CHUNK = 256
GATE_CAP = 15.0
F_HEADS = 16
F_HD = D_MODEL // F_HEADS
N_GROUPS = 4
EXP_PER_GROUP = 4
N_EXPERTS = N_GROUPS * EXP_PER_GROUP
D_EXPERT = D_MODEL // 4
EPS = 1e-6
NEG = -1e30

LANES = 128
ROW_TILE = 512
IN_ROW_TILE = 1024
MOE_ROW_TILE = 1024
REAL0 = 512
PAD_FRONT = REAL0 - N_META
MLSTM_ROWS = 512
V_ROWS = F_HD + 16
ATT_TQ = 1024
ATT_TK = 512
LOG2E = 1.4426950408889634
VMEM_LIMIT = 56 * 1024 * 1024

_f32 = jnp.float32
_bf16 = jnp.bfloat16


def _dot(a, b):
    return jnp.dot(a, b, preferred_element_type=_f32)


def _split2(w):
    hi = w.astype(_bf16)
    return hi, (w - hi.astype(_f32)).astype(_bf16)


def _dot_x3(x, w_hi, w_lo):
    x_hi, x_lo = _split2(x)
    return _dot(x_hi, w_hi) + _dot(x_lo, w_hi) + _dot(x_hi, w_lo)


def _dot_ones3(ones_mat, x):
    x1 = x.astype(_bf16)
    r1 = x - x1.astype(_f32)
    x2 = r1.astype(_bf16)
    x3 = (r1 - x2.astype(_f32)).astype(_bf16)
    return _dot(ones_mat, x1) + _dot(ones_mat, x2) + _dot(ones_mat, x3)


def _dot_nt(a, b):
    return lax.dot_general(a, b, (((1,), (1,)), ((), ())), preferred_element_type=_f32)


def _rmsnorm(x, g):
    return x * lax.rsqrt(jnp.mean(x * x, axis=-1, keepdims=True) + EPS) * g


def _log_sigmoid(x):
    return jnp.minimum(x, 0.0) - jnp.log1p(jnp.exp(-jnp.abs(x)))


def _params(*sem):
    return pltpu.CompilerParams(dimension_semantics=sem, vmem_limit_bytes=VMEM_LIMIT)


def _const_spec(shape):
    return pl.BlockSpec(shape, lambda *_: (0,) * len(shape))


def _mlstm_in_kernel(n_batch, batch_rows, x_lo_ref, x_hi_ref, head_ref, g_ref, wq_ref, wkt_ref, wv_ref, wo_ref,
                     wg_hi_ref, wg_lo_ref, bgate_ref, h_ref, q_ref, kt_ref, v_ref, o_ref, u_ref, b_ref):
    i = pl.program_id(0)
    half = x_lo_ref.shape[0]
    halves_per_batch = batch_rows // half
    parts = []
    for k, x_ref in enumerate((x_lo_ref, x_hi_ref)):
        is_head = lax.rem(2 * i + k, halves_per_batch) == 0
        parts.append(jnp.where(is_head, head_ref[...], x_ref[...]))
    x = jnp.concatenate(parts, axis=0)
    h_ref[...] = x
    xn = _rmsnorm(x, g_ref[...])
    xb = xn.astype(_bf16)
    q_ref[...] = (_dot(xb, wq_ref[...]) * (M_DK ** -0.5)).astype(_bf16)
    kt_ref[...] = _dot_nt(wkt_ref[...], xb).astype(_bf16)
    v_ref[...] = _dot(xb, wv_ref[...]).astype(_bf16)
    o_ref[...] = jax.nn.sigmoid(_dot(xb, wo_ref[...])).astype(_bf16)
    gates = _dot_x3(xn, wg_hi_ref[...], wg_lo_ref[...]) + bgate_ref[...]
    gates = GATE_CAP * jnp.tanh(gates * (1.0 / GATE_CAP))
    rows = x.shape[0]
    row = i * rows + lax.broadcasted_iota(jnp.int32, (rows, LANES), 0)
    is_pad = row < PAD_FRONT
    for bb in range(1, n_batch):
        is_pad = is_pad | ((row >= bb * batch_rows) & (row < bb * batch_rows + PAD_FRONT))
    li = jnp.where(is_pad, NEG, gates)
    lf = jnp.where(is_pad, 0.0, _log_sigmoid(gates))
    r = lax.broadcasted_iota(jnp.int32, (M_CHUNK, M_CHUNK), 0)
    c = lax.broadcasted_iota(jnp.int32, (M_CHUNK, M_CHUNK), 1)
    tril = jnp.where(c <= r, 1.0, 0.0).astype(_bf16)
    b = jnp.concatenate([_dot_ones3(tril, lf[k * M_CHUNK:(k + 1) * M_CHUNK, :]) for k in range(rows // M_CHUNK)], axis=0)
    b = pltpu.roll(b, LANES - M_HEADS, 1)
    u_ref[...] = (li - b)[:, :8]
    b_ref[...] = b[:, :8]


def _mlstm_in_proj(x, head, gain, w_in, b_gate, batch_rows):
    B, S, _ = x.shape
    T = B * batch_rows
    tm = IN_ROW_TILE
    half = tm // 2
    assert half == REAL0 and T % tm == 0 and S % half == 0 and batch_rows == REAL0 + S
    qk_w = M_HEADS * M_DK
    wq = w_in[:, :qk_w].astype(_bf16)
    wkt = w_in[:, qk_w:2 * qk_w].T.astype(_bf16)
    wv = w_in[:, 2 * qk_w:2 * qk_w + D_MODEL].astype(_bf16)
    wo = w_in[:, 2 * qk_w + D_MODEL:2 * qk_w + 2 * D_MODEL].astype(_bf16)
    n_gate = 2 * M_HEADS
    wg_hi, wg_lo = _split2(jnp.zeros((D_MODEL, LANES), _f32).at[:, :n_gate].set(w_in[:, 2 * qk_w + 2 * D_MODEL:]))
    bgate = jnp.zeros((1, LANES), _f32).at[0, :n_gate].set(b_gate)
    row = lambda n: pl.BlockSpec((tm, n), lambda i: (i, 0))

    def x_half(k):
        def index(i):
            r = 2 * i + k
            b = r // (batch_rows // half)
            j = r - b * (batch_rows // half)
            return (b * (S // half) + jnp.maximum(j - 1, 0), 0)
        return pl.BlockSpec((half, D_MODEL), index)

    x2 = x.reshape(B * S, D_MODEL)
    return pl.pallas_call(
        functools.partial(_mlstm_in_kernel, B, batch_rows),
        grid=(T // tm,),
        in_specs=[x_half(0), x_half(1), _const_spec((half, D_MODEL)), _const_spec((1, D_MODEL)),
                  _const_spec((D_MODEL, qk_w)), _const_spec((qk_w, D_MODEL)), _const_spec((D_MODEL, D_MODEL)),
                  _const_spec((D_MODEL, D_MODEL)), _const_spec((D_MODEL, LANES)), _const_spec((D_MODEL, LANES)),
                  _const_spec((1, LANES))],
        out_specs=[row(D_MODEL), row(qk_w), pl.BlockSpec((qk_w, tm), lambda i: (0, i)), row(D_MODEL), row(D_MODEL),
                   row(8), row(8)],
        out_shape=[jax.ShapeDtypeStruct((T, D_MODEL), _f32),
                   jax.ShapeDtypeStruct((T, qk_w), _bf16), jax.ShapeDtypeStruct((qk_w, T), _bf16),
                   jax.ShapeDtypeStruct((T, D_MODEL), _bf16), jax.ShapeDtypeStruct((T, D_MODEL), _bf16),
                   jax.ShapeDtypeStruct((T, 8), _f32), jax.ShapeDtypeStruct((T, 8), _f32)],
        compiler_params=_params("parallel"),
        name="mlstm_in_proj",
    )(x2, x2, head, gain.reshape(1, D_MODEL), wq, wkt, wv, wo, wg_hi, wg_lo, bgate)


def _mlstm_core_kernel(q_ref, kt_ref, v_ref, o_ref, ut_ref, b_ref, gout_ref, out_ref, state_ref, m_ref):
    C = M_CHUNK
    n_chunks = q_ref.shape[0] // C

    @pl.when(pl.program_id(1) == 0)
    def _():
        state_ref[...] = jnp.zeros_like(state_ref)
        m_ref[...] = jnp.zeros_like(m_ref)

    r = lax.broadcasted_iota(jnp.int32, (C, C), 0)
    c = lax.broadcasted_iota(jnp.int32, (C, C), 1)
    causal = c <= r
    ones_col = jnp.where(lax.broadcasted_iota(jnp.int32, (C, LANES), 1) == 0, 1.0, 0.0).astype(_bf16)

    H = range(M_HEADS)
    S = [state_ref[h] for h in H]
    m = [m_ref[h][0:1, 0:1] for h in H]
    for ci in range(n_chunks):
        rows = slice(ci * C, (ci + 1) * C)
        qc = [q_ref[rows, h * M_DK:(h + 1) * M_DK] for h in H]
        ktc = [kt_ref[h * M_DK:(h + 1) * M_DK, rows] for h in H]
        v_aug = [jnp.concatenate([v_ref[rows, h * M_DV:(h + 1) * M_DV], ones_col], axis=1) for h in H]
        u_row = [ut_ref[h:h + 1, rows] for h in H]
        b_col = [b_ref[rows, h:h + 1] for h in H]
        qk = [_dot(qc[h], ktc[h]) for h in H]
        qs = [_dot(qc[h], S[h].astype(_bf16)) for h in H]
        cu_col = [jnp.max(jnp.where(causal, u_row[h], NEG), axis=1, keepdims=True) for h in H]
        M = [jnp.maximum(cu_col[h], m[h]) for h in H]
        s_mat = [(qk[h] * jnp.where(causal, jnp.exp(u_row[h] - M[h]), 0.0)).astype(_bf16) for h in H]
        tot = [_dot(s_mat[h], v_aug[h]) + jnp.exp(m[h] - M[h]) * qs[h] for h in H]
        for h in H:
            g_tot = b_col[h][C - 1:C, :]
            cu_last = cu_col[h][C - 1:C, :]
            m_loc = g_tot + cu_last
            m_new = jnp.maximum(g_tot + m[h], m_loc)
            ktw = (ktc[h].astype(_f32) * jnp.exp(u_row[h] - cu_last)).astype(_bf16)
            kv = _dot(ktw, v_aug[h])
            S[h] = jnp.exp(g_tot + m[h] - m_new) * S[h] + jnp.exp(m_loc - m_new) * kv
            m[h] = m_new
        for h in H:
            num = tot[h][:, :M_DV]
            den = tot[h][:, M_DV:M_DV + 1]
            hout = num / jnp.maximum(jnp.abs(den), jnp.exp(-(b_col[h] + M[h])))
            y = hout * lax.rsqrt(jnp.mean(hout * hout, axis=-1, keepdims=True) + EPS)
            y = y * gout_ref[:, h * M_DV:(h + 1) * M_DV] * o_ref[rows, h * M_DV:(h + 1) * M_DV].astype(_f32)
            out_ref[rows, h * M_DV:(h + 1) * M_DV] = y.astype(out_ref.dtype)
    for h in H:
        state_ref[h] = S[h]
        m_ref[h] = jnp.broadcast_to(m[h], m_ref.shape[1:])


def _mlstm_core(q, kt, v, o, ut, bcol, g_out, batch, batch_rows):
    R = MLSTM_ROWS
    spb = batch_rows // R
    row = lambda n: pl.BlockSpec((R, n), lambda b, s: (b * spb + s, 0))
    col = lambda n: pl.BlockSpec((n, R), lambda b, s: (0, b * spb + s))
    qk_w = M_HEADS * M_DK
    return pl.pallas_call(
        _mlstm_core_kernel,
        grid=(batch, spb),
        in_specs=[row(qk_w), col(qk_w), row(D_MODEL), row(D_MODEL), col(8), row(8), _const_spec((1, D_MODEL))],
        out_specs=row(D_MODEL),
        out_shape=jax.ShapeDtypeStruct((batch * batch_rows, D_MODEL), _bf16),
        scratch_shapes=[pltpu.VMEM((M_HEADS, M_DK, M_DV + LANES), _f32), pltpu.VMEM((M_HEADS, 8, LANES), _f32)],
        compiler_params=_params("parallel", "arbitrary"),
        name="mlstm_core",
    )(q, kt, v, o, ut, bcol, g_out.reshape(1, D_MODEL))


MOE_CHUNK = 288
MOE_MAX_CHUNKS = (MOE_ROW_TILE - 1) // MOE_CHUNK + N_GROUPS
MOE_SORTED_ROWS = 2048
MOE_XG_W = D_MODEL + 2 * LANES


def _route(logits):
    lane = lax.broadcasted_iota(jnp.int32, logits.shape, 1)
    is_g = lane < N_GROUPS
    gl = jnp.where(is_g, logits, -jnp.inf)
    gmax = jnp.max(gl, axis=1, keepdims=True)
    top_g = jnp.min(jnp.where(gl == gmax, lane, LANES), axis=1, keepdims=True)
    p_g = 1.0 / jnp.sum(jnp.where(is_g, jnp.exp(gl - gmax), 0.0), axis=1, keepdims=True)
    e_lane = lane - N_GROUPS
    in_group = (e_lane >= top_g * EXP_PER_GROUP) & (e_lane < (top_g + 1) * EXP_PER_GROUP)
    el = jnp.where(in_group, logits, -jnp.inf)
    v1 = jnp.max(el, axis=1, keepdims=True)
    i1 = jnp.min(jnp.where(el == v1, lane, LANES), axis=1, keepdims=True)
    el2 = jnp.where(lane == i1, -jnp.inf, el)
    v2 = jnp.max(el2, axis=1, keepdims=True)
    i2 = jnp.min(jnp.where(el2 == v2, lane, LANES), axis=1, keepdims=True)
    e2 = jnp.exp(v2 - v1)
    p1 = p_g / (1.0 + e2)
    p2 = p_g * e2 / (1.0 + e2)
    return jnp.where(lane == i1, p1, jnp.where(lane == i2, p2, 0.0)), top_g


def _router_kernel(y_ref, wo_ref, res_lo_ref, res_hi_ref, g_ref, wr_hi_ref, wr_lo_ref, br_ref, tril_ref,
                   h_ref, xg_ref, pos_ref, meta_ref):
    res = jnp.concatenate([res_lo_ref[...], res_hi_ref[...]], axis=0)
    x = res + _dot(y_ref[...], wo_ref[...])
    h_ref[...] = x
    xn = _rmsnorm(x, g_ref[...])
    gates, top_g = _route(_dot_x3(xn, wr_hi_ref[...], wr_lo_ref[...]) + br_ref[...])
    g_hi, g_lo = _split2(gates)
    xg_ref[:, :D_MODEL] = xn.astype(_bf16)
    xg_ref[:, D_MODEL:D_MODEL + LANES] = g_hi
    xg_ref[:, D_MODEL + LANES:] = g_lo
    rows = x.shape[0]
    lane = lax.broadcasted_iota(jnp.int32, (rows, LANES), 1)
    in_g = lane == top_g
    onehot = jnp.where(in_g, 1.0, 0.0)
    incl = _dot(tril_ref[...], onehot.astype(_bf16))
    counts = incl[rows - 1:rows, :]
    nchunk = jnp.floor((counts + (MOE_CHUNK - 1)) * (1.0 / MOE_CHUNK) + 0.5 / MOE_CHUNK)
    first = pltpu.roll(nchunk, 1, 1) + pltpu.roll(nchunk, 2, 1) + pltpu.roll(nchunk, 3, 1)
    base = MOE_CHUNK * first + (incl - onehot)
    pos_ref[...] = jnp.sum(jnp.where(in_g, base, 0.0), axis=1, keepdims=True)
    end = first + nchunk
    lane1 = lax.broadcasted_iota(jnp.int32, (1, LANES), 1).astype(_f32)
    grp_of = jnp.zeros((1, LANES), _f32)
    for g in range(N_GROUPS - 1):
        grp_of = grp_of + jnp.where(lane1 >= end[:, g:g + 1], 1.0, 0.0)
    meta = jnp.where(lane1 == MOE_MAX_CHUNKS, end[:, N_GROUPS - 1:N_GROUPS], grp_of)
    meta_ref[...] = jnp.broadcast_to(meta, meta_ref.shape).astype(jnp.int32)


def _experts_kernel(grp_ref, cnt_ref, xg_ref, posr_ref, posc_ref, h_ref, wg_ref, wu_ref, wd_ref, out_ref, ys_ref):
    i = pl.program_id(0)
    c = pl.program_id(1)
    tm = xg_ref.shape[0]
    chunk_rows = pl.ds(pl.multiple_of(jnp.minimum(c, MOE_MAX_CHUNKS - 1) * MOE_CHUNK, 16), MOE_CHUNK)

    @pl.when(c == 0)
    def _():
        tail = MOE_MAX_CHUNKS * MOE_CHUNK
        ys_ref[tail:, :] = jnp.zeros((MOE_SORTED_ROWS - tail, D_MODEL), _bf16)

    @pl.when(c < cnt_ref[i])
    def _():
        grp = grp_ref[i * MOE_MAX_CHUNKS + c]
        first_row = (c * MOE_CHUNK).astype(_f32)
        r_id = lax.broadcasted_iota(jnp.int32, (MOE_CHUNK, tm), 0).astype(_f32) + first_row
        sel = jnp.where(posr_ref[...] == r_id, 1.0, 0.0).astype(_bf16)
        xg = _dot(sel, xg_ref[...])
        xs = xg[:, :D_MODEL].astype(_bf16)
        gs = xg[:, D_MODEL:D_MODEL + LANES] + xg[:, D_MODEL + LANES:]
        lane = lax.broadcasted_iota(jnp.int32, gs.shape, 1)
        acc = jnp.zeros((MOE_CHUNK, D_MODEL), _f32)
        for e in range(EXP_PER_GROUP):
            gate_e = jnp.sum(jnp.where(lane == N_GROUPS + grp * EXP_PER_GROUP + e, gs, 0.0), axis=1, keepdims=True)
            hg = _dot(xs, wg_ref[e])
            hu = _dot(xs, wu_ref[e])
            act = (hg * jax.nn.sigmoid(hg) * hu * gate_e).astype(_bf16)
            acc = acc + _dot(act, wd_ref[e])
        ys_ref[chunk_rows, :] = acc.astype(_bf16)

    @pl.when((c >= cnt_ref[i]) & (c < MOE_MAX_CHUNKS))
    def _():
        ys_ref[chunk_rows, :] = jnp.zeros((MOE_CHUNK, D_MODEL), _bf16)

    @pl.when(c == MOE_MAX_CHUNKS)
    def _():
        s_id = lax.broadcasted_iota(jnp.int32, (tm, MOE_SORTED_ROWS), 1).astype(_f32)
        sel_t = jnp.where(posc_ref[...] == s_id, 1.0, 0.0).astype(_bf16)
        out_ref[...] = h_ref[...] + _dot(sel_t, ys_ref[...])


def _mix_out_moe(y, w_out, res, res_batch_rows, res_row0, seq, gain, w_group, b_group, w_router, b_router,
                 w_gate, w_up, w_down):
    T = y.shape[0]
    tm = MOE_ROW_TILE
    half = tm // 2
    tiles = T // tm
    assert T % tm == 0 and seq % half == 0 and res_batch_rows % half == 0 and res_row0 % half == 0
    assert MOE_MAX_CHUNKS * MOE_CHUNK <= MOE_SORTED_ROWS and MOE_CHUNK % 16 == 0
    wr = jnp.zeros((D_MODEL, LANES), _f32)
    wr_hi, wr_lo = _split2(wr.at[:, :N_GROUPS].set(w_group).at[:, N_GROUPS:N_GROUPS + N_EXPERTS].set(w_router))
    br = jnp.zeros((1, LANES), _f32)
    br = br.at[0, :N_GROUPS].set(b_group).at[0, N_GROUPS:N_GROUPS + N_EXPERTS].set(b_router)
    tril = jnp.asarray(np.tril(np.ones((tm, tm), np.float32)), _bf16)
    row = lambda n: pl.BlockSpec((tm, n), lambda i: (i, 0))

    def res_half(k):
        def index(i):
            r = 2 * i + k
            b = r // (seq // half)
            return (b * (res_batch_rows // half) + res_row0 // half + r - b * (seq // half), 0)
        return pl.BlockSpec((half, D_MODEL), index)

    h, xg, pos, meta = pl.pallas_call(
        _router_kernel,
        grid=(tiles,),
        in_specs=[row(D_MODEL), _const_spec((D_MODEL, D_MODEL)), res_half(0), res_half(1), _const_spec((1, D_MODEL)),
                  _const_spec((D_MODEL, LANES)), _const_spec((D_MODEL, LANES)), _const_spec((1, LANES)),
                  _const_spec((tm, tm))],
        out_specs=[row(D_MODEL), row(MOE_XG_W), row(1), pl.BlockSpec((8, LANES), lambda i: (i, 0))],
        out_shape=[jax.ShapeDtypeStruct((T, D_MODEL), _f32), jax.ShapeDtypeStruct((T, MOE_XG_W), _bf16),
                   jax.ShapeDtypeStruct((T, 1), _f32), jax.ShapeDtypeStruct((tiles * 8, LANES), jnp.int32)],
        compiler_params=_params("parallel"),
        name="mix_out_router",
    )(y, w_out.astype(_bf16), res, res, gain.reshape(1, D_MODEL), wr_hi, wr_lo, br, tril)
    meta = meta.reshape(tiles, 8, LANES)[:, 0]
    chunk_group = meta[:, :MOE_MAX_CHUNKS].reshape(-1)
    n_chunks = meta[:, MOE_MAX_CHUNKS]

    def wspec(shape):
        def index(i, c, grp, cnt):
            return (grp[i * MOE_MAX_CHUNKS + jnp.minimum(c, cnt[i] - 1)], 0, 0)
        return pl.BlockSpec(shape, index)

    tile = lambda n: pl.BlockSpec((tm, n), lambda i, c, grp, cnt: (i, 0))
    return pl.pallas_call(
        _experts_kernel,
        grid_spec=pltpu.PrefetchScalarGridSpec(
            num_scalar_prefetch=2,
            grid=(tiles, MOE_MAX_CHUNKS + 1),
            in_specs=[tile(MOE_XG_W), pl.BlockSpec((None, 1, tm), lambda i, c, grp, cnt: (i, 0, 0)),
                      tile(1), tile(D_MODEL), wspec((EXP_PER_GROUP, D_MODEL, D_EXPERT)),
                      wspec((EXP_PER_GROUP, D_MODEL, D_EXPERT)), wspec((EXP_PER_GROUP, D_EXPERT, D_MODEL))],
            out_specs=tile(D_MODEL),
            scratch_shapes=[pltpu.VMEM((MOE_SORTED_ROWS, D_MODEL), _bf16)]),
        out_shape=jax.ShapeDtypeStruct((T, D_MODEL), _f32),
        compiler_params=_params("parallel", "arbitrary"),
        name="moe_experts",
    )(chunk_group, n_chunks, xg, pos.reshape(tiles, 1, tm), pos, h, w_gate.astype(_bf16),
      w_up.astype(_bf16), w_down.astype(_bf16))


SEG_W = 256


def _segment_ones():
    return jnp.kron(jnp.eye(SEG_W // F_HD, dtype=_f32), jnp.ones((F_HD, F_HD), _f32)).astype(_bf16)


def _head_place_matrices():
    pairs = F_HEADS // 2
    place = np.zeros((pairs, 2 * LANES, 2 * LANES), np.float32)
    i = np.arange(F_HD)
    for p in range(pairs):
        place[p, i, i] = 1.0
        place[p, F_HD + i, LANES + i] = 1.0
        for hh in range(2):
            for t in range(3):
                place[p, LANES + F_HEADS * t + 2 * p + hh, hh * LANES + F_HD + t] = 1.0
    return jnp.asarray(place, _bf16)


def _heads_rmsnorm(x, seg, g):
    outs = []
    for c in range(x.shape[1] // SEG_W):
        xs = x[:, c * SEG_W:(c + 1) * SEG_W]
        sq_hi, sq_lo = _split2(xs * xs)
        ms = (_dot(sq_hi, seg) + _dot(sq_lo, seg)) * (1.0 / F_HD)
        outs.append(xs * lax.rsqrt(ms + EPS) * g[:, c * SEG_W:(c + 1) * SEG_W])
    return jnp.concatenate(outs, axis=1)


def _kv_kernel(h_ref, g_ref, wk_ref, wvt_ref, wf_hi_ref, wf_lo_ref, bf_ref, gk_ref, seg_ref, place_ref,
               k_ref, vt_ref, carry_ref):
    j = pl.program_id(1)

    @pl.when(j == 0)
    def _():
        carry_ref[...] = jnp.zeros_like(carry_ref)

    x = h_ref[...]
    rows = x.shape[0]
    xn = _rmsnorm(x, g_ref[...])
    xb = xn.astype(_bf16)
    vt = _dot_nt(wvt_ref[...], xb).astype(_bf16)
    for h in range(F_HEADS):
        vt_ref[h, :F_HD, :] = vt[h * F_HD:(h + 1) * F_HD, :]
        vt_ref[h, F_HD:, :] = jnp.ones((V_ROWS - F_HD, rows), _bf16)
    kn = _heads_rmsnorm(_dot(xb, wk_ref[...]), seg_ref[...], gk_ref[...]).astype(_bf16)
    lf = _log_sigmoid(_dot_x3(xn, wf_hi_ref[...], wf_lo_ref[...]) + bf_ref[...])
    row_in_batch = j * rows + lax.broadcasted_iota(jnp.int32, lf.shape, 0)
    lf = jnp.where(row_in_batch < PAD_FRONT, 0.0, lf)
    r = lax.broadcasted_iota(jnp.int32, (LANES, LANES), 0)
    c = lax.broadcasted_iota(jnp.int32, (LANES, LANES), 1)
    tril = jnp.where(c <= r, 1.0, 0.0).astype(_bf16)
    carry = carry_ref[0:1, :]
    terms = []
    for sb in range(rows // LANES):
        cs = _dot_ones3(tril, lf[sb * LANES:(sb + 1) * LANES, :]) + carry
        carry = cs[LANES - 1:LANES, :]
        f = (-LOG2E) * cs
        f1 = f.astype(_bf16).astype(_f32)
        f2 = (f - f1).astype(_bf16).astype(_f32)
        f3 = (f - f1) - f2
        terms.append(jnp.where(c < F_HEADS, f1, jnp.where(c < 2 * F_HEADS, pltpu.roll(f2, F_HEADS, 1),
                                                          jnp.where(c < 3 * F_HEADS, pltpu.roll(f3, 2 * F_HEADS, 1), 0.0))))
    carry_ref[...] = jnp.broadcast_to(carry, carry_ref.shape)
    fterms = jnp.concatenate(terms, axis=0).astype(_bf16)
    for p in range(F_HEADS // 2):
        lhs = jnp.concatenate([kn[:, p * LANES:(p + 1) * LANES], fterms], axis=1)
        k_ref[:, 2 * p * LANES:2 * (p + 1) * LANES] = _dot(lhs, place_ref[p]).astype(_bf16)


def _shared_kv(h, gain, w_kv, b_f, g_k, batch, batch_rows):
    tm = ROW_TILE
    spb = batch_rows // tm
    T = batch * batch_rows
    wk = w_kv[:, :D_MODEL].astype(_bf16)
    wvt = w_kv[:, D_MODEL:2 * D_MODEL].T.astype(_bf16)
    wf_hi, wf_lo = _split2(jnp.zeros((D_MODEL, LANES), _f32).at[:, :F_HEADS].set(w_kv[:, 2 * D_MODEL:]))
    bf = jnp.zeros((1, LANES), _f32).at[0, :F_HEADS].set(b_f)
    row = lambda n: pl.BlockSpec((tm, n), lambda b, j: (b * spb + j, 0))
    wide = F_HEADS * LANES
    ka, vt = pl.pallas_call(
        _kv_kernel,
        grid=(batch, spb),
        in_specs=[row(D_MODEL), _const_spec((1, D_MODEL)), _const_spec((D_MODEL, D_MODEL)),
                  _const_spec((D_MODEL, D_MODEL)), _const_spec((D_MODEL, LANES)), _const_spec((D_MODEL, LANES)),
                  _const_spec((1, LANES)), _const_spec((1, D_MODEL)), _const_spec((SEG_W, SEG_W)),
                  _const_spec((F_HEADS // 2, 2 * LANES, 2 * LANES))],
        out_specs=[row(wide), pl.BlockSpec((None, F_HEADS, V_ROWS, tm), lambda b, j: (b, 0, 0, j))],
        out_shape=[jax.ShapeDtypeStruct((T, wide), _bf16),
                   jax.ShapeDtypeStruct((batch, F_HEADS, V_ROWS, batch_rows), _bf16)],
        scratch_shapes=[pltpu.VMEM((8, LANES), _f32)],
        compiler_params=_params("parallel", "arbitrary"),
        name="shared_kv",
    )(h, gain.reshape(1, D_MODEL), wk, wvt, wf_hi, wf_lo, bf, jnp.tile(g_k, F_HEADS).reshape(1, D_MODEL),
      _segment_ones(), _head_place_matrices())
    return ka.reshape(batch, batch_rows, wide), vt


def _q_kernel(h_ref, g_ref, wq_ref, gq_ref, seg_ref, place_ref, q_ref):
    xb = _rmsnorm(h_ref[...], g_ref[...]).astype(_bf16)
    qn = _heads_rmsnorm(_dot(xb, wq_ref[...]), seg_ref[...], gq_ref[...])
    qn = (qn * (F_HD ** -0.5 * LOG2E)).astype(_bf16)
    row = lax.broadcasted_iota(jnp.int32, (2 * LANES, 1), 0) % LANES
    ones = jnp.where((row >= F_HD) & (row < F_HD + 3), 1.0, 0.0)
    spread_t = place_ref[0, :LANES, :].T
    for p in range(F_HEADS // 2):
        spread = _dot_nt(spread_t, qn[:, p * LANES:(p + 1) * LANES])
        q_ref[2 * p * LANES:2 * (p + 1) * LANES, :] = (spread + ones).astype(_bf16)


def _q_proj(h, gain, w_q, g_q, batch, seq, batch_rows, row0):
    tm = ROW_TILE
    spb = seq // tm
    wide = F_HEADS * LANES
    qa = pl.pallas_call(
        _q_kernel,
        grid=(batch, spb),
        in_specs=[pl.BlockSpec((tm, D_MODEL), lambda b, j: (b * (batch_rows // tm) + row0 // tm + j, 0)),
                  _const_spec((1, D_MODEL)), _const_spec((D_MODEL, D_MODEL)), _const_spec((1, D_MODEL)),
                  _const_spec((SEG_W, SEG_W)), _const_spec((F_HEADS // 2, 2 * LANES, 2 * LANES))],
        out_specs=pl.BlockSpec((None, wide, tm), lambda b, j: (b, 0, j)),
        out_shape=jax.ShapeDtypeStruct((batch, wide, seq), _bf16),
        compiler_params=_params("parallel", "parallel"),
        name="attn_q_proj",
    )(h, gain.reshape(1, D_MODEL), w_q.astype(_bf16), jnp.tile(g_q, F_HEADS).reshape(1, D_MODEL), _segment_ones(),
      _head_place_matrices())
    return qa


def _attn_kernel(q_ref, k_ref, vt_ref, out_ref, sa_ref, sb_ref, xa_ref, xb_ref, m_ref, acc_ref):
    iq = pl.program_id(2)
    tq, tk = ATT_TQ, ATT_TK
    tile = 2 * LANES
    n_qt = tq // tile
    m_ref[...] = jnp.full(m_ref.shape, NEG, _f32)
    acc_ref[...] = jnp.zeros_like(acc_ref)

    def scores(ks, rows, hh):
        return _dot(k_ref[pl.ds(ks, rows), hh * LANES:(hh + 1) * LANES], q_ref[hh * LANES:(hh + 1) * LANES, :])

    def issue_head(jk, s_ref, x_ref, hh):
        st = scores(pl.multiple_of(jk * tk, tk), tk, hh)
        s_ref[hh] = st
        x_ref[hh] = jnp.max(st, axis=0, keepdims=True)

    def issue(jk, s_ref, x_ref):
        for hh in range(2):
            issue_head(jk, s_ref, x_ref, hh)

    def consume(st, st_max, ks, rows, hh, q_tiles=None, diag_tile=None, lane0=0):
        for qh in (range(n_qt) if q_tiles is None else q_tiles):
            ql = slice(qh * tile, (qh + 1) * tile)
            sl = slice(qh * tile - lane0, (qh + 1) * tile - lane0)
            m_prev = m_ref[hh, :, ql]
            m_new = jnp.maximum(m_prev, st_max[:, sl])
            alpha = jnp.exp2(m_prev - m_new)
            pv = None
            for kh in range(rows // tile) if rows >= tile else range(1):
                if diag_tile is not None and kh > qh - diag_tile:
                    continue
                kr = min(tile, rows)
                pt = jnp.exp2((st[kh * kr:(kh + 1) * kr, sl] - m_new).astype(_bf16))
                d = _dot(vt_ref[hh, :, pl.ds(ks + kh * kr, kr)], pt)
                pv = d if pv is None else pv + d
            acc_ref[hh, :, ql] = alpha * acc_ref[hh, :, ql] + pv
            m_ref[hh, :, ql] = m_new

    def consume_diag(jk, s_ref, hh, q0):
        ks = pl.multiple_of(jk * tk, tk)
        k_idx = lax.broadcasted_iota(jnp.int32, (tk, tk), 0)
        q_idx = lax.broadcasted_iota(jnp.int32, (tk, tk), 1)
        st = jnp.where(k_idx <= q_idx, s_ref[hh, :, q0:q0 + tk], NEG)
        consume(st, jnp.max(st, axis=0, keepdims=True), ks, tk, hh,
                q_tiles=range(q0 // tile, (q0 + tk) // tile), diag_tile=q0 // tile, lane0=q0)

    def step(j_issue, si_ref, xi_ref, j_cons, sc_ref, xc_ref):
        ks = pl.multiple_of(j_cons * tk, tk)
        for hh in range(2):
            issue_head(j_issue, si_ref, xi_ref, hh)
            consume(sc_ref[hh], xc_ref[hh], ks, tk, hh)

    st_meta = [scores(PAD_FRONT, N_META, hh) for hh in range(2)]
    issue(1, sa_ref, xa_ref)
    for hh in range(2):
        consume(st_meta[hh], jnp.max(st_meta[hh], axis=0, keepdims=True), PAD_FRONT, N_META, hh)

    def body(t, carry):
        j = 2 * t + 1
        step(j + 1, sb_ref, xb_ref, j, sa_ref, xa_ref)
        step(j + 2, sa_ref, xa_ref, j + 1, sb_ref, xb_ref)
        return carry

    lax.fori_loop(0, iq, body, 0)
    j_diag = 2 * iq + 1
    ks = pl.multiple_of(j_diag * tk, tk)
    ks_next = pl.multiple_of((j_diag + 1) * tk, tk)
    for hh in range(2):
        sb_ref[hh, :, tk:] = _dot(k_ref[pl.ds(ks_next, tk), hh * LANES:(hh + 1) * LANES],
                                  q_ref[hh * LANES:(hh + 1) * LANES, tk:])
        consume_diag(j_diag, sa_ref, hh, 0)
        consume(sa_ref[hh], xa_ref[hh], ks, tk, hh, q_tiles=range(tk // tile, n_qt))
    for hh in range(2):
        consume_diag(j_diag + 1, sb_ref, hh, tk)

    ot = jnp.concatenate([acc_ref[hh, :F_HD, :] / acc_ref[hh, F_HD:F_HD + 1, :] for hh in range(2)], axis=0)
    out_ref[...] = ot.T.astype(out_ref.dtype)


def _attention(q, k, vt, batch, seq, batch_rows):
    assert ATT_TQ == 2 * ATT_TK and REAL0 == ATT_TK
    pairs = F_HEADS // 2
    nq = seq // ATT_TQ
    return pl.pallas_call(
        _attn_kernel,
        grid=(batch, pairs, nq),
        in_specs=[pl.BlockSpec((None, 2 * LANES, ATT_TQ), lambda b, p, i: (b, p, i)),
                  pl.BlockSpec((None, batch_rows, 2 * LANES), lambda b, p, i: (b, 0, p)),
                  pl.BlockSpec((None, 2, V_ROWS, batch_rows), lambda b, p, i: (b, p, 0, 0))],
        out_specs=pl.BlockSpec((None, ATT_TQ, LANES), lambda b, p, i: (b, i, p)),
        out_shape=jax.ShapeDtypeStruct((batch, seq, D_MODEL), _bf16),
        scratch_shapes=[pltpu.VMEM((2, ATT_TK, ATT_TQ), _f32), pltpu.VMEM((2, ATT_TK, ATT_TQ), _f32),
                        pltpu.VMEM((2, 1, ATT_TQ), _f32), pltpu.VMEM((2, 1, ATT_TQ), _f32),
                        pltpu.VMEM((2, 1, ATT_TQ), _f32), pltpu.VMEM((2, V_ROWS, ATT_TQ), _f32)],
        compiler_params=_params("parallel", "parallel", "arbitrary"),
        name="fox_attention",
    )(q, k, vt)


def kernel(x, meta_tokens, norm_mix, norm_ffn, m_w_in, m_b_gate, m_g_out, m_w_out, kv_norm, kv_w, kv_b_f, k_norm,
           f_w_q, f_q_norm, f_w_o, moe_w_group, moe_b_group, moe_w_router, moe_b_router, moe_w_gate, moe_w_up,
           moe_w_down):
    B, S, D = x.shape
    depth = norm_mix.shape[0]
    n_a = m_w_in.shape[0]
    LP = REAL0 + S
    assert D == D_MODEL and S % ATT_TQ == 0 and LP % ROW_TILE == 0 and meta_tokens.shape[0] == N_META
    assert (B * LP) % MOE_ROW_TILE == 0 and (B * S) % MOE_ROW_TILE == 0
    assert depth > n_a and REAL0 % M_CHUNK == 0 and MLSTM_ROWS % M_CHUNK == 0
    assert n_a == 1
    head = jnp.concatenate([jnp.zeros((PAD_FRONT, D), x.dtype), meta_tokens.astype(x.dtype)], axis=0)
    k_sh = vt_sh = None
    rows_per_batch, row0, seq = LP, 0, LP
    for l in range(depth):
        if l < n_a:
            h, q, kt, v, o, u, bcol = _mlstm_in_proj(x, head, norm_mix[l], m_w_in[l], m_b_gate[l], LP)
            y = _mlstm_core(q, kt, v, o, u.T, bcol, m_g_out[l].reshape(-1), B, LP)
            w_out = m_w_out[l]
        else:
            if l == n_a:
                k_sh, vt_sh = _shared_kv(h, kv_norm, kv_w, kv_b_f, k_norm, B, LP)
                row0, seq = REAL0, S
            j = l - n_a
            q = _q_proj(h, norm_mix[l], f_w_q[j], f_q_norm[j], B, S, rows_per_batch, row0)
            y = _attention(q, k_sh, vt_sh, B, S, LP).reshape(B * S, D)
            w_out = f_w_o[j]
        h = _mix_out_moe(y, w_out, h, rows_per_batch, row0, seq, norm_ffn[l], moe_w_group[l], moe_b_group[l],
                         moe_w_router[l], moe_b_router[l], moe_w_gate[l], moe_w_up[l], moe_w_down[l])
        rows_per_batch, row0 = seq, 0
    return h.reshape(B, S, D)
```

```python
import functools

import jax
import numpy as np
import jax.numpy as jnp
from jax import lax
from jax.experimental import pallas as pl
from jax.experimental.pallas import tpu as pltpu

D_MODEL = 1024
N_META = 16
M_HEADS = 4
M_DV = D_MODEL // M_HEADS
M_DK = M_DV // 2
M_CHUNK = 256
GATE_CAP = 15.0
F_HEADS = 16
F_HD = D_MODEL // F_HEADS
N_GROUPS = 4
EXP_PER_GROUP = 4
N_EXPERTS = N_GROUPS * EXP_PER_GROUP
D_EXPERT = D_MODEL // 4
EPS = 1e-6
NEG = -1e30

LANES = 128
ROW_TILE = 512
IN_ROW_TILE = 1024
MOE_ROW_TILE = 1024
REAL0 = 512
PAD_FRONT = REAL0 - N_META
MLSTM_ROWS = 512
V_ROWS = F_HD + 16
ATT_TQ = 1024
ATT_TK = 512
LOG2E = 1.4426950408889634
VMEM_LIMIT = 56 * 1024 * 1024

_f32 = jnp.float32
_bf16 = jnp.bfloat16


def _dot(a, b):
    return jnp.dot(a, b, preferred_element_type=_f32)


def _split2(w):
    hi = w.astype(_bf16)
    return hi, (w - hi.astype(_f32)).astype(_bf16)


def _dot_x3(x, w_hi, w_lo):
    x_hi, x_lo = _split2(x)
    return _dot(x_hi, w_hi) + _dot(x_lo, w_hi) + _dot(x_hi, w_lo)


def _dot_ones3(ones_mat, x):
    x1 = x.astype(_bf16)
    r1 = x - x1.astype(_f32)
    x2 = r1.astype(_bf16)
    x3 = (r1 - x2.astype(_f32)).astype(_bf16)
    return _dot(ones_mat, x1) + _dot(ones_mat, x2) + _dot(ones_mat, x3)


def _dot_nt(a, b):
    return lax.dot_general(a, b, (((1,), (1,)), ((), ())), preferred_element_type=_f32)


def _rmsnorm(x, g):
    return x * lax.rsqrt(jnp.mean(x * x, axis=-1, keepdims=True) + EPS) * g


def _log_sigmoid(x):
    return jnp.minimum(x, 0.0) - jnp.log1p(jnp.exp(-jnp.abs(x)))


def _params(*sem):
    return pltpu.CompilerParams(dimension_semantics=sem, vmem_limit_bytes=VMEM_LIMIT)


def _const_spec(shape):
    return pl.BlockSpec(shape, lambda *_: (0,) * len(shape))


def _mlstm_in_kernel(n_batch, batch_rows, x_lo_ref, x_hi_ref, head_ref, g_ref, wq_ref, wkt_ref, wv_ref, wo_ref,
                     wg_hi_ref, wg_lo_ref, bgate_ref, h_ref, q_ref, kt_ref, v_ref, o_ref, u_ref, b_ref):
    i = pl.program_id(0)
    half = x_lo_ref.shape[0]
    halves_per_batch = batch_rows // half
    parts = []
    for k, x_ref in enumerate((x_lo_ref, x_hi_ref)):
        is_head = lax.rem(2 * i + k, halves_per_batch) == 0
        parts.append(jnp.where(is_head, head_ref[...], x_ref[...]))
    x = jnp.concatenate(parts, axis=0)
    h_ref[...] = x
    xn = _rmsnorm(x, g_ref[...])
    xb = xn.astype(_bf16)
    q_ref[...] = (_dot(xb, wq_ref[...]) * (M_DK ** -0.5)).astype(_bf16)
    kt_ref[...] = _dot_nt(wkt_ref[...], xb).astype(_bf16)
    v_ref[...] = _dot(xb, wv_ref[...]).astype(_bf16)
    o_ref[...] = jax.nn.sigmoid(_dot(xb, wo_ref[...])).astype(_bf16)
    gates = _dot_x3(xn, wg_hi_ref[...], wg_lo_ref[...]) + bgate_ref[...]
    gates = GATE_CAP * jnp.tanh(gates * (1.0 / GATE_CAP))
    rows = x.shape[0]
    row = i * rows + lax.broadcasted_iota(jnp.int32, (rows, LANES), 0)
    is_pad = row < PAD_FRONT
    for bb in range(1, n_batch):
        is_pad = is_pad | ((row >= bb * batch_rows) & (row < bb * batch_rows + PAD_FRONT))
    li = jnp.where(is_pad, NEG, gates)
    lf = jnp.where(is_pad, 0.0, _log_sigmoid(gates))
    r = lax.broadcasted_iota(jnp.int32, (M_CHUNK, M_CHUNK), 0)
    c = lax.broadcasted_iota(jnp.int32, (M_CHUNK, M_CHUNK), 1)
    tril = jnp.where(c <= r, 1.0, 0.0).astype(_bf16)
    b = jnp.concatenate([_dot_ones3(tril, lf[k * M_CHUNK:(k + 1) * M_CHUNK, :]) for k in range(rows // M_CHUNK)], axis=0)
    b = pltpu.roll(b, LANES - M_HEADS, 1)
    u_ref[...] = (li - b)[:, :8]
    b_ref[...] = b[:, :8]


def _mlstm_in_proj(x, head, gain, w_in, b_gate, batch_rows):
    B, S, _ = x.shape
    T = B * batch_rows
    tm = IN_ROW_TILE
    half = tm // 2
    assert half == REAL0 and T % tm == 0 and S % half == 0 and batch_rows == REAL0 + S
    qk_w = M_HEADS * M_DK
    wq = w_in[:, :qk_w].astype(_bf16)
    wkt = w_in[:, qk_w:2 * qk_w].T.astype(_bf16)
    wv = w_in[:, 2 * qk_w:2 * qk_w + D_MODEL].astype(_bf16)
    wo = w_in[:, 2 * qk_w + D_MODEL:2 * qk_w + 2 * D_MODEL].astype(_bf16)
    n_gate = 2 * M_HEADS
    wg_hi, wg_lo = _split2(jnp.zeros((D_MODEL, LANES), _f32).at[:, :n_gate].set(w_in[:, 2 * qk_w + 2 * D_MODEL:]))
    bgate = jnp.zeros((1, LANES), _f32).at[0, :n_gate].set(b_gate)
    row = lambda n: pl.BlockSpec((tm, n), lambda i: (i, 0))

    def x_half(k):
        def index(i):
            r = 2 * i + k
            b = r // (batch_rows // half)
            j = r - b * (batch_rows // half)
            return (b * (S // half) + jnp.maximum(j - 1, 0), 0)
        return pl.BlockSpec((half, D_MODEL), index)

    x2 = x.reshape(B * S, D_MODEL)
    return pl.pallas_call(
        functools.partial(_mlstm_in_kernel, B, batch_rows),
        grid=(T // tm,),
        in_specs=[x_half(0), x_half(1), _const_spec((half, D_MODEL)), _const_spec((1, D_MODEL)),
                  _const_spec((D_MODEL, qk_w)), _const_spec((qk_w, D_MODEL)), _const_spec((D_MODEL, D_MODEL)),
                  _const_spec((D_MODEL, D_MODEL)), _const_spec((D_MODEL, LANES)), _const_spec((D_MODEL, LANES)),
                  _const_spec((1, LANES))],
        out_specs=[row(D_MODEL), row(qk_w), pl.BlockSpec((qk_w, tm), lambda i: (0, i)), row(D_MODEL), row(D_MODEL),
                   row(8), row(8)],
        out_shape=[jax.ShapeDtypeStruct((T, D_MODEL), _f32),
                   jax.ShapeDtypeStruct((T, qk_w), _bf16), jax.ShapeDtypeStruct((qk_w, T), _bf16),
                   jax.ShapeDtypeStruct((T, D_MODEL), _bf16), jax.ShapeDtypeStruct((T, D_MODEL), _bf16),
                   jax.ShapeDtypeStruct((T, 8), _f32), jax.ShapeDtypeStruct((T, 8), _f32)],
        compiler_params=_params("parallel"),
        name="mlstm_in_proj",
    )(x2, x2, head, gain.reshape(1, D_MODEL), wq, wkt, wv, wo, wg_hi, wg_lo, bgate)


def _mlstm_core_kernel(q_ref, kt_ref, v_ref, o_ref, ut_ref, b_ref, gout_ref, out_ref, state_ref, m_ref):
    C = M_CHUNK
    n_chunks = q_ref.shape[0] // C

    @pl.when(pl.program_id(1) == 0)
    def _():
        state_ref[...] = jnp.zeros_like(state_ref)
        m_ref[...] = jnp.zeros_like(m_ref)

    r = lax.broadcasted_iota(jnp.int32, (C, C), 0)
    c = lax.broadcasted_iota(jnp.int32, (C, C), 1)
    causal = c <= r
    ones_col = jnp.where(lax.broadcasted_iota(jnp.int32, (C, LANES), 1) == 0, 1.0, 0.0).astype(_bf16)

    H = range(M_HEADS)
    S = [state_ref[h] for h in H]
    m = [m_ref[h][0:1, 0:1] for h in H]
    for ci in range(n_chunks):
        rows = slice(ci * C, (ci + 1) * C)
        qc = [q_ref[rows, h * M_DK:(h + 1) * M_DK] for h in H]
        ktc = [kt_ref[h * M_DK:(h + 1) * M_DK, rows] for h in H]
        v_aug = [jnp.concatenate([v_ref[rows, h * M_DV:(h + 1) * M_DV], ones_col], axis=1) for h in H]
        u_row = [ut_ref[h:h + 1, rows] for h in H]
        b_col = [b_ref[rows, h:h + 1] for h in H]
        qk = [_dot(qc[h], ktc[h]) for h in H]
        qs = [_dot(qc[h], S[h].astype(_bf16)) for h in H]
        cu_col = [jnp.max(jnp.where(causal, u_row[h], NEG), axis=1, keepdims=True) for h in H]
        M = [jnp.maximum(cu_col[h], m[h]) for h in H]
        s_mat = [(qk[h] * jnp.where(causal, jnp.exp(u_row[h] - M[h]), 0.0)).astype(_bf16) for h in H]
        tot = [_dot(s_mat[h], v_aug[h]) + jnp.exp(m[h] - M[h]) * qs[h] for h in H]
        for h in H:
            g_tot = b_col[h][C - 1:C, :]
            cu_last = cu_col[h][C - 1:C, :]
            m_loc = g_tot + cu_last
            m_new = jnp.maximum(g_tot + m[h], m_loc)
            ktw = (ktc[h].astype(_f32) * jnp.exp(u_row[h] - cu_last)).astype(_bf16)
            kv = _dot(ktw, v_aug[h])
            S[h] = jnp.exp(g_tot + m[h] - m_new) * S[h] + jnp.exp(m_loc - m_new) * kv
            m[h] = m_new
        for h in H:
            num = tot[h][:, :M_DV]
            den = tot[h][:, M_DV:M_DV + 1]
            hout = num / jnp.maximum(jnp.abs(den), jnp.exp(-(b_col[h] + M[h])))
            y = hout * lax.rsqrt(jnp.mean(hout * hout, axis=-1, keepdims=True) + EPS)
            y = y * gout_ref[:, h * M_DV:(h + 1) * M_DV] * o_ref[rows, h * M_DV:(h + 1) * M_DV].astype(_f32)
            out_ref[rows, h * M_DV:(h + 1) * M_DV] = y.astype(out_ref.dtype)
    for h in H:
        state_ref[h] = S[h]
        m_ref[h] = jnp.broadcast_to(m[h], m_ref.shape[1:])


def _mlstm_core(q, kt, v, o, ut, bcol, g_out, batch, batch_rows):
    R = MLSTM_ROWS
    spb = batch_rows // R
    row = lambda n: pl.BlockSpec((R, n), lambda b, s: (b * spb + s, 0))
    col = lambda n: pl.BlockSpec((n, R), lambda b, s: (0, b * spb + s))
    qk_w = M_HEADS * M_DK
    return pl.pallas_call(
        _mlstm_core_kernel,
        grid=(batch, spb),
        in_specs=[row(qk_w), col(qk_w), row(D_MODEL), row(D_MODEL), col(8), row(8), _const_spec((1, D_MODEL))],
        out_specs=row(D_MODEL),
        out_shape=jax.ShapeDtypeStruct((batch * batch_rows, D_MODEL), _bf16),
        scratch_shapes=[pltpu.VMEM((M_HEADS, M_DK, M_DV + LANES), _f32), pltpu.VMEM((M_HEADS, 8, LANES), _f32)],
        compiler_params=_params("parallel", "arbitrary"),
        name="mlstm_core",
    )(q, kt, v, o, ut, bcol, g_out.reshape(1, D_MODEL))


MOE_CHUNK = 288
MOE_MAX_CHUNKS = (MOE_ROW_TILE - 1) // MOE_CHUNK + N_GROUPS
MOE_SORTED_ROWS = 2048
MOE_XG_W = D_MODEL + 2 * LANES


def _route(logits):
    lane = lax.broadcasted_iota(jnp.int32, logits.shape, 1)
    is_g = lane < N_GROUPS
    gl = jnp.where(is_g, logits, -jnp.inf)
    gmax = jnp.max(gl, axis=1, keepdims=True)
    top_g = jnp.min(jnp.where(gl == gmax, lane, LANES), axis=1, keepdims=True)
    p_g = 1.0 / jnp.sum(jnp.where(is_g, jnp.exp(gl - gmax), 0.0), axis=1, keepdims=True)
    e_lane = lane - N_GROUPS
    in_group = (e_lane >= top_g * EXP_PER_GROUP) & (e_lane < (top_g + 1) * EXP_PER_GROUP)
    el = jnp.where(in_group, logits, -jnp.inf)
    v1 = jnp.max(el, axis=1, keepdims=True)
    i1 = jnp.min(jnp.where(el == v1, lane, LANES), axis=1, keepdims=True)
    el2 = jnp.where(lane == i1, -jnp.inf, el)
    v2 = jnp.max(el2, axis=1, keepdims=True)
    i2 = jnp.min(jnp.where(el2 == v2, lane, LANES), axis=1, keepdims=True)
    e2 = jnp.exp(v2 - v1)
    p1 = p_g / (1.0 + e2)
    p2 = p_g * e2 / (1.0 + e2)
    return jnp.where(lane == i1, p1, jnp.where(lane == i2, p2, 0.0)), top_g


def _router_kernel(y_ref, wo_ref, res_lo_ref, res_hi_ref, g_ref, wr_hi_ref, wr_lo_ref, br_ref, tril_ref,
                   h_ref, xg_ref, pos_ref, meta_ref):
    hr = res_lo_ref.shape[0]
    halves = (slice(0, hr), slice(hr, 2 * hr))
    x = [r[...] + _dot(y_ref[sl, :], wo_ref[...]) for r, sl in zip((res_lo_ref, res_hi_ref), halves)]
    xn = [_rmsnorm(xk, g_ref[...]) for xk in x]
    logits = [_dot_x3(xk, wr_hi_ref[...], wr_lo_ref[...]) + br_ref[...] for xk in xn]
    for xk, xnk, sl in zip(x, xn, halves):
        h_ref[sl, :] = xk
        xg_ref[sl, :D_MODEL] = xnk.astype(_bf16)
    routed = [_route(lg) for lg in logits]
    lane = lax.broadcasted_iota(jnp.int32, (hr, LANES), 1)
    in_g = [lane == top_g for _, top_g in routed]
    onehot = [jnp.where(m, 1.0, 0.0) for m in in_g]
    incl = [_dot(tril_ref[...], oh.astype(_bf16)) for oh in onehot]
    incl[1] = incl[1] + incl[0][hr - 1:hr, :]
    for (gates, _), sl in zip(routed, halves):
        g_hi, g_lo = _split2(gates)
        xg_ref[sl, D_MODEL:D_MODEL + LANES] = g_hi
        xg_ref[sl, D_MODEL + LANES:] = g_lo
    counts = incl[1][hr - 1:hr, :]
    nchunk = jnp.floor((counts + (MOE_CHUNK - 1)) * (1.0 / MOE_CHUNK) + 0.5 / MOE_CHUNK)
    first = pltpu.roll(nchunk, 1, 1) + pltpu.roll(nchunk, 2, 1) + pltpu.roll(nchunk, 3, 1)
    for m, oh, inc, sl in zip(in_g, onehot, incl, halves):
        base = MOE_CHUNK * first + (inc - oh)
        pos_ref[sl, :] = jnp.sum(jnp.where(m, base, 0.0), axis=1, keepdims=True)
    end = first + nchunk
    lane1 = lax.broadcasted_iota(jnp.int32, (1, LANES), 1).astype(_f32)
    grp_of = jnp.zeros((1, LANES), _f32)
    for g in range(N_GROUPS - 1):
        grp_of = grp_of + jnp.where(lane1 >= end[:, g:g + 1], 1.0, 0.0)
    meta = jnp.where(lane1 == MOE_MAX_CHUNKS, end[:, N_GROUPS - 1:N_GROUPS], grp_of)
    meta_ref[...] = jnp.broadcast_to(meta, meta_ref.shape).astype(jnp.int32)


def _experts_kernel(grp_ref, cnt_ref, xg_ref, posr_ref, posc_ref, h_ref, wg_ref, wu_ref, wd_ref, out_ref, ys_ref):
    i = pl.program_id(0)
    c = pl.program_id(1)
    tm = xg_ref.shape[0]
    chunk_rows = pl.ds(pl.multiple_of(jnp.minimum(c, MOE_MAX_CHUNKS - 1) * MOE_CHUNK, 16), MOE_CHUNK)

    @pl.when(c == 0)
    def _():
        tail = MOE_MAX_CHUNKS * MOE_CHUNK
        ys_ref[tail:, :] = jnp.zeros((MOE_SORTED_ROWS - tail, D_MODEL), _bf16)

    @pl.when(c < cnt_ref[i])
    def _():
        grp = grp_ref[i * MOE_MAX_CHUNKS + c]
        first_row = (c * MOE_CHUNK).astype(_f32)
        r_id = lax.broadcasted_iota(jnp.int32, (MOE_CHUNK, tm), 0).astype(_f32) + first_row
        sel = jnp.where(posr_ref[...] == r_id, 1.0, 0.0).astype(_bf16)
        xg = _dot(sel, xg_ref[...])
        xs = xg[:, :D_MODEL].astype(_bf16)
        gs = xg[:, D_MODEL:D_MODEL + LANES] + xg[:, D_MODEL + LANES:]
        lane = lax.broadcasted_iota(jnp.int32, gs.shape, 1)
        acc = jnp.zeros((MOE_CHUNK, D_MODEL), _f32)
        for e in range(EXP_PER_GROUP):
            gate_e = jnp.sum(jnp.where(lane == N_GROUPS + grp * EXP_PER_GROUP + e, gs, 0.0), axis=1, keepdims=True)
            hg = _dot(xs, wg_ref[e])
            hu = _dot(xs, wu_ref[e])
            act = (hg * jax.nn.sigmoid(hg) * hu * gate_e).astype(_bf16)
            acc = acc + _dot(act, wd_ref[e])
        ys_ref[chunk_rows, :] = acc.astype(_bf16)

    @pl.when((c >= cnt_ref[i]) & (c < MOE_MAX_CHUNKS))
    def _():
        ys_ref[chunk_rows, :] = jnp.zeros((MOE_CHUNK, D_MODEL), _bf16)

    @pl.when(c == MOE_MAX_CHUNKS)
    def _():
        s_id = lax.broadcasted_iota(jnp.int32, (tm, MOE_SORTED_ROWS), 1).astype(_f32)
        sel_t = jnp.where(posc_ref[...] == s_id, 1.0, 0.0).astype(_bf16)
        out_ref[...] = h_ref[...] + _dot(sel_t, ys_ref[...])


def _mix_out_moe(y, w_out, res, res_batch_rows, res_row0, seq, gain, w_group, b_group, w_router, b_router,
                 w_gate, w_up, w_down):
    T = y.shape[0]
    tm = MOE_ROW_TILE
    half = tm // 2
    tiles = T // tm
    assert T % tm == 0 and seq % half == 0 and res_batch_rows % half == 0 and res_row0 % half == 0
    assert MOE_MAX_CHUNKS * MOE_CHUNK <= MOE_SORTED_ROWS and MOE_CHUNK % 16 == 0
    wr = jnp.zeros((D_MODEL, LANES), _f32)
    wr_hi, wr_lo = _split2(wr.at[:, :N_GROUPS].set(w_group).at[:, N_GROUPS:N_GROUPS + N_EXPERTS].set(w_router))
    br = jnp.zeros((1, LANES), _f32)
    br = br.at[0, :N_GROUPS].set(b_group).at[0, N_GROUPS:N_GROUPS + N_EXPERTS].set(b_router)
    tril = jnp.asarray(np.tril(np.ones((half, half), np.float32)), _bf16)
    row = lambda n: pl.BlockSpec((tm, n), lambda i: (i, 0))

    def res_half(k):
        def index(i):
            r = 2 * i + k
            b = r // (seq // half)
            return (b * (res_batch_rows // half) + res_row0 // half + r - b * (seq // half), 0)
        return pl.BlockSpec((half, D_MODEL), index)

    h, xg, pos, meta = pl.pallas_call(
        _router_kernel,
        grid=(tiles,),
        in_specs=[row(D_MODEL), _const_spec((D_MODEL, D_MODEL)), res_half(0), res_half(1), _const_spec((1, D_MODEL)),
                  _const_spec((D_MODEL, LANES)), _const_spec((D_MODEL, LANES)), _const_spec((1, LANES)),
                  _const_spec((half, half))],
        out_specs=[row(D_MODEL), row(MOE_XG_W), row(1), pl.BlockSpec((8, LANES), lambda i: (i, 0))],
        out_shape=[jax.ShapeDtypeStruct((T, D_MODEL), _f32), jax.ShapeDtypeStruct((T, MOE_XG_W), _bf16),
                   jax.ShapeDtypeStruct((T, 1), _f32), jax.ShapeDtypeStruct((tiles * 8, LANES), jnp.int32)],
        compiler_params=_params("parallel"),
        name="mix_out_router",
    )(y, w_out.astype(_bf16), res, res, gain.reshape(1, D_MODEL), wr_hi, wr_lo, br, tril)
    meta = meta.reshape(tiles, 8, LANES)[:, 0]
    chunk_group = meta[:, :MOE_MAX_CHUNKS].reshape(-1)
    n_chunks = meta[:, MOE_MAX_CHUNKS]

    def wspec(shape):
        def index(i, c, grp, cnt):
            return (grp[i * MOE_MAX_CHUNKS + jnp.minimum(c, cnt[i] - 1)], 0, 0)
        return pl.BlockSpec(shape, index)

    tile = lambda n: pl.BlockSpec((tm, n), lambda i, c, grp, cnt: (i, 0))
    return pl.pallas_call(
        _experts_kernel,
        grid_spec=pltpu.PrefetchScalarGridSpec(
            num_scalar_prefetch=2,
            grid=(tiles, MOE_MAX_CHUNKS + 1),
            in_specs=[tile(MOE_XG_W), pl.BlockSpec((None, 1, tm), lambda i, c, grp, cnt: (i, 0, 0)),
                      tile(1), tile(D_MODEL), wspec((EXP_PER_GROUP, D_MODEL, D_EXPERT)),
                      wspec((EXP_PER_GROUP, D_MODEL, D_EXPERT)), wspec((EXP_PER_GROUP, D_EXPERT, D_MODEL))],
            out_specs=tile(D_MODEL),
            scratch_shapes=[pltpu.VMEM((MOE_SORTED_ROWS, D_MODEL), _bf16)]),
        out_shape=jax.ShapeDtypeStruct((T, D_MODEL), _f32),
        compiler_params=_params("parallel", "arbitrary"),
        name="moe_experts",
    )(chunk_group, n_chunks, xg, pos.reshape(tiles, 1, tm), pos, h, w_gate.astype(_bf16),
      w_up.astype(_bf16), w_down.astype(_bf16))


SEG_W = 256


def _segment_ones():
    return jnp.kron(jnp.eye(SEG_W // F_HD, dtype=_f32), jnp.ones((F_HD, F_HD), _f32)).astype(_bf16)


def _head_place_matrices():
    pairs = F_HEADS // 2
    place = np.zeros((pairs, 2 * LANES, 2 * LANES), np.float32)
    i = np.arange(F_HD)
    for p in range(pairs):
        place[p, i, i] = 1.0
        place[p, F_HD + i, LANES + i] = 1.0
        for hh in range(2):
            for t in range(3):
                place[p, LANES + F_HEADS * t + 2 * p + hh, hh * LANES + F_HD + t] = 1.0
    return jnp.asarray(place, _bf16)


def _heads_rmsnorm(x, seg, g):
    outs = []
    for c in range(x.shape[1] // SEG_W):
        xs = x[:, c * SEG_W:(c + 1) * SEG_W]
        sq_hi, sq_lo = _split2(xs * xs)
        ms = (_dot(sq_hi, seg) + _dot(sq_lo, seg)) * (1.0 / F_HD)
        outs.append(xs * lax.rsqrt(ms + EPS) * g[:, c * SEG_W:(c + 1) * SEG_W])
    return jnp.concatenate(outs, axis=1)


def _kv_kernel(h_ref, g_ref, wk_ref, wvt_ref, wf_hi_ref, wf_lo_ref, bf_ref, gk_ref, seg_ref, place_ref,
               k_ref, vt_ref, carry_ref):
    j = pl.program_id(1)

    @pl.when(j == 0)
    def _():
        carry_ref[...] = jnp.zeros_like(carry_ref)

    x = h_ref[...]
    rows = x.shape[0]
    xn = _rmsnorm(x, g_ref[...])
    xb = xn.astype(_bf16)
    vt = _dot_nt(wvt_ref[...], xb).astype(_bf16)
    for h in range(F_HEADS):
        vt_ref[h, :F_HD, :] = vt[h * F_HD:(h + 1) * F_HD, :]
        vt_ref[h, F_HD:, :] = jnp.ones((V_ROWS - F_HD, rows), _bf16)
    kn = _heads_rmsnorm(_dot(xb, wk_ref[...]), seg_ref[...], gk_ref[...]).astype(_bf16)
    lf = _log_sigmoid(_dot_x3(xn, wf_hi_ref[...], wf_lo_ref[...]) + bf_ref[...])
    row_in_batch = j * rows + lax.broadcasted_iota(jnp.int32, lf.shape, 0)
    lf = jnp.where(row_in_batch < PAD_FRONT, 0.0, lf)
    r = lax.broadcasted_iota(jnp.int32, (LANES, LANES), 0)
    c = lax.broadcasted_iota(jnp.int32, (LANES, LANES), 1)
    tril = jnp.where(c <= r, 1.0, 0.0).astype(_bf16)
    carry = carry_ref[0:1, :]
    terms = []
    for sb in range(rows // LANES):
        cs = _dot_ones3(tril, lf[sb * LANES:(sb + 1) * LANES, :]) + carry
        carry = cs[LANES - 1:LANES, :]
        f = (-LOG2E) * cs
        f1 = f.astype(_bf16).astype(_f32)
        f2 = (f - f1).astype(_bf16).astype(_f32)
        f3 = (f - f1) - f2
        terms.append(jnp.where(c < F_HEADS, f1, jnp.where(c < 2 * F_HEADS, pltpu.roll(f2, F_HEADS, 1),
                                                          jnp.where(c < 3 * F_HEADS, pltpu.roll(f3, 2 * F_HEADS, 1), 0.0))))
    carry_ref[...] = jnp.broadcast_to(carry, carry_ref.shape)
    fterms = jnp.concatenate(terms, axis=0).astype(_bf16)
    for p in range(F_HEADS // 2):
        lhs = jnp.concatenate([kn[:, p * LANES:(p + 1) * LANES], fterms], axis=1)
        k_ref[:, 2 * p * LANES:2 * (p + 1) * LANES] = _dot(lhs, place_ref[p]).astype(_bf16)


def _shared_kv(h, gain, w_kv, b_f, g_k, batch, batch_rows):
    tm = ROW_TILE
    spb = batch_rows // tm
    T = batch * batch_rows
    wk = w_kv[:, :D_MODEL].astype(_bf16)
    wvt = w_kv[:, D_MODEL:2 * D_MODEL].T.astype(_bf16)
    wf_hi, wf_lo = _split2(jnp.zeros((D_MODEL, LANES), _f32).at[:, :F_HEADS].set(w_kv[:, 2 * D_MODEL:]))
    bf = jnp.zeros((1, LANES), _f32).at[0, :F_HEADS].set(b_f)
    row = lambda n: pl.BlockSpec((tm, n), lambda b, j: (b * spb + j, 0))
    wide = F_HEADS * LANES
    ka, vt = pl.pallas_call(
        _kv_kernel,
        grid=(batch, spb),
        in_specs=[row(D_MODEL), _const_spec((1, D_MODEL)), _const_spec((D_MODEL, D_MODEL)),
                  _const_spec((D_MODEL, D_MODEL)), _const_spec((D_MODEL, LANES)), _const_spec((D_MODEL, LANES)),
                  _const_spec((1, LANES)), _const_spec((1, D_MODEL)), _const_spec((SEG_W, SEG_W)),
                  _const_spec((F_HEADS // 2, 2 * LANES, 2 * LANES))],
        out_specs=[row(wide), pl.BlockSpec((None, F_HEADS, V_ROWS, tm), lambda b, j: (b, 0, 0, j))],
        out_shape=[jax.ShapeDtypeStruct((T, wide), _bf16),
                   jax.ShapeDtypeStruct((batch, F_HEADS, V_ROWS, batch_rows), _bf16)],
        scratch_shapes=[pltpu.VMEM((8, LANES), _f32)],
        compiler_params=_params("parallel", "arbitrary"),
        name="shared_kv",
    )(h, gain.reshape(1, D_MODEL), wk, wvt, wf_hi, wf_lo, bf, jnp.tile(g_k, F_HEADS).reshape(1, D_MODEL),
      _segment_ones(), _head_place_matrices())
    return ka.reshape(batch, batch_rows, wide), vt


def _q_kernel(h_ref, g_ref, wq_ref, gq_ref, seg_ref, place_ref, q_ref):
    xb = _rmsnorm(h_ref[...], g_ref[...]).astype(_bf16)
    qn = _heads_rmsnorm(_dot(xb, wq_ref[...]), seg_ref[...], gq_ref[...])
    qn = (qn * (F_HD ** -0.5 * LOG2E)).astype(_bf16)
    row = lax.broadcasted_iota(jnp.int32, (2 * LANES, 1), 0) % LANES
    ones = jnp.where((row >= F_HD) & (row < F_HD + 3), 1.0, 0.0)
    spread_t = place_ref[0, :LANES, :].T
    for p in range(F_HEADS // 2):
        spread = _dot_nt(spread_t, qn[:, p * LANES:(p + 1) * LANES])
        q_ref[2 * p * LANES:2 * (p + 1) * LANES, :] = (spread + ones).astype(_bf16)


def _q_proj(h, gain, w_q, g_q, batch, seq, batch_rows, row0):
    tm = ROW_TILE
    spb = seq // tm
    wide = F_HEADS * LANES
    qa = pl.pallas_call(
        _q_kernel,
        grid=(batch, spb),
        in_specs=[pl.BlockSpec((tm, D_MODEL), lambda b, j: (b * (batch_rows // tm) + row0 // tm + j, 0)),
                  _const_spec((1, D_MODEL)), _const_spec((D_MODEL, D_MODEL)), _const_spec((1, D_MODEL)),
                  _const_spec((SEG_W, SEG_W)), _const_spec((F_HEADS // 2, 2 * LANES, 2 * LANES))],
        out_specs=pl.BlockSpec((None, wide, tm), lambda b, j: (b, 0, j)),
        out_shape=jax.ShapeDtypeStruct((batch, wide, seq), _bf16),
        compiler_params=_params("parallel", "parallel"),
        name="attn_q_proj",
    )(h, gain.reshape(1, D_MODEL), w_q.astype(_bf16), jnp.tile(g_q, F_HEADS).reshape(1, D_MODEL), _segment_ones(),
      _head_place_matrices())
    return qa


def _attn_kernel(q_ref, k_ref, vt_ref, out_ref, sa_ref, sb_ref, xa_ref, xb_ref, m_ref, acc_ref):
    iq = pl.program_id(2)
    tq, tk = ATT_TQ, ATT_TK
    tile = 2 * LANES
    n_qt = tq // tile
    m_ref[...] = jnp.full(m_ref.shape, NEG, _f32)
    acc_ref[...] = jnp.zeros_like(acc_ref)

    def scores(ks, rows, hh):
        return _dot(k_ref[pl.ds(ks, rows), hh * LANES:(hh + 1) * LANES], q_ref[hh * LANES:(hh + 1) * LANES, :])

    def issue_head(jk, s_ref, x_ref, hh):
        st = scores(pl.multiple_of(jk * tk, tk), tk, hh)
        s_ref[hh] = st
        x_ref[hh] = jnp.max(st, axis=0, keepdims=True)

    def issue(jk, s_ref, x_ref):
        for hh in range(2):
            issue_head(jk, s_ref, x_ref, hh)

    def consume(st, st_max, ks, rows, hh, q_tiles=None, diag_tile=None, lane0=0):
        for qh in (range(n_qt) if q_tiles is None else q_tiles):
            ql = slice(qh * tile, (qh + 1) * tile)
            sl = slice(qh * tile - lane0, (qh + 1) * tile - lane0)
            m_prev = m_ref[hh, :, ql]
            m_new = jnp.maximum(m_prev, st_max[:, sl])
            alpha = jnp.exp2(m_prev - m_new)
            pv = None
            for kh in range(rows // tile) if rows >= tile else range(1):
                if diag_tile is not None and kh > qh - diag_tile:
                    continue
                kr = min(tile, rows)
                pt = jnp.exp2((st[kh * kr:(kh + 1) * kr, sl] - m_new).astype(_bf16))
                d = _dot(vt_ref[hh, :, pl.ds(ks + kh * kr, kr)], pt)
                pv = d if pv is None else pv + d
            acc_ref[hh, :, ql] = alpha * acc_ref[hh, :, ql] + pv
            m_ref[hh, :, ql] = m_new

    def consume_diag(jk, s_ref, hh, q0):
        ks = pl.multiple_of(jk * tk, tk)
        k_idx = lax.broadcasted_iota(jnp.int32, (tk, tk), 0)
        q_idx = lax.broadcasted_iota(jnp.int32, (tk, tk), 1)
        st = jnp.where(k_idx <= q_idx, s_ref[hh, :, q0:q0 + tk], NEG)
        consume(st, jnp.max(st, axis=0, keepdims=True), ks, tk, hh,
                q_tiles=range(q0 // tile, (q0 + tk) // tile), diag_tile=q0 // tile, lane0=q0)

    def step(j_issue, si_ref, xi_ref, j_cons, sc_ref, xc_ref):
        ks = pl.multiple_of(j_cons * tk, tk)
        for hh in range(2):
            issue_head(j_issue, si_ref, xi_ref, hh)
            consume(sc_ref[hh], xc_ref[hh], ks, tk, hh)

    st_meta = [scores(PAD_FRONT, N_META, hh) for hh in range(2)]
    issue(1, sa_ref, xa_ref)
    for hh in range(2):
        consume(st_meta[hh], jnp.max(st_meta[hh], axis=0, keepdims=True), PAD_FRONT, N_META, hh)

    def body(t, carry):
        j = 2 * t + 1
        step(j + 1, sb_ref, xb_ref, j, sa_ref, xa_ref)
        step(j + 2, sa_ref, xa_ref, j + 1, sb_ref, xb_ref)
        return carry

    lax.fori_loop(0, iq, body, 0)
    j_diag = 2 * iq + 1
    ks = pl.multiple_of(j_diag * tk, tk)
    ks_next = pl.multiple_of((j_diag + 1) * tk, tk)
    for hh in range(2):
        sb_ref[hh, :, tk:] = _dot(k_ref[pl.ds(ks_next, tk), hh * LANES:(hh + 1) * LANES],
                                  q_ref[hh * LANES:(hh + 1) * LANES, tk:])
        consume_diag(j_diag, sa_ref, hh, 0)
        consume(sa_ref[hh], xa_ref[hh], ks, tk, hh, q_tiles=range(tk // tile, n_qt))
    for hh in range(2):
        consume_diag(j_diag + 1, sb_ref, hh, tk)

    ot = jnp.concatenate([acc_ref[hh, :F_HD, :] / acc_ref[hh, F_HD:F_HD + 1, :] for hh in range(2)], axis=0)
    out_ref[...] = ot.T.astype(out_ref.dtype)


def _attention(q, k, vt, batch, seq, batch_rows):
    assert ATT_TQ == 2 * ATT_TK and REAL0 == ATT_TK
    pairs = F_HEADS // 2
    nq = seq // ATT_TQ
    return pl.pallas_call(
        _attn_kernel,
        grid=(batch, pairs, nq),
        in_specs=[pl.BlockSpec((None, 2 * LANES, ATT_TQ), lambda b, p, i: (b, p, i)),
                  pl.BlockSpec((None, batch_rows, 2 * LANES), lambda b, p, i: (b, 0, p)),
                  pl.BlockSpec((None, 2, V_ROWS, batch_rows), lambda b, p, i: (b, p, 0, 0))],
        out_specs=pl.BlockSpec((None, ATT_TQ, LANES), lambda b, p, i: (b, i, p)),
        out_shape=jax.ShapeDtypeStruct((batch, seq, D_MODEL), _bf16),
        scratch_shapes=[pltpu.VMEM((2, ATT_TK, ATT_TQ), _f32), pltpu.VMEM((2, ATT_TK, ATT_TQ), _f32),
                        pltpu.VMEM((2, 1, ATT_TQ), _f32), pltpu.VMEM((2, 1, ATT_TQ), _f32),
                        pltpu.VMEM((2, 1, ATT_TQ), _f32), pltpu.VMEM((2, V_ROWS, ATT_TQ), _f32)],
        compiler_params=_params("parallel", "parallel", "arbitrary"),
        name="fox_attention",
    )(q, k, vt)


def kernel(x, meta_tokens, norm_mix, norm_ffn, m_w_in, m_b_gate, m_g_out, m_w_out, kv_norm, kv_w, kv_b_f, k_norm,
           f_w_q, f_q_norm, f_w_o, moe_w_group, moe_b_group, moe_w_router, moe_b_router, moe_w_gate, moe_w_up,
           moe_w_down):
    B, S, D = x.shape
    depth = norm_mix.shape[0]
    n_a = m_w_in.shape[0]
    LP = REAL0 + S
    assert D == D_MODEL and S % ATT_TQ == 0 and LP % ROW_TILE == 0 and meta_tokens.shape[0] == N_META
    assert (B * LP) % MOE_ROW_TILE == 0 and (B * S) % MOE_ROW_TILE == 0
    assert depth > n_a and REAL0 % M_CHUNK == 0 and MLSTM_ROWS % M_CHUNK == 0
    assert n_a == 1
    head = jnp.concatenate([jnp.zeros((PAD_FRONT, D), x.dtype), meta_tokens.astype(x.dtype)], axis=0)
    k_sh = vt_sh = None
    rows_per_batch, row0, seq = LP, 0, LP
    for l in range(depth):
        if l < n_a:
            h, q, kt, v, o, u, bcol = _mlstm_in_proj(x, head, norm_mix[l], m_w_in[l], m_b_gate[l], LP)
            y = _mlstm_core(q, kt, v, o, u.T, bcol, m_g_out[l].reshape(-1), B, LP)
            w_out = m_w_out[l]
        else:
            if l == n_a:
                k_sh, vt_sh = _shared_kv(h, kv_norm, kv_w, kv_b_f, k_norm, B, LP)
                row0, seq = REAL0, S
            j = l - n_a
            q = _q_proj(h, norm_mix[l], f_w_q[j], f_q_norm[j], B, S, rows_per_batch, row0)
            y = _attention(q, k_sh, vt_sh, B, S, LP).reshape(B * S, D)
            w_out = f_w_o[j]
        h = _mix_out_moe(y, w_out, h, rows_per_batch, row0, seq, norm_ffn[l], moe_w_group[l], moe_b_group[l],
                         moe_w_router[l], moe_b_router[l], moe_w_gate[l], moe_w_up[l], moe_w_down[l])
        rows_per_batch, row0 = seq, 0
    return h.reshape(B, S, D)
```

```python
import functools

import jax
import numpy as np
import jax.numpy as jnp
from jax import lax
from jax.experimental import pallas as pl
from jax.experimental.pallas import tpu as pltpu

D_MODEL = 1024
N_META = 16
M_HEADS = 4
M_DV = D_MODEL // M_HEADS
M_DK = M_DV // 2
M_CHUNK = 256
GATE_CAP = 15.0
F_HEADS = 16
F_HD = D_MODEL // F_HEADS
N_GROUPS = 4
EXP_PER_GROUP = 4
N_EXPERTS = N_GROUPS * EXP_PER_GROUP
D_EXPERT = D_MODEL // 4
EPS = 1e-6
NEG = -1e30

LANES = 128
ROW_TILE = 512
IN_ROW_TILE = 1024
MOE_ROW_TILE = 1024
REAL0 = 512
PAD_FRONT = REAL0 - N_META
MLSTM_ROWS = 512
V_ROWS = F_HD + 16
ATT_TQ = 1024
ATT_TK = 512
LOG2E = 1.4426950408889634
VMEM_LIMIT = 56 * 1024 * 1024

_f32 = jnp.float32
_bf16 = jnp.bfloat16


def _dot(a, b):
    return jnp.dot(a, b, preferred_element_type=_f32)


def _split2(w):
    hi = w.astype(_bf16)
    return hi, (w - hi.astype(_f32)).astype(_bf16)


def _dot_x3(x, w_hi, w_lo):
    x_hi, x_lo = _split2(x)
    return _dot(x_hi, w_hi) + _dot(x_lo, w_hi) + _dot(x_hi, w_lo)


def _dot_ones3(ones_mat, x):
    x1 = x.astype(_bf16)
    r1 = x - x1.astype(_f32)
    x2 = r1.astype(_bf16)
    x3 = (r1 - x2.astype(_f32)).astype(_bf16)
    return _dot(ones_mat, x1) + _dot(ones_mat, x2) + _dot(ones_mat, x3)


def _dot_nt(a, b):
    return lax.dot_general(a, b, (((1,), (1,)), ((), ())), preferred_element_type=_f32)


def _rmsnorm(x, g):
    return x * lax.rsqrt(jnp.mean(x * x, axis=-1, keepdims=True) + EPS) * g


def _log_sigmoid(x):
    return jnp.minimum(x, 0.0) - jnp.log1p(jnp.exp(-jnp.abs(x)))


def _params(*sem):
    return pltpu.CompilerParams(dimension_semantics=sem, vmem_limit_bytes=VMEM_LIMIT)


def _const_spec(shape):
    return pl.BlockSpec(shape, lambda *_: (0,) * len(shape))


def _mlstm_in_kernel(n_batch, batch_rows, x_lo_ref, x_hi_ref, head_ref, g_ref, wq_ref, wkt_ref, wv_ref, wo_ref,
                     wg_hi_ref, wg_lo_ref, bgate_ref, h_ref, q_ref, kt_ref, v_ref, o_ref, u_ref, b_ref):
    i = pl.program_id(0)
    half = x_lo_ref.shape[0]
    halves_per_batch = batch_rows // half
    parts = []
    for k, x_ref in enumerate((x_lo_ref, x_hi_ref)):
        is_head = lax.rem(2 * i + k, halves_per_batch) == 0
        parts.append(jnp.where(is_head, head_ref[...], x_ref[...]))
    x = jnp.concatenate(parts, axis=0)
    h_ref[...] = x
    xn = _rmsnorm(x, g_ref[...])
    xb = xn.astype(_bf16)
    q_ref[...] = (_dot(xb, wq_ref[...]) * (M_DK ** -0.5)).astype(_bf16)
    kt_ref[...] = _dot_nt(wkt_ref[...], xb).astype(_bf16)
    v_ref[...] = _dot(xb, wv_ref[...]).astype(_bf16)
    o_ref[...] = jax.nn.sigmoid(_dot(xb, wo_ref[...])).astype(_bf16)
    gates = _dot_x3(xn, wg_hi_ref[...], wg_lo_ref[...]) + bgate_ref[...]
    gates = GATE_CAP * jnp.tanh(gates * (1.0 / GATE_CAP))
    rows = x.shape[0]
    row = i * rows + lax.broadcasted_iota(jnp.int32, (rows, LANES), 0)
    is_pad = row < PAD_FRONT
    for bb in range(1, n_batch):
        is_pad = is_pad | ((row >= bb * batch_rows) & (row < bb * batch_rows + PAD_FRONT))
    li = jnp.where(is_pad, NEG, gates)
    lf = jnp.where(is_pad, 0.0, _log_sigmoid(gates))
    r = lax.broadcasted_iota(jnp.int32, (M_CHUNK, M_CHUNK), 0)
    c = lax.broadcasted_iota(jnp.int32, (M_CHUNK, M_CHUNK), 1)
    tril = jnp.where(c <= r, 1.0, 0.0).astype(_bf16)
    b = jnp.concatenate([_dot_ones3(tril, lf[k * M_CHUNK:(k + 1) * M_CHUNK, :]) for k in range(rows // M_CHUNK)], axis=0)
    b = pltpu.roll(b, LANES - M_HEADS, 1)
    u_ref[...] = (li - b)[:, :8]
    b_ref[...] = b[:, :8]


def _mlstm_in_proj(x, head, gain, w_in, b_gate, batch_rows):
    B, S, _ = x.shape
    T = B * batch_rows
    tm = IN_ROW_TILE
    half = tm // 2
    assert half == REAL0 and T % tm == 0 and S % half == 0 and batch_rows == REAL0 + S
    qk_w = M_HEADS * M_DK
    wq = w_in[:, :qk_w].astype(_bf16)
    wkt = w_in[:, qk_w:2 * qk_w].T.astype(_bf16)
    wv = w_in[:, 2 * qk_w:2 * qk_w + D_MODEL].astype(_bf16)
    wo = w_in[:, 2 * qk_w + D_MODEL:2 * qk_w + 2 * D_MODEL].astype(_bf16)
    n_gate = 2 * M_HEADS
    wg_hi, wg_lo = _split2(jnp.zeros((D_MODEL, LANES), _f32).at[:, :n_gate].set(w_in[:, 2 * qk_w + 2 * D_MODEL:]))
    bgate = jnp.zeros((1, LANES), _f32).at[0, :n_gate].set(b_gate)
    row = lambda n: pl.BlockSpec((tm, n), lambda i: (i, 0))

    def x_half(k):
        def index(i):
            r = 2 * i + k
            b = r // (batch_rows // half)
            j = r - b * (batch_rows // half)
            return (b * (S // half) + jnp.maximum(j - 1, 0), 0)
        return pl.BlockSpec((half, D_MODEL), index)

    x2 = x.reshape(B * S, D_MODEL)
    return pl.pallas_call(
        functools.partial(_mlstm_in_kernel, B, batch_rows),
        grid=(T // tm,),
        in_specs=[x_half(0), x_half(1), _const_spec((half, D_MODEL)), _const_spec((1, D_MODEL)),
                  _const_spec((D_MODEL, qk_w)), _const_spec((qk_w, D_MODEL)), _const_spec((D_MODEL, D_MODEL)),
                  _const_spec((D_MODEL, D_MODEL)), _const_spec((D_MODEL, LANES)), _const_spec((D_MODEL, LANES)),
                  _const_spec((1, LANES))],
        out_specs=[row(D_MODEL), row(qk_w), pl.BlockSpec((qk_w, tm), lambda i: (0, i)), row(D_MODEL), row(D_MODEL),
                   row(8), row(8)],
        out_shape=[jax.ShapeDtypeStruct((T, D_MODEL), _f32),
                   jax.ShapeDtypeStruct((T, qk_w), _bf16), jax.ShapeDtypeStruct((qk_w, T), _bf16),
                   jax.ShapeDtypeStruct((T, D_MODEL), _bf16), jax.ShapeDtypeStruct((T, D_MODEL), _bf16),
                   jax.ShapeDtypeStruct((T, 8), _f32), jax.ShapeDtypeStruct((T, 8), _f32)],
        compiler_params=_params("parallel"),
        name="mlstm_in_proj",
    )(x2, x2, head, gain.reshape(1, D_MODEL), wq, wkt, wv, wo, wg_hi, wg_lo, bgate)


def _mlstm_core_kernel(q_ref, kt_ref, v_ref, o_ref, ut_ref, b_ref, gout_ref, out_ref, state_ref, m_ref):
    C = M_CHUNK
    n_chunks = q_ref.shape[0] // C

    @pl.when(pl.program_id(1) == 0)
    def _():
        state_ref[...] = jnp.zeros_like(state_ref)
        m_ref[...] = jnp.zeros_like(m_ref)

    r = lax.broadcasted_iota(jnp.int32, (C, C), 0)
    c = lax.broadcasted_iota(jnp.int32, (C, C), 1)
    causal = c <= r
    ones_col = jnp.where(lax.broadcasted_iota(jnp.int32, (C, LANES), 1) == 0, 1.0, 0.0).astype(_bf16)

    H = range(M_HEADS)
    S = [state_ref[h] for h in H]
    m = [m_ref[h][0:1, 0:1] for h in H]
    for ci in range(n_chunks):
        rows = slice(ci * C, (ci + 1) * C)
        qc = [q_ref[rows, h * M_DK:(h + 1) * M_DK] for h in H]
        ktc = [kt_ref[h * M_DK:(h + 1) * M_DK, rows] for h in H]
        v_aug = [jnp.concatenate([v_ref[rows, h * M_DV:(h + 1) * M_DV], ones_col], axis=1) for h in H]
        u_row = [ut_ref[h:h + 1, rows] for h in H]
        b_col = [b_ref[rows, h:h + 1] for h in H]
        qk = [_dot(qc[h], ktc[h]) for h in H]
        qs = [_dot(qc[h], S[h].astype(_bf16)) for h in H]
        cu_col = [jnp.max(jnp.where(causal, u_row[h], NEG), axis=1, keepdims=True) for h in H]
        M = [jnp.maximum(cu_col[h], m[h]) for h in H]
        s_mat = [(qk[h] * jnp.where(causal, jnp.exp(u_row[h] - M[h]), 0.0)).astype(_bf16) for h in H]
        tot = [_dot(s_mat[h], v_aug[h]) + jnp.exp(m[h] - M[h]) * qs[h] for h in H]
        for h in H:
            g_tot = b_col[h][C - 1:C, :]
            cu_last = cu_col[h][C - 1:C, :]
            m_loc = g_tot + cu_last
            m_new = jnp.maximum(g_tot + m[h], m_loc)
            ktw = (ktc[h].astype(_f32) * jnp.exp(u_row[h] - cu_last)).astype(_bf16)
            kv = _dot(ktw, v_aug[h])
            S[h] = jnp.exp(g_tot + m[h] - m_new) * S[h] + jnp.exp(m_loc - m_new) * kv
            m[h] = m_new
        for h in H:
            num = tot[h][:, :M_DV]
            den = tot[h][:, M_DV:M_DV + 1]
            hout = num / jnp.maximum(jnp.abs(den), jnp.exp(-(b_col[h] + M[h])))
            y = hout * lax.rsqrt(jnp.mean(hout * hout, axis=-1, keepdims=True) + EPS)
            y = y * gout_ref[:, h * M_DV:(h + 1) * M_DV] * o_ref[rows, h * M_DV:(h + 1) * M_DV].astype(_f32)
            out_ref[rows, h * M_DV:(h + 1) * M_DV] = y.astype(out_ref.dtype)
    for h in H:
        state_ref[h] = S[h]
        m_ref[h] = jnp.broadcast_to(m[h], m_ref.shape[1:])


def _mlstm_core(q, kt, v, o, ut, bcol, g_out, batch, batch_rows):
    R = MLSTM_ROWS
    spb = batch_rows // R
    row = lambda n: pl.BlockSpec((R, n), lambda b, s: (b * spb + s, 0))
    col = lambda n: pl.BlockSpec((n, R), lambda b, s: (0, b * spb + s))
    qk_w = M_HEADS * M_DK
    return pl.pallas_call(
        _mlstm_core_kernel,
        grid=(batch, spb),
        in_specs=[row(qk_w), col(qk_w), row(D_MODEL), row(D_MODEL), col(8), row(8), _const_spec((1, D_MODEL))],
        out_specs=row(D_MODEL),
        out_shape=jax.ShapeDtypeStruct((batch * batch_rows, D_MODEL), _bf16),
        scratch_shapes=[pltpu.VMEM((M_HEADS, M_DK, M_DV + LANES), _f32), pltpu.VMEM((M_HEADS, 8, LANES), _f32)],
        compiler_params=_params("parallel", "arbitrary"),
        name="mlstm_core",
    )(q, kt, v, o, ut, bcol, g_out.reshape(1, D_MODEL))


MOE_CHUNK = 288
MOE_MAX_CHUNKS = (MOE_ROW_TILE - 1) // MOE_CHUNK + N_GROUPS
MOE_SORTED_ROWS = 2048
MOE_XG_W = D_MODEL + 2 * LANES


def _route(logits):
    lane = lax.broadcasted_iota(jnp.int32, logits.shape, 1)
    is_g = lane < N_GROUPS
    gl = jnp.where(is_g, logits, -jnp.inf)
    gmax = jnp.max(gl, axis=1, keepdims=True)
    top_g = jnp.min(jnp.where(gl == gmax, lane, LANES), axis=1, keepdims=True)
    p_g = 1.0 / jnp.sum(jnp.where(is_g, jnp.exp(gl - gmax), 0.0), axis=1, keepdims=True)
    e_lane = lane - N_GROUPS
    in_group = (e_lane >= top_g * EXP_PER_GROUP) & (e_lane < (top_g + 1) * EXP_PER_GROUP)
    el = jnp.where(in_group, logits, -jnp.inf)
    v1 = jnp.max(el, axis=1, keepdims=True)
    i1 = jnp.min(jnp.where(el == v1, lane, LANES), axis=1, keepdims=True)
    el2 = jnp.where(lane == i1, -jnp.inf, el)
    v2 = jnp.max(el2, axis=1, keepdims=True)
    i2 = jnp.min(jnp.where(el2 == v2, lane, LANES), axis=1, keepdims=True)
    e2 = jnp.exp(v2 - v1)
    p1 = p_g / (1.0 + e2)
    p2 = p_g * e2 / (1.0 + e2)
    return jnp.where(lane == i1, p1, jnp.where(lane == i2, p2, 0.0)), top_g


def _router_kernel(y_ref, wo_ref, res_lo_ref, res_hi_ref, g_ref, wr_hi_ref, wr_lo_ref, br_ref, tril_ref,
                   h_ref, xg_ref, pos_ref, meta_ref):
    hr = res_lo_ref.shape[0]
    halves = (slice(0, hr), slice(hr, 2 * hr))
    x = [r[...] + _dot(y_ref[sl, :], wo_ref[...]) for r, sl in zip((res_lo_ref, res_hi_ref), halves)]
    xn = [_rmsnorm(xk, g_ref[...]) for xk in x]
    logits = [_dot_x3(xk, wr_hi_ref[...], wr_lo_ref[...]) + br_ref[...] for xk in xn]
    for xk, xnk, sl in zip(x, xn, halves):
        h_ref[sl, :] = xk
        xg_ref[sl, :D_MODEL] = xnk.astype(_bf16)
    routed = [_route(lg) for lg in logits]
    lane = lax.broadcasted_iota(jnp.int32, (hr, LANES), 1)
    in_g = [lane == top_g for _, top_g in routed]
    onehot = [jnp.where(m, 1.0, 0.0) for m in in_g]
    incl = [_dot(tril_ref[...], oh.astype(_bf16)) for oh in onehot]
    incl[1] = incl[1] + incl[0][hr - 1:hr, :]
    for (gates, _), sl in zip(routed, halves):
        g_hi, g_lo = _split2(gates)
        xg_ref[sl, D_MODEL:D_MODEL + LANES] = g_hi
        xg_ref[sl, D_MODEL + LANES:] = g_lo
    counts = incl[1][hr - 1:hr, :]
    nchunk = jnp.floor((counts + (MOE_CHUNK - 1)) * (1.0 / MOE_CHUNK) + 0.5 / MOE_CHUNK)
    first = pltpu.roll(nchunk, 1, 1) + pltpu.roll(nchunk, 2, 1) + pltpu.roll(nchunk, 3, 1)
    for m, oh, inc, sl in zip(in_g, onehot, incl, halves):
        base = MOE_CHUNK * first + (inc - oh)
        pos_ref[sl, :] = jnp.sum(jnp.where(m, base, 0.0), axis=1, keepdims=True)
    end = first + nchunk
    lane1 = lax.broadcasted_iota(jnp.int32, (1, LANES), 1).astype(_f32)
    grp_of = jnp.zeros((1, LANES), _f32)
    for g in range(N_GROUPS - 1):
        grp_of = grp_of + jnp.where(lane1 >= end[:, g:g + 1], 1.0, 0.0)
    meta = jnp.where(lane1 == MOE_MAX_CHUNKS, end[:, N_GROUPS - 1:N_GROUPS], grp_of)
    meta_ref[...] = jnp.broadcast_to(meta, meta_ref.shape).astype(jnp.int32)


def _experts_kernel(grp_ref, cnt_ref, xg_ref, posr_ref, posc_ref, h_ref, wg_ref, wu_ref, wd_ref, out_ref, ys_ref):
    i = pl.program_id(0)
    c = pl.program_id(1)
    tm = xg_ref.shape[0]
    chunk_rows = pl.ds(pl.multiple_of(jnp.minimum(c, MOE_MAX_CHUNKS - 1) * MOE_CHUNK, 16), MOE_CHUNK)

    @pl.when(c == 0)
    def _():
        tail = MOE_MAX_CHUNKS * MOE_CHUNK
        ys_ref[tail:, :] = jnp.zeros((MOE_SORTED_ROWS - tail, D_MODEL), _bf16)

    @pl.when(c < cnt_ref[i])
    def _():
        grp = grp_ref[i * MOE_MAX_CHUNKS + c]
        first_row = (c * MOE_CHUNK).astype(_f32)
        r_id = lax.broadcasted_iota(jnp.int32, (MOE_CHUNK, tm), 0).astype(_f32) + first_row
        sel = jnp.where(posr_ref[...] == r_id, 1.0, 0.0).astype(_bf16)
        xg = _dot(sel, xg_ref[...])
        xs = xg[:, :D_MODEL].astype(_bf16)
        gs = xg[:, D_MODEL:D_MODEL + LANES] + xg[:, D_MODEL + LANES:]
        lane = lax.broadcasted_iota(jnp.int32, gs.shape, 1)
        acc = jnp.zeros((MOE_CHUNK, D_MODEL), _f32)
        for e in range(EXP_PER_GROUP):
            gate_e = jnp.sum(jnp.where(lane == N_GROUPS + grp * EXP_PER_GROUP + e, gs, 0.0), axis=1, keepdims=True)
            hg = _dot(xs, wg_ref[e])
            hu = _dot(xs, wu_ref[e])
            act = (hg * jax.nn.sigmoid(hg) * hu * gate_e).astype(_bf16)
            acc = acc + _dot(act, wd_ref[e])
        ys_ref[chunk_rows, :] = acc.astype(_bf16)

    @pl.when((c >= cnt_ref[i]) & (c < MOE_MAX_CHUNKS))
    def _():
        ys_ref[chunk_rows, :] = jnp.zeros((MOE_CHUNK, D_MODEL), _bf16)

    @pl.when(c == MOE_MAX_CHUNKS)
    def _():
        s_id = lax.broadcasted_iota(jnp.int32, (tm, MOE_SORTED_ROWS), 1).astype(_f32)
        sel_t = jnp.where(posc_ref[...] == s_id, 1.0, 0.0).astype(_bf16)
        out_ref[...] = h_ref[...] + _dot(sel_t, ys_ref[...])


def _mix_out_moe(y, w_out, res, res_batch_rows, res_row0, seq, gain, w_group, b_group, w_router, b_router,
                 w_gate, w_up, w_down):
    T = y.shape[0]
    tm = MOE_ROW_TILE
    half = tm // 2
    tiles = T // tm
    assert T % tm == 0 and seq % half == 0 and res_batch_rows % half == 0 and res_row0 % half == 0
    assert MOE_MAX_CHUNKS * MOE_CHUNK <= MOE_SORTED_ROWS and MOE_CHUNK % 16 == 0
    wr = jnp.zeros((D_MODEL, LANES), _f32)
    wr_hi, wr_lo = _split2(wr.at[:, :N_GROUPS].set(w_group).at[:, N_GROUPS:N_GROUPS + N_EXPERTS].set(w_router))
    br = jnp.zeros((1, LANES), _f32)
    br = br.at[0, :N_GROUPS].set(b_group).at[0, N_GROUPS:N_GROUPS + N_EXPERTS].set(b_router)
    tril = jnp.asarray(np.tril(np.ones((half, half), np.float32)), _bf16)
    row = lambda n: pl.BlockSpec((tm, n), lambda i: (i, 0))

    def res_half(k):
        def index(i):
            r = 2 * i + k
            b = r // (seq // half)
            return (b * (res_batch_rows // half) + res_row0 // half + r - b * (seq // half), 0)
        return pl.BlockSpec((half, D_MODEL), index)

    h, xg, pos, meta = pl.pallas_call(
        _router_kernel,
        grid=(tiles,),
        in_specs=[row(D_MODEL), _const_spec((D_MODEL, D_MODEL)), res_half(0), res_half(1), _const_spec((1, D_MODEL)),
                  _const_spec((D_MODEL, LANES)), _const_spec((D_MODEL, LANES)), _const_spec((1, LANES)),
                  _const_spec((half, half))],
        out_specs=[row(D_MODEL), row(MOE_XG_W), row(1), pl.BlockSpec((8, LANES), lambda i: (i, 0))],
        out_shape=[jax.ShapeDtypeStruct((T, D_MODEL), _f32), jax.ShapeDtypeStruct((T, MOE_XG_W), _bf16),
                   jax.ShapeDtypeStruct((T, 1), _f32), jax.ShapeDtypeStruct((tiles * 8, LANES), jnp.int32)],
        compiler_params=_params("parallel"),
        name="mix_out_router",
    )(y, w_out.astype(_bf16), res, res, gain.reshape(1, D_MODEL), wr_hi, wr_lo, br, tril)
    meta = meta.reshape(tiles, 8, LANES)[:, 0]
    chunk_group = meta[:, :MOE_MAX_CHUNKS].reshape(-1)
    n_chunks = meta[:, MOE_MAX_CHUNKS]

    def wspec(shape):
        def index(i, c, grp, cnt):
            return (grp[i * MOE_MAX_CHUNKS + jnp.minimum(c, cnt[i] - 1)], 0, 0)
        return pl.BlockSpec(shape, index)

    tile = lambda n: pl.BlockSpec((tm, n), lambda i, c, grp, cnt: (i, 0))
    return pl.pallas_call(
        _experts_kernel,
        grid_spec=pltpu.PrefetchScalarGridSpec(
            num_scalar_prefetch=2,
            grid=(tiles, MOE_MAX_CHUNKS + 1),
            in_specs=[tile(MOE_XG_W), pl.BlockSpec((None, 1, tm), lambda i, c, grp, cnt: (i, 0, 0)),
                      tile(1), tile(D_MODEL), wspec((EXP_PER_GROUP, D_MODEL, D_EXPERT)),
                      wspec((EXP_PER_GROUP, D_MODEL, D_EXPERT)), wspec((EXP_PER_GROUP, D_EXPERT, D_MODEL))],
            out_specs=tile(D_MODEL),
            scratch_shapes=[pltpu.VMEM((MOE_SORTED_ROWS, D_MODEL), _bf16)]),
        out_shape=jax.ShapeDtypeStruct((T, D_MODEL), _f32),
        compiler_params=_params("parallel", "arbitrary"),
        name="moe_experts",
    )(chunk_group, n_chunks, xg, pos.reshape(tiles, 1, tm), pos, h, w_gate.astype(_bf16),
      w_up.astype(_bf16), w_down.astype(_bf16))


SEG_W = 256


def _segment_ones():
    return jnp.kron(jnp.eye(SEG_W // F_HD, dtype=_f32), jnp.ones((F_HD, F_HD), _f32)).astype(_bf16)


def _head_place_matrices():
    pairs = F_HEADS // 2
    place = np.zeros((pairs, 2 * LANES, 2 * LANES), np.float32)
    i = np.arange(F_HD)
    for p in range(pairs):
        place[p, i, i] = 1.0
        place[p, F_HD + i, LANES + i] = 1.0
        for hh in range(2):
            for t in range(3):
                place[p, LANES + F_HEADS * t + 2 * p + hh, hh * LANES + F_HD + t] = 1.0
    return jnp.asarray(place, _bf16)


def _heads_rmsnorm(x, seg, g):
    outs = []
    for c in range(x.shape[1] // SEG_W):
        xs = x[:, c * SEG_W:(c + 1) * SEG_W]
        sq_hi, sq_lo = _split2(xs * xs)
        ms = (_dot(sq_hi, seg) + _dot(sq_lo, seg)) * (1.0 / F_HD)
        outs.append(xs * lax.rsqrt(ms + EPS) * g[:, c * SEG_W:(c + 1) * SEG_W])
    return jnp.concatenate(outs, axis=1)


def _kvq_kernel(h_ref, g_ref, wk_ref, wvt_ref, wf_hi_ref, wf_lo_ref, bf_ref, gk_ref, seg_ref, place_ref,
                gmix_ref, wq_ref, gq_ref, k_ref, vt_ref, q_ref, carry_ref):
    j = pl.program_id(1)

    @pl.when(j == 0)
    def _():
        carry_ref[...] = jnp.zeros_like(carry_ref)

    x = h_ref[...]
    rows = x.shape[0]
    xhat = x * lax.rsqrt(jnp.mean(x * x, axis=-1, keepdims=True) + EPS)
    xn = xhat * g_ref[...]
    xb = xn.astype(_bf16)
    qn = _heads_rmsnorm(_dot((xhat * gmix_ref[...]).astype(_bf16), wq_ref[...]), seg_ref[...], gq_ref[...])
    qn = (qn * (F_HD ** -0.5 * LOG2E)).astype(_bf16)
    qrow = lax.broadcasted_iota(jnp.int32, (2 * LANES, 1), 0) % LANES
    ones = jnp.where((qrow >= F_HD) & (qrow < F_HD + 3), 1.0, 0.0)
    spread_t = place_ref[0, :LANES, :].T
    for p in range(F_HEADS // 2):
        spread = _dot_nt(spread_t, qn[:, p * LANES:(p + 1) * LANES])
        q_ref[2 * p * LANES:2 * (p + 1) * LANES, :] = (spread + ones).astype(_bf16)
    vt = _dot_nt(wvt_ref[...], xb).astype(_bf16)
    for h in range(F_HEADS):
        vt_ref[h, :F_HD, :] = vt[h * F_HD:(h + 1) * F_HD, :]
        vt_ref[h, F_HD:, :] = jnp.ones((V_ROWS - F_HD, rows), _bf16)
    kn = _heads_rmsnorm(_dot(xb, wk_ref[...]), seg_ref[...], gk_ref[...]).astype(_bf16)
    lf = _log_sigmoid(_dot_x3(xn, wf_hi_ref[...], wf_lo_ref[...]) + bf_ref[...])
    row_in_batch = j * rows + lax.broadcasted_iota(jnp.int32, lf.shape, 0)
    lf = jnp.where(row_in_batch < PAD_FRONT, 0.0, lf)
    r = lax.broadcasted_iota(jnp.int32, (LANES, LANES), 0)
    c = lax.broadcasted_iota(jnp.int32, (LANES, LANES), 1)
    tril = jnp.where(c <= r, 1.0, 0.0).astype(_bf16)
    carry = carry_ref[0:1, :]
    terms = []
    for sb in range(rows // LANES):
        cs = _dot_ones3(tril, lf[sb * LANES:(sb + 1) * LANES, :]) + carry
        carry = cs[LANES - 1:LANES, :]
        f = (-LOG2E) * cs
        f1 = f.astype(_bf16).astype(_f32)
        f2 = (f - f1).astype(_bf16).astype(_f32)
        f3 = (f - f1) - f2
        terms.append(jnp.where(c < F_HEADS, f1, jnp.where(c < 2 * F_HEADS, pltpu.roll(f2, F_HEADS, 1),
                                                          jnp.where(c < 3 * F_HEADS, pltpu.roll(f3, 2 * F_HEADS, 1), 0.0))))
    carry_ref[...] = jnp.broadcast_to(carry, carry_ref.shape)
    fterms = jnp.concatenate(terms, axis=0).astype(_bf16)
    for p in range(F_HEADS // 2):
        lhs = jnp.concatenate([kn[:, p * LANES:(p + 1) * LANES], fterms], axis=1)
        k_ref[:, 2 * p * LANES:2 * (p + 1) * LANES] = _dot(lhs, place_ref[p]).astype(_bf16)


def _shared_kv_and_q(h, gain, w_kv, b_f, g_k, gain_mix, w_q, g_q, batch, batch_rows, row0):
    tm = ROW_TILE
    spb = batch_rows // tm
    q0 = row0 // tm
    T = batch * batch_rows
    assert row0 % tm == 0
    wk = w_kv[:, :D_MODEL].astype(_bf16)
    wvt = w_kv[:, D_MODEL:2 * D_MODEL].T.astype(_bf16)
    wf_hi, wf_lo = _split2(jnp.zeros((D_MODEL, LANES), _f32).at[:, :F_HEADS].set(w_kv[:, 2 * D_MODEL:]))
    bf = jnp.zeros((1, LANES), _f32).at[0, :F_HEADS].set(b_f)
    row = lambda n: pl.BlockSpec((tm, n), lambda b, j: (b * spb + j, 0))
    wide = F_HEADS * LANES
    ka, vt, qa = pl.pallas_call(
        _kvq_kernel,
        grid=(batch, spb),
        in_specs=[row(D_MODEL), _const_spec((1, D_MODEL)), _const_spec((D_MODEL, D_MODEL)),
                  _const_spec((D_MODEL, D_MODEL)), _const_spec((D_MODEL, LANES)), _const_spec((D_MODEL, LANES)),
                  _const_spec((1, LANES)), _const_spec((1, D_MODEL)), _const_spec((SEG_W, SEG_W)),
                  _const_spec((F_HEADS // 2, 2 * LANES, 2 * LANES)), _const_spec((1, D_MODEL)),
                  _const_spec((D_MODEL, D_MODEL)), _const_spec((1, D_MODEL))],
        out_specs=[row(wide), pl.BlockSpec((None, F_HEADS, V_ROWS, tm), lambda b, j: (b, 0, 0, j)),
                   pl.BlockSpec((None, wide, tm), lambda b, j: (b, 0, jnp.maximum(j - q0, 0)))],
        out_shape=[jax.ShapeDtypeStruct((T, wide), _bf16),
                   jax.ShapeDtypeStruct((batch, F_HEADS, V_ROWS, batch_rows), _bf16),
                   jax.ShapeDtypeStruct((batch, wide, batch_rows - row0), _bf16)],
        scratch_shapes=[pltpu.VMEM((8, LANES), _f32)],
        compiler_params=_params("parallel", "arbitrary"),
        name="shared_kv_q",
    )(h, gain.reshape(1, D_MODEL), wk, wvt, wf_hi, wf_lo, bf, jnp.tile(g_k, F_HEADS).reshape(1, D_MODEL),
      _segment_ones(), _head_place_matrices(), gain_mix.reshape(1, D_MODEL), w_q.astype(_bf16),
      jnp.tile(g_q, F_HEADS).reshape(1, D_MODEL))
    return ka.reshape(batch, batch_rows, wide), vt, qa


def _attn_kernel(q_ref, k_ref, vt_ref, out_ref, sa_ref, sb_ref, xa_ref, xb_ref, m_ref, acc_ref):
    iq = pl.program_id(2)
    tq, tk = ATT_TQ, ATT_TK
    tile = 2 * LANES
    n_qt = tq // tile
    m_ref[...] = jnp.full(m_ref.shape, NEG, _f32)
    acc_ref[...] = jnp.zeros_like(acc_ref)

    def scores(ks, rows, hh):
        return _dot(k_ref[pl.ds(ks, rows), hh * LANES:(hh + 1) * LANES], q_ref[hh * LANES:(hh + 1) * LANES, :])

    def issue_head(jk, s_ref, x_ref, hh):
        st = scores(pl.multiple_of(jk * tk, tk), tk, hh)
        s_ref[hh] = st
        x_ref[hh] = jnp.max(st, axis=0, keepdims=True)

    def issue(jk, s_ref, x_ref):
        for hh in range(2):
            issue_head(jk, s_ref, x_ref, hh)

    def consume(st, st_max, ks, rows, hh, q_tiles=None, diag_tile=None, lane0=0):
        for qh in (range(n_qt) if q_tiles is None else q_tiles):
            ql = slice(qh * tile, (qh + 1) * tile)
            sl = slice(qh * tile - lane0, (qh + 1) * tile - lane0)
            m_prev = m_ref[hh, :, ql]
            m_new = jnp.maximum(m_prev, st_max[:, sl])
            alpha = jnp.exp2(m_prev - m_new)
            pv = None
            for kh in range(rows // tile) if rows >= tile else range(1):
                if diag_tile is not None and kh > qh - diag_tile:
                    continue
                kr = min(tile, rows)
                pt = jnp.exp2((st[kh * kr:(kh + 1) * kr, sl] - m_new).astype(_bf16))
                d = _dot(vt_ref[hh, :, pl.ds(ks + kh * kr, kr)], pt)
                pv = d if pv is None else pv + d
            acc_ref[hh, :, ql] = alpha * acc_ref[hh, :, ql] + pv
            m_ref[hh, :, ql] = m_new

    def consume_diag(jk, s_ref, hh, q0):
        ks = pl.multiple_of(jk * tk, tk)
        k_idx = lax.broadcasted_iota(jnp.int32, (tk, tk), 0)
        q_idx = lax.broadcasted_iota(jnp.int32, (tk, tk), 1)
        st = jnp.where(k_idx <= q_idx, s_ref[hh, :, q0:q0 + tk], NEG)
        consume(st, jnp.max(st, axis=0, keepdims=True), ks, tk, hh,
                q_tiles=range(q0 // tile, (q0 + tk) // tile), diag_tile=q0 // tile, lane0=q0)

    def step(j_issue, si_ref, xi_ref, j_cons, sc_ref, xc_ref):
        ks = pl.multiple_of(j_cons * tk, tk)
        for hh in range(2):
            issue_head(j_issue, si_ref, xi_ref, hh)
            consume(sc_ref[hh], xc_ref[hh], ks, tk, hh)

    st_meta = [scores(PAD_FRONT, N_META, hh) for hh in range(2)]
    issue(1, sa_ref, xa_ref)
    for hh in range(2):
        consume(st_meta[hh], jnp.max(st_meta[hh], axis=0, keepdims=True), PAD_FRONT, N_META, hh)

    def body(t, carry):
        j = 2 * t + 1
        step(j + 1, sb_ref, xb_ref, j, sa_ref, xa_ref)
        step(j + 2, sa_ref, xa_ref, j + 1, sb_ref, xb_ref)
        return carry

    lax.fori_loop(0, iq, body, 0)
    j_diag = 2 * iq + 1
    ks = pl.multiple_of(j_diag * tk, tk)
    ks_next = pl.multiple_of((j_diag + 1) * tk, tk)
    for hh in range(2):
        sb_ref[hh, :, tk:] = _dot(k_ref[pl.ds(ks_next, tk), hh * LANES:(hh + 1) * LANES],
                                  q_ref[hh * LANES:(hh + 1) * LANES, tk:])
        consume_diag(j_diag, sa_ref, hh, 0)
        consume(sa_ref[hh], xa_ref[hh], ks, tk, hh, q_tiles=range(tk // tile, n_qt))
    for hh in range(2):
        consume_diag(j_diag + 1, sb_ref, hh, tk)

    ot = jnp.concatenate([acc_ref[hh, :F_HD, :] / acc_ref[hh, F_HD:F_HD + 1, :] for hh in range(2)], axis=0)
    out_ref[...] = ot.T.astype(out_ref.dtype)


def _attention(q, k, vt, batch, seq, batch_rows):
    assert ATT_TQ == 2 * ATT_TK and REAL0 == ATT_TK
    pairs = F_HEADS // 2
    nq = seq // ATT_TQ
    return pl.pallas_call(
        _attn_kernel,
        grid=(batch, pairs, nq),
        in_specs=[pl.BlockSpec((None, 2 * LANES, ATT_TQ), lambda b, p, i: (b, p, i)),
                  pl.BlockSpec((None, batch_rows, 2 * LANES), lambda b, p, i: (b, 0, p)),
                  pl.BlockSpec((None, 2, V_ROWS, batch_rows), lambda b, p, i: (b, p, 0, 0))],
        out_specs=pl.BlockSpec((None, ATT_TQ, LANES), lambda b, p, i: (b, i, p)),
        out_shape=jax.ShapeDtypeStruct((batch, seq, D_MODEL), _bf16),
        scratch_shapes=[pltpu.VMEM((2, ATT_TK, ATT_TQ), _f32), pltpu.VMEM((2, ATT_TK, ATT_TQ), _f32),
                        pltpu.VMEM((2, 1, ATT_TQ), _f32), pltpu.VMEM((2, 1, ATT_TQ), _f32),
                        pltpu.VMEM((2, 1, ATT_TQ), _f32), pltpu.VMEM((2, V_ROWS, ATT_TQ), _f32)],
        compiler_params=_params("parallel", "parallel", "arbitrary"),
        name="fox_attention",
    )(q, k, vt)


def kernel(x, meta_tokens, norm_mix, norm_ffn, m_w_in, m_b_gate, m_g_out, m_w_out, kv_norm, kv_w, kv_b_f, k_norm,
           f_w_q, f_q_norm, f_w_o, moe_w_group, moe_b_group, moe_w_router, moe_b_router, moe_w_gate, moe_w_up,
           moe_w_down):
    B, S, D = x.shape
    depth = norm_mix.shape[0]
    n_a = m_w_in.shape[0]
    LP = REAL0 + S
    assert D == D_MODEL and S % ATT_TQ == 0 and LP % ROW_TILE == 0 and meta_tokens.shape[0] == N_META
    assert (B * LP) % MOE_ROW_TILE == 0 and (B * S) % MOE_ROW_TILE == 0
    assert depth > n_a and REAL0 % M_CHUNK == 0 and MLSTM_ROWS % M_CHUNK == 0
    assert n_a == 1
    head = jnp.concatenate([jnp.zeros((PAD_FRONT, D), x.dtype), meta_tokens.astype(x.dtype)], axis=0)
    k_sh = vt_sh = None
    rows_per_batch, row0, seq = LP, 0, LP
    for l in range(depth):
        if l < n_a:
            h, q, kt, v, o, u, bcol = _mlstm_in_proj(x, head, norm_mix[l], m_w_in[l], m_b_gate[l], LP)
            y = _mlstm_core(q, kt, v, o, u.T, bcol, m_g_out[l].reshape(-1), B, LP)
            w_out = m_w_out[l]
        else:
            j = l - n_a
            assert j == 0
            row0, seq = REAL0, S
            k_sh, vt_sh, q = _shared_kv_and_q(h, kv_norm, kv_w, kv_b_f, k_norm, norm_mix[l], f_w_q[j], f_q_norm[j],
                                              B, LP, row0)
            y = _attention(q, k_sh, vt_sh, B, S, LP).reshape(B * S, D)
            w_out = f_w_o[j]
        h = _mix_out_moe(y, w_out, h, rows_per_batch, row0, seq, norm_ffn[l], moe_w_group[l], moe_b_group[l],
                         moe_w_router[l], moe_b_router[l], moe_w_gate[l], moe_w_up[l], moe_w_down[l])
        rows_per_batch, row0 = seq, 0
    return h.reshape(B, S, D)
```

```python
import functools

import jax
import numpy as np
import jax.numpy as jnp
from jax import lax
from jax.experimental import pallas as pl
from jax.experimental.pallas import tpu as pltpu

D_MODEL = 1024
N_META = 16
M_HEADS = 4
M_DV = D_MODEL // M_HEADS
M_DK = M_DV // 2
M_CHUNK = 256
GATE_CAP = 15.0
F_HEADS = 16
F_HD = D_MODEL // F_HEADS
N_GROUPS = 4
EXP_PER_GROUP = 4
N_EXPERTS = N_GROUPS * EXP_PER_GROUP
D_EXPERT = D_MODEL // 4
EPS = 1e-6
NEG = -1e30

LANES = 128
ROW_TILE = 512
IN_ROW_TILE = 1024
MOE_ROW_TILE = 1024
REAL0 = 512
PAD_FRONT = REAL0 - N_META
MLSTM_ROWS = 512
V_ROWS = F_HD + 16
ATT_TQ = 1024
ATT_TK = 512
LOG2E = 1.4426950408889634
VMEM_LIMIT = 56 * 1024 * 1024

_f32 = jnp.float32
_bf16 = jnp.bfloat16


def _dot(a, b):
    return jnp.dot(a, b, preferred_element_type=_f32)


def _split2(w):
    hi = w.astype(_bf16)
    return hi, (w - hi.astype(_f32)).astype(_bf16)


def _dot_x3(x, w_hi, w_lo):
    x_hi, x_lo = _split2(x)
    return _dot(x_hi, w_hi) + _dot(x_lo, w_hi) + _dot(x_hi, w_lo)


def _dot_ones3(ones_mat, x):
    x1 = x.astype(_bf16)
    r1 = x - x1.astype(_f32)
    x2 = r1.astype(_bf16)
    x3 = (r1 - x2.astype(_f32)).astype(_bf16)
    return _dot(ones_mat, x1) + _dot(ones_mat, x2) + _dot(ones_mat, x3)


def _dot_nt(a, b):
    return lax.dot_general(a, b, (((1,), (1,)), ((), ())), preferred_element_type=_f32)


def _rmsnorm(x, g):
    return x * lax.rsqrt(jnp.mean(x * x, axis=-1, keepdims=True) + EPS) * g


def _log_sigmoid(x):
    return jnp.minimum(x, 0.0) - jnp.log1p(jnp.exp(-jnp.abs(x)))


def _params(*sem):
    return pltpu.CompilerParams(dimension_semantics=sem, vmem_limit_bytes=VMEM_LIMIT)


def _const_spec(shape):
    return pl.BlockSpec(shape, lambda *_: (0,) * len(shape))


def _mlstm_in_kernel(n_batch, batch_rows, x_lo_ref, x_hi_ref, head_ref, g_ref, wq_ref, wkt_ref, wv_ref, wo_ref,
                     wg_hi_ref, wg_lo_ref, bgate_ref, h_ref, q_ref, kt_ref, v_ref, o_ref, u_ref, b_ref):
    i = pl.program_id(0)
    half = x_lo_ref.shape[0]
    halves_per_batch = batch_rows // half
    parts = []
    for k, x_ref in enumerate((x_lo_ref, x_hi_ref)):
        is_head = lax.rem(2 * i + k, halves_per_batch) == 0
        parts.append(jnp.where(is_head, head_ref[...], x_ref[...]))
    x = jnp.concatenate(parts, axis=0)
    h_ref[...] = x
    xn = _rmsnorm(x, g_ref[...])
    xb = xn.astype(_bf16)
    q_ref[...] = (_dot(xb, wq_ref[...]) * (M_DK ** -0.5)).astype(_bf16)
    kt_ref[...] = _dot_nt(wkt_ref[...], xb).astype(_bf16)
    v_ref[...] = _dot(xb, wv_ref[...]).astype(_bf16)
    o_ref[...] = jax.nn.sigmoid(_dot(xb, wo_ref[...])).astype(_bf16)
    gates = _dot_x3(xn, wg_hi_ref[...], wg_lo_ref[...]) + bgate_ref[...]
    gates = GATE_CAP * jnp.tanh(gates * (1.0 / GATE_CAP))
    rows = x.shape[0]
    row = i * rows + lax.broadcasted_iota(jnp.int32, (rows, LANES), 0)
    is_pad = row < PAD_FRONT
    for bb in range(1, n_batch):
        is_pad = is_pad | ((row >= bb * batch_rows) & (row < bb * batch_rows + PAD_FRONT))
    li = jnp.where(is_pad, NEG, gates)
    lf = jnp.where(is_pad, 0.0, _log_sigmoid(gates))
    r = lax.broadcasted_iota(jnp.int32, (M_CHUNK, M_CHUNK), 0)
    c = lax.broadcasted_iota(jnp.int32, (M_CHUNK, M_CHUNK), 1)
    tril = jnp.where(c <= r, 1.0, 0.0).astype(_bf16)
    b = jnp.concatenate([_dot_ones3(tril, lf[k * M_CHUNK:(k + 1) * M_CHUNK, :]) for k in range(rows // M_CHUNK)], axis=0)
    b = pltpu.roll(b, LANES - M_HEADS, 1)
    u_ref[...] = (li - b)[:, :8]
    b_ref[...] = b[:, :8]


def _mlstm_in_proj(x, head, gain, w_in, b_gate, batch_rows):
    B, S, _ = x.shape
    T = B * batch_rows
    tm = IN_ROW_TILE
    half = tm // 2
    assert half == REAL0 and T % tm == 0 and S % half == 0 and batch_rows == REAL0 + S
    qk_w = M_HEADS * M_DK
    wq = w_in[:, :qk_w].astype(_bf16)
    wkt = w_in[:, qk_w:2 * qk_w].T.astype(_bf16)
    wv = w_in[:, 2 * qk_w:2 * qk_w + D_MODEL].astype(_bf16)
    wo = w_in[:, 2 * qk_w + D_MODEL:2 * qk_w + 2 * D_MODEL].astype(_bf16)
    n_gate = 2 * M_HEADS
    wg_hi, wg_lo = _split2(jnp.zeros((D_MODEL, LANES), _f32).at[:, :n_gate].set(w_in[:, 2 * qk_w + 2 * D_MODEL:]))
    bgate = jnp.zeros((1, LANES), _f32).at[0, :n_gate].set(b_gate)
    row = lambda n: pl.BlockSpec((tm, n), lambda i: (i, 0))

    def x_half(k):
        def index(i):
            r = 2 * i + k
            b = r // (batch_rows // half)
            j = r - b * (batch_rows // half)
            return (b * (S // half) + jnp.maximum(j - 1, 0), 0)
        return pl.BlockSpec((half, D_MODEL), index)

    x2 = x.reshape(B * S, D_MODEL)
    return pl.pallas_call(
        functools.partial(_mlstm_in_kernel, B, batch_rows),
        grid=(T // tm,),
        in_specs=[x_half(0), x_half(1), _const_spec((half, D_MODEL)), _const_spec((1, D_MODEL)),
                  _const_spec((D_MODEL, qk_w)), _const_spec((qk_w, D_MODEL)), _const_spec((D_MODEL, D_MODEL)),
                  _const_spec((D_MODEL, D_MODEL)), _const_spec((D_MODEL, LANES)), _const_spec((D_MODEL, LANES)),
                  _const_spec((1, LANES))],
        out_specs=[row(D_MODEL), row(qk_w), pl.BlockSpec((qk_w, tm), lambda i: (0, i)), row(D_MODEL), row(D_MODEL),
                   row(8), row(8)],
        out_shape=[jax.ShapeDtypeStruct((T, D_MODEL), _f32),
                   jax.ShapeDtypeStruct((T, qk_w), _bf16), jax.ShapeDtypeStruct((qk_w, T), _bf16),
                   jax.ShapeDtypeStruct((T, D_MODEL), _bf16), jax.ShapeDtypeStruct((T, D_MODEL), _bf16),
                   jax.ShapeDtypeStruct((T, 8), _f32), jax.ShapeDtypeStruct((T, 8), _f32)],
        compiler_params=_params("parallel"),
        name="mlstm_in_proj",
    )(x2, x2, head, gain.reshape(1, D_MODEL), wq, wkt, wv, wo, wg_hi, wg_lo, bgate)


def _mlstm_core_kernel(q_ref, kt_ref, v_ref, o_ref, ut_ref, b_ref, gout_ref, out_ref, state_ref, m_ref):
    C = M_CHUNK
    n_chunks = q_ref.shape[0] // C

    @pl.when(pl.program_id(1) == 0)
    def _():
        state_ref[...] = jnp.zeros_like(state_ref)
        m_ref[...] = jnp.zeros_like(m_ref)

    r = lax.broadcasted_iota(jnp.int32, (C, C), 0)
    c = lax.broadcasted_iota(jnp.int32, (C, C), 1)
    causal = c <= r
    ones_col = jnp.where(lax.broadcasted_iota(jnp.int32, (C, LANES), 1) == 0, 1.0, 0.0).astype(_bf16)

    H = range(M_HEADS)
    S = [state_ref[h] for h in H]
    m = [m_ref[h][0:1, 0:1] for h in H]
    for ci in range(n_chunks):
        rows = slice(ci * C, (ci + 1) * C)
        qc = [q_ref[rows, h * M_DK:(h + 1) * M_DK] for h in H]
        ktc = [kt_ref[h * M_DK:(h + 1) * M_DK, rows] for h in H]
        v_aug = [jnp.concatenate([v_ref[rows, h * M_DV:(h + 1) * M_DV], ones_col], axis=1) for h in H]
        u_row = [ut_ref[h:h + 1, rows] for h in H]
        b_col = [b_ref[rows, h:h + 1] for h in H]
        qk = [_dot(qc[h], ktc[h]) for h in H]
        qs = [_dot(qc[h], S[h].astype(_bf16)) for h in H]
        cu_col = [jnp.max(jnp.where(causal, u_row[h], NEG), axis=1, keepdims=True) for h in H]
        M = [jnp.maximum(cu_col[h], m[h]) for h in H]
        s_mat = [(qk[h] * jnp.where(causal, jnp.exp(u_row[h] - M[h]), 0.0)).astype(_bf16) for h in H]
        tot = [_dot(s_mat[h], v_aug[h]) + jnp.exp(m[h] - M[h]) * qs[h] for h in H]
        for h in H:
            g_tot = b_col[h][C - 1:C, :]
            cu_last = cu_col[h][C - 1:C, :]
            m_loc = g_tot + cu_last
            m_new = jnp.maximum(g_tot + m[h], m_loc)
            ktw = (ktc[h].astype(_f32) * jnp.exp(u_row[h] - cu_last)).astype(_bf16)
            kv = _dot(ktw, v_aug[h])
            S[h] = jnp.exp(g_tot + m[h] - m_new) * S[h] + jnp.exp(m_loc - m_new) * kv
            m[h] = m_new
        for h in H:
            num = tot[h][:, :M_DV]
            den = tot[h][:, M_DV:M_DV + 1]
            hout = num / jnp.maximum(jnp.abs(den), jnp.exp(-(b_col[h] + M[h])))
            y = hout * lax.rsqrt(jnp.mean(hout * hout, axis=-1, keepdims=True) + EPS)
            y = y * gout_ref[:, h * M_DV:(h + 1) * M_DV] * o_ref[rows, h * M_DV:(h + 1) * M_DV].astype(_f32)
            out_ref[rows, h * M_DV:(h + 1) * M_DV] = y.astype(out_ref.dtype)
    for h in H:
        state_ref[h] = S[h]
        m_ref[h] = jnp.broadcast_to(m[h], m_ref.shape[1:])


def _mlstm_core(q, kt, v, o, ut, bcol, g_out, batch, batch_rows):
    R = MLSTM_ROWS
    spb = batch_rows // R
    row = lambda n: pl.BlockSpec((R, n), lambda b, s: (b * spb + s, 0))
    col = lambda n: pl.BlockSpec((n, R), lambda b, s: (0, b * spb + s))
    qk_w = M_HEADS * M_DK
    return pl.pallas_call(
        _mlstm_core_kernel,
        grid=(batch, spb),
        in_specs=[row(qk_w), col(qk_w), row(D_MODEL), row(D_MODEL), col(8), row(8), _const_spec((1, D_MODEL))],
        out_specs=row(D_MODEL),
        out_shape=jax.ShapeDtypeStruct((batch * batch_rows, D_MODEL), _bf16),
        scratch_shapes=[pltpu.VMEM((M_HEADS, M_DK, M_DV + LANES), _f32), pltpu.VMEM((M_HEADS, 8, LANES), _f32)],
        compiler_params=_params("parallel", "arbitrary"),
        name="mlstm_core",
    )(q, kt, v, o, ut, bcol, g_out.reshape(1, D_MODEL))


MOE_CHUNK = 288
MOE_MAX_CHUNKS = (MOE_ROW_TILE - 1) // MOE_CHUNK + N_GROUPS
MOE_SORTED_ROWS = 2048
MOE_XG_W = D_MODEL + 2 * LANES


def _route(logits):
    lane = lax.broadcasted_iota(jnp.int32, logits.shape, 1)
    is_g = lane < N_GROUPS
    gl = jnp.where(is_g, logits, -jnp.inf)
    gmax = jnp.max(gl, axis=1, keepdims=True)
    top_g = jnp.min(jnp.where(gl == gmax, lane, LANES), axis=1, keepdims=True)
    p_g = 1.0 / jnp.sum(jnp.where(is_g, jnp.exp(gl - gmax), 0.0), axis=1, keepdims=True)
    e_lane = lane - N_GROUPS
    in_group = (e_lane >= top_g * EXP_PER_GROUP) & (e_lane < (top_g + 1) * EXP_PER_GROUP)
    el = jnp.where(in_group, logits, -jnp.inf)
    v1 = jnp.max(el, axis=1, keepdims=True)
    i1 = jnp.min(jnp.where(el == v1, lane, LANES), axis=1, keepdims=True)
    el2 = jnp.where(lane == i1, -jnp.inf, el)
    v2 = jnp.max(el2, axis=1, keepdims=True)
    i2 = jnp.min(jnp.where(el2 == v2, lane, LANES), axis=1, keepdims=True)
    e2 = jnp.exp(v2 - v1)
    p1 = p_g / (1.0 + e2)
    p2 = p_g * e2 / (1.0 + e2)
    return jnp.where(lane == i1, p1, jnp.where(lane == i2, p2, 0.0)), top_g


def _router_kernel(y_ref, wo_ref, res_lo_ref, res_hi_ref, g_ref, wr_hi_ref, wr_lo_ref, br_ref, tril_ref,
                   h_ref, xg_ref, pos_ref, meta_ref):
    hr = res_lo_ref.shape[0]
    halves = (slice(0, hr), slice(hr, 2 * hr))
    x = [r[...] + _dot(y_ref[sl, :], wo_ref[...]) for r, sl in zip((res_lo_ref, res_hi_ref), halves)]
    xn = [_rmsnorm(xk, g_ref[...]) for xk in x]
    logits = [_dot_x3(xk, wr_hi_ref[...], wr_lo_ref[...]) + br_ref[...] for xk in xn]
    for xk, xnk, sl in zip(x, xn, halves):
        h_ref[sl, :] = xk
        xg_ref[sl, :D_MODEL] = xnk.astype(_bf16)
    routed = [_route(lg) for lg in logits]
    lane = lax.broadcasted_iota(jnp.int32, (hr, LANES), 1)
    in_g = [lane == top_g for _, top_g in routed]
    onehot = [jnp.where(m, 1.0, 0.0) for m in in_g]
    incl = [_dot(tril_ref[...], oh.astype(_bf16)) for oh in onehot]
    incl[1] = incl[1] + incl[0][hr - 1:hr, :]
    for (gates, _), sl in zip(routed, halves):
        g_hi, g_lo = _split2(gates)
        xg_ref[sl, D_MODEL:D_MODEL + LANES] = g_hi
        xg_ref[sl, D_MODEL + LANES:] = g_lo
    counts = incl[1][hr - 1:hr, :]
    nchunk = jnp.floor((counts + (MOE_CHUNK - 1)) * (1.0 / MOE_CHUNK) + 0.5 / MOE_CHUNK)
    first = pltpu.roll(nchunk, 1, 1) + pltpu.roll(nchunk, 2, 1) + pltpu.roll(nchunk, 3, 1)
    for m, oh, inc, sl in zip(in_g, onehot, incl, halves):
        base = MOE_CHUNK * first + (inc - oh)
        pos_ref[sl, :] = jnp.sum(jnp.where(m, base, 0.0), axis=1, keepdims=True)
    end = first + nchunk
    lane1 = lax.broadcasted_iota(jnp.int32, (1, LANES), 1).astype(_f32)
    grp_of = jnp.zeros((1, LANES), _f32)
    for g in range(N_GROUPS - 1):
        grp_of = grp_of + jnp.where(lane1 >= end[:, g:g + 1], 1.0, 0.0)
    meta = jnp.where(lane1 == MOE_MAX_CHUNKS, end[:, N_GROUPS - 1:N_GROUPS], grp_of)
    meta_ref[...] = jnp.broadcast_to(meta, meta_ref.shape).astype(jnp.int32)


def _experts_kernel(grp_ref, cnt_ref, xg_ref, posr_ref, posc_ref, h_ref, wg_ref, wu_ref, wd_ref, out_ref, ys_ref):
    i = pl.program_id(0)
    c = pl.program_id(1)
    tm = xg_ref.shape[0]
    chunk_rows = pl.ds(pl.multiple_of(jnp.minimum(c, MOE_MAX_CHUNKS - 1) * MOE_CHUNK, 16), MOE_CHUNK)

    @pl.when(c == 0)
    def _():
        tail = MOE_MAX_CHUNKS * MOE_CHUNK
        ys_ref[tail:, :] = jnp.zeros((MOE_SORTED_ROWS - tail, D_MODEL), _bf16)

    @pl.when(c < cnt_ref[i])
    def _():
        grp = grp_ref[i * MOE_MAX_CHUNKS + c]
        first_row = (c * MOE_CHUNK).astype(_f32)
        r_id = lax.broadcasted_iota(jnp.int32, (MOE_CHUNK, tm), 0).astype(_f32) + first_row
        sel = jnp.where(posr_ref[...] == r_id, 1.0, 0.0).astype(_bf16)
        xg = _dot(sel, xg_ref[...])
        xs = xg[:, :D_MODEL].astype(_bf16)
        gs = xg[:, D_MODEL:D_MODEL + LANES] + xg[:, D_MODEL + LANES:]
        lane = lax.broadcasted_iota(jnp.int32, gs.shape, 1)
        acts = []
        for e in range(EXP_PER_GROUP):
            gate_e = jnp.sum(jnp.where(lane == N_GROUPS + grp * EXP_PER_GROUP + e, gs, 0.0), axis=1, keepdims=True)
            hg = _dot(xs, wg_ref[e])
            hu = _dot(xs, wu_ref[e])
            acts.append((hg * jax.nn.sigmoid(hg) * hu * gate_e).astype(_bf16))
        act_all = jnp.concatenate(acts, axis=1)
        w_down = wd_ref[...].reshape(EXP_PER_GROUP * D_EXPERT, D_MODEL)
        half = D_MODEL // 2
        for n in range(2):
            ys_ref[chunk_rows, n * half:(n + 1) * half] = _dot(act_all, w_down[:, n * half:(n + 1) * half]).astype(_bf16)

    @pl.when((c >= cnt_ref[i]) & (c < MOE_MAX_CHUNKS))
    def _():
        ys_ref[chunk_rows, :] = jnp.zeros((MOE_CHUNK, D_MODEL), _bf16)

    @pl.when(c == MOE_MAX_CHUNKS)
    def _():
        s_id = lax.broadcasted_iota(jnp.int32, (tm, MOE_SORTED_ROWS), 1).astype(_f32)
        sel_t = jnp.where(posc_ref[...] == s_id, 1.0, 0.0).astype(_bf16)
        out_ref[...] = h_ref[...] + _dot(sel_t, ys_ref[...])


def _mix_out_moe(y, w_out, res, res_batch_rows, res_row0, seq, gain, w_group, b_group, w_router, b_router,
                 w_gate, w_up, w_down):
    T = y.shape[0]
    tm = MOE_ROW_TILE
    half = tm // 2
    tiles = T // tm
    assert T % tm == 0 and seq % half == 0 and res_batch_rows % half == 0 and res_row0 % half == 0
    assert MOE_MAX_CHUNKS * MOE_CHUNK <= MOE_SORTED_ROWS and MOE_CHUNK % 16 == 0
    wr = jnp.zeros((D_MODEL, LANES), _f32)
    wr_hi, wr_lo = _split2(wr.at[:, :N_GROUPS].set(w_group).at[:, N_GROUPS:N_GROUPS + N_EXPERTS].set(w_router))
    br = jnp.zeros((1, LANES), _f32)
    br = br.at[0, :N_GROUPS].set(b_group).at[0, N_GROUPS:N_GROUPS + N_EXPERTS].set(b_router)
    tril = jnp.asarray(np.tril(np.ones((half, half), np.float32)), _bf16)
    row = lambda n: pl.BlockSpec((tm, n), lambda i: (i, 0))

    def res_half(k):
        def index(i):
            r = 2 * i + k
            b = r // (seq // half)
            return (b * (res_batch_rows // half) + res_row0 // half + r - b * (seq // half), 0)
        return pl.BlockSpec((half, D_MODEL), index)

    h, xg, pos, meta = pl.pallas_call(
        _router_kernel,
        grid=(tiles,),
        in_specs=[row(D_MODEL), _const_spec((D_MODEL, D_MODEL)), res_half(0), res_half(1), _const_spec((1, D_MODEL)),
                  _const_spec((D_MODEL, LANES)), _const_spec((D_MODEL, LANES)), _const_spec((1, LANES)),
                  _const_spec((half, half))],
        out_specs=[row(D_MODEL), row(MOE_XG_W), row(1), pl.BlockSpec((8, LANES), lambda i: (i, 0))],
        out_shape=[jax.ShapeDtypeStruct((T, D_MODEL), _f32), jax.ShapeDtypeStruct((T, MOE_XG_W), _bf16),
                   jax.ShapeDtypeStruct((T, 1), _f32), jax.ShapeDtypeStruct((tiles * 8, LANES), jnp.int32)],
        compiler_params=_params("parallel"),
        name="mix_out_router",
    )(y, w_out.astype(_bf16), res, res, gain.reshape(1, D_MODEL), wr_hi, wr_lo, br, tril)
    meta = meta.reshape(tiles, 8, LANES)[:, 0]
    chunk_group = meta[:, :MOE_MAX_CHUNKS].reshape(-1)
    n_chunks = meta[:, MOE_MAX_CHUNKS]

    def wspec(shape):
        def index(i, c, grp, cnt):
            return (grp[i * MOE_MAX_CHUNKS + jnp.minimum(c, cnt[i] - 1)], 0, 0)
        return pl.BlockSpec(shape, index)

    tile = lambda n: pl.BlockSpec((tm, n), lambda i, c, grp, cnt: (i, 0))
    return pl.pallas_call(
        _experts_kernel,
        grid_spec=pltpu.PrefetchScalarGridSpec(
            num_scalar_prefetch=2,
            grid=(tiles, MOE_MAX_CHUNKS + 1),
            in_specs=[tile(MOE_XG_W), pl.BlockSpec((None, 1, tm), lambda i, c, grp, cnt: (i, 0, 0)),
                      tile(1), tile(D_MODEL), wspec((EXP_PER_GROUP, D_MODEL, D_EXPERT)),
                      wspec((EXP_PER_GROUP, D_MODEL, D_EXPERT)), wspec((EXP_PER_GROUP, D_EXPERT, D_MODEL))],
            out_specs=tile(D_MODEL),
            scratch_shapes=[pltpu.VMEM((MOE_SORTED_ROWS, D_MODEL), _bf16)]),
        out_shape=jax.ShapeDtypeStruct((T, D_MODEL), _f32),
        compiler_params=_params("parallel", "arbitrary"),
        name="moe_experts",
    )(chunk_group, n_chunks, xg, pos.reshape(tiles, 1, tm), pos, h, w_gate.astype(_bf16),
      w_up.astype(_bf16), w_down.astype(_bf16))


SEG_W = 256


def _segment_ones():
    return jnp.kron(jnp.eye(SEG_W // F_HD, dtype=_f32), jnp.ones((F_HD, F_HD), _f32)).astype(_bf16)


def _head_place_matrices():
    pairs = F_HEADS // 2
    place = np.zeros((pairs, 2 * LANES, 2 * LANES), np.float32)
    i = np.arange(F_HD)
    for p in range(pairs):
        place[p, i, i] = 1.0
        place[p, F_HD + i, LANES + i] = 1.0
        for hh in range(2):
            for t in range(3):
                place[p, LANES + F_HEADS * t + 2 * p + hh, hh * LANES + F_HD + t] = 1.0
    return jnp.asarray(place, _bf16)


def _heads_rmsnorm(x, seg, g):
    outs = []
    for c in range(x.shape[1] // SEG_W):
        xs = x[:, c * SEG_W:(c + 1) * SEG_W]
        sq_hi, sq_lo = _split2(xs * xs)
        ms = (_dot(sq_hi, seg) + _dot(sq_lo, seg)) * (1.0 / F_HD)
        outs.append(xs * lax.rsqrt(ms + EPS) * g[:, c * SEG_W:(c + 1) * SEG_W])
    return jnp.concatenate(outs, axis=1)


def _kvq_kernel(h_ref, g_ref, wk_ref, wvt_ref, wf_hi_ref, wf_lo_ref, bf_ref, gk_ref, seg_ref, place_ref,
                gmix_ref, wq_ref, gq_ref, k_ref, vt_ref, q_ref, carry_ref):
    j = pl.program_id(1)

    @pl.when(j == 0)
    def _():
        carry_ref[...] = jnp.zeros_like(carry_ref)

    x = h_ref[...]
    rows = x.shape[0]
    xhat = x * lax.rsqrt(jnp.mean(x * x, axis=-1, keepdims=True) + EPS)
    xn = xhat * g_ref[...]
    xb = xn.astype(_bf16)
    qn = _heads_rmsnorm(_dot((xhat * gmix_ref[...]).astype(_bf16), wq_ref[...]), seg_ref[...], gq_ref[...])
    qn = (qn * (F_HD ** -0.5 * LOG2E)).astype(_bf16)
    qrow = lax.broadcasted_iota(jnp.int32, (2 * LANES, 1), 0) % LANES
    ones = jnp.where((qrow >= F_HD) & (qrow < F_HD + 3), 1.0, 0.0)
    spread_t = place_ref[0, :LANES, :].T
    for p in range(F_HEADS // 2):
        spread = _dot_nt(spread_t, qn[:, p * LANES:(p + 1) * LANES])
        q_ref[2 * p * LANES:2 * (p + 1) * LANES, :] = (spread + ones).astype(_bf16)
    vt = _dot_nt(wvt_ref[...], xb).astype(_bf16)
    for h in range(F_HEADS):
        vt_ref[h, :F_HD, :] = vt[h * F_HD:(h + 1) * F_HD, :]
        vt_ref[h, F_HD:, :] = jnp.ones((V_ROWS - F_HD, rows), _bf16)
    kn = _heads_rmsnorm(_dot(xb, wk_ref[...]), seg_ref[...], gk_ref[...]).astype(_bf16)
    lf = _log_sigmoid(_dot_x3(xn, wf_hi_ref[...], wf_lo_ref[...]) + bf_ref[...])
    row_in_batch = j * rows + lax.broadcasted_iota(jnp.int32, lf.shape, 0)
    lf = jnp.where(row_in_batch < PAD_FRONT, 0.0, lf)
    r = lax.broadcasted_iota(jnp.int32, (LANES, LANES), 0)
    c = lax.broadcasted_iota(jnp.int32, (LANES, LANES), 1)
    tril = jnp.where(c <= r, 1.0, 0.0).astype(_bf16)
    carry = carry_ref[0:1, :]
    terms = []
    for sb in range(rows // LANES):
        cs = _dot_ones3(tril, lf[sb * LANES:(sb + 1) * LANES, :]) + carry
        carry = cs[LANES - 1:LANES, :]
        f = (-LOG2E) * cs
        f1 = f.astype(_bf16).astype(_f32)
        f2 = (f - f1).astype(_bf16).astype(_f32)
        f3 = (f - f1) - f2
        terms.append(jnp.where(c < F_HEADS, f1, jnp.where(c < 2 * F_HEADS, pltpu.roll(f2, F_HEADS, 1),
                                                          jnp.where(c < 3 * F_HEADS, pltpu.roll(f3, 2 * F_HEADS, 1), 0.0))))
    carry_ref[...] = jnp.broadcast_to(carry, carry_ref.shape)
    fterms = jnp.concatenate(terms, axis=0).astype(_bf16)
    for p in range(F_HEADS // 2):
        lhs = jnp.concatenate([kn[:, p * LANES:(p + 1) * LANES], fterms], axis=1)
        k_ref[:, 2 * p * LANES:2 * (p + 1) * LANES] = _dot(lhs, place_ref[p]).astype(_bf16)


def _shared_kv_and_q(h, gain, w_kv, b_f, g_k, gain_mix, w_q, g_q, batch, batch_rows, row0):
    tm = ROW_TILE
    spb = batch_rows // tm
    q0 = row0 // tm
    T = batch * batch_rows
    assert row0 % tm == 0
    wk = w_kv[:, :D_MODEL].astype(_bf16)
    wvt = w_kv[:, D_MODEL:2 * D_MODEL].T.astype(_bf16)
    wf_hi, wf_lo = _split2(jnp.zeros((D_MODEL, LANES), _f32).at[:, :F_HEADS].set(w_kv[:, 2 * D_MODEL:]))
    bf = jnp.zeros((1, LANES), _f32).at[0, :F_HEADS].set(b_f)
    row = lambda n: pl.BlockSpec((tm, n), lambda b, j: (b * spb + j, 0))
    wide = F_HEADS * LANES
    ka, vt, qa = pl.pallas_call(
        _kvq_kernel,
        grid=(batch, spb),
        in_specs=[row(D_MODEL), _const_spec((1, D_MODEL)), _const_spec((D_MODEL, D_MODEL)),
                  _const_spec((D_MODEL, D_MODEL)), _const_spec((D_MODEL, LANES)), _const_spec((D_MODEL, LANES)),
                  _const_spec((1, LANES)), _const_spec((1, D_MODEL)), _const_spec((SEG_W, SEG_W)),
                  _const_spec((F_HEADS // 2, 2 * LANES, 2 * LANES)), _const_spec((1, D_MODEL)),
                  _const_spec((D_MODEL, D_MODEL)), _const_spec((1, D_MODEL))],
        out_specs=[row(wide), pl.BlockSpec((None, F_HEADS, V_ROWS, tm), lambda b, j: (b, 0, 0, j)),
                   pl.BlockSpec((None, wide, tm), lambda b, j: (b, 0, jnp.maximum(j - q0, 0)))],
        out_shape=[jax.ShapeDtypeStruct((T, wide), _bf16),
                   jax.ShapeDtypeStruct((batch, F_HEADS, V_ROWS, batch_rows), _bf16),
                   jax.ShapeDtypeStruct((batch, wide, batch_rows - row0), _bf16)],
        scratch_shapes=[pltpu.VMEM((8, LANES), _f32)],
        compiler_params=_params("parallel", "arbitrary"),
        name="shared_kv_q",
    )(h, gain.reshape(1, D_MODEL), wk, wvt, wf_hi, wf_lo, bf, jnp.tile(g_k, F_HEADS).reshape(1, D_MODEL),
      _segment_ones(), _head_place_matrices(), gain_mix.reshape(1, D_MODEL), w_q.astype(_bf16),
      jnp.tile(g_q, F_HEADS).reshape(1, D_MODEL))
    return ka.reshape(batch, batch_rows, wide), vt, qa


def _attn_kernel(q_ref, k_ref, vt_ref, out_ref, sa_ref, sb_ref, xa_ref, xb_ref, m_ref, acc_ref):
    iq = pl.program_id(2)
    tq, tk = ATT_TQ, ATT_TK
    tile = 2 * LANES
    n_qt = tq // tile
    m_ref[...] = jnp.full(m_ref.shape, NEG, _f32)
    acc_ref[...] = jnp.zeros_like(acc_ref)

    def scores(ks, rows, hh):
        return _dot(k_ref[pl.ds(ks, rows), hh * LANES:(hh + 1) * LANES], q_ref[hh * LANES:(hh + 1) * LANES, :])

    def issue_head(jk, s_ref, x_ref, hh):
        st = scores(pl.multiple_of(jk * tk, tk), tk, hh)
        s_ref[hh] = st
        x_ref[hh] = jnp.max(st, axis=0, keepdims=True)

    def issue(jk, s_ref, x_ref):
        for hh in range(2):
            issue_head(jk, s_ref, x_ref, hh)

    def consume(st, st_max, ks, rows, hh, q_tiles=None, diag_tile=None, lane0=0):
        for qh in (range(n_qt) if q_tiles is None else q_tiles):
            ql = slice(qh * tile, (qh + 1) * tile)
            sl = slice(qh * tile - lane0, (qh + 1) * tile - lane0)
            m_prev = m_ref[hh, :, ql]
            m_new = jnp.maximum(m_prev, st_max[:, sl])
            alpha = jnp.exp2(m_prev - m_new)
            pv = None
            for kh in range(rows // tile) if rows >= tile else range(1):
                if diag_tile is not None and kh > qh - diag_tile:
                    continue
                kr = min(tile, rows)
                pt = jnp.exp2((st[kh * kr:(kh + 1) * kr, sl] - m_new).astype(_bf16))
                d = _dot(vt_ref[hh, :, pl.ds(ks + kh * kr, kr)], pt)
                pv = d if pv is None else pv + d
            acc_ref[hh, :, ql] = alpha * acc_ref[hh, :, ql] + pv
            m_ref[hh, :, ql] = m_new

    def consume_diag(jk, s_ref, hh, q0):
        ks = pl.multiple_of(jk * tk, tk)
        k_idx = lax.broadcasted_iota(jnp.int32, (tk, tk), 0)
        q_idx = lax.broadcasted_iota(jnp.int32, (tk, tk), 1)
        st = jnp.where(k_idx <= q_idx, s_ref[hh, :, q0:q0 + tk], NEG)
        consume(st, jnp.max(st, axis=0, keepdims=True), ks, tk, hh,
                q_tiles=range(q0 // tile, (q0 + tk) // tile), diag_tile=q0 // tile, lane0=q0)

    def step(j_issue, si_ref, xi_ref, j_cons, sc_ref, xc_ref):
        ks = pl.multiple_of(j_cons * tk, tk)
        for hh in range(2):
            issue_head(j_issue, si_ref, xi_ref, hh)
            consume(sc_ref[hh], xc_ref[hh], ks, tk, hh)

    st_meta = [scores(PAD_FRONT, N_META, hh) for hh in range(2)]
    issue(1, sa_ref, xa_ref)
    for hh in range(2):
        consume(st_meta[hh], jnp.max(st_meta[hh], axis=0, keepdims=True), PAD_FRONT, N_META, hh)

    def body(t, carry):
        j = 2 * t + 1
        step(j + 1, sb_ref, xb_ref, j, sa_ref, xa_ref)
        step(j + 2, sa_ref, xa_ref, j + 1, sb_ref, xb_ref)
        return carry

    lax.fori_loop(0, iq, body, 0)
    j_diag = 2 * iq + 1
    ks = pl.multiple_of(j_diag * tk, tk)
    ks_next = pl.multiple_of((j_diag + 1) * tk, tk)
    for hh in range(2):
        sb_ref[hh, :, tk:] = _dot(k_ref[pl.ds(ks_next, tk), hh * LANES:(hh + 1) * LANES],
                                  q_ref[hh * LANES:(hh + 1) * LANES, tk:])
        consume_diag(j_diag, sa_ref, hh, 0)
        consume(sa_ref[hh], xa_ref[hh], ks, tk, hh, q_tiles=range(tk // tile, n_qt))
    for hh in range(2):
        consume_diag(j_diag + 1, sb_ref, hh, tk)

    ot = jnp.concatenate([acc_ref[hh, :F_HD, :] / acc_ref[hh, F_HD:F_HD + 1, :] for hh in range(2)], axis=0)
    out_ref[...] = ot.T.astype(out_ref.dtype)


def _attention(q, k, vt, batch, seq, batch_rows):
    assert ATT_TQ == 2 * ATT_TK and REAL0 == ATT_TK
    pairs = F_HEADS // 2
    nq = seq // ATT_TQ
    return pl.pallas_call(
        _attn_kernel,
        grid=(batch, pairs, nq),
        in_specs=[pl.BlockSpec((None, 2 * LANES, ATT_TQ), lambda b, p, i: (b, p, i)),
                  pl.BlockSpec((None, batch_rows, 2 * LANES), lambda b, p, i: (b, 0, p)),
                  pl.BlockSpec((None, 2, V_ROWS, batch_rows), lambda b, p, i: (b, p, 0, 0))],
        out_specs=pl.BlockSpec((None, ATT_TQ, LANES), lambda b, p, i: (b, i, p)),
        out_shape=jax.ShapeDtypeStruct((batch, seq, D_MODEL), _bf16),
        scratch_shapes=[pltpu.VMEM((2, ATT_TK, ATT_TQ), _f32), pltpu.VMEM((2, ATT_TK, ATT_TQ), _f32),
                        pltpu.VMEM((2, 1, ATT_TQ), _f32), pltpu.VMEM((2, 1, ATT_TQ), _f32),
                        pltpu.VMEM((2, 1, ATT_TQ), _f32), pltpu.VMEM((2, V_ROWS, ATT_TQ), _f32)],
        compiler_params=_params("parallel", "parallel", "arbitrary"),
        name="fox_attention",
    )(q, k, vt)


def kernel(x, meta_tokens, norm_mix, norm_ffn, m_w_in, m_b_gate, m_g_out, m_w_out, kv_norm, kv_w, kv_b_f, k_norm,
           f_w_q, f_q_norm, f_w_o, moe_w_group, moe_b_group, moe_w_router, moe_b_router, moe_w_gate, moe_w_up,
           moe_w_down):
    B, S, D = x.shape
    depth = norm_mix.shape[0]
    n_a = m_w_in.shape[0]
    LP = REAL0 + S
    assert D == D_MODEL and S % ATT_TQ == 0 and LP % ROW_TILE == 0 and meta_tokens.shape[0] == N_META
    assert (B * LP) % MOE_ROW_TILE == 0 and (B * S) % MOE_ROW_TILE == 0
    assert depth > n_a and REAL0 % M_CHUNK == 0 and MLSTM_ROWS % M_CHUNK == 0
    assert n_a == 1
    head = jnp.concatenate([jnp.zeros((PAD_FRONT, D), x.dtype), meta_tokens.astype(x.dtype)], axis=0)
    k_sh = vt_sh = None
    rows_per_batch, row0, seq = LP, 0, LP
    for l in range(depth):
        if l < n_a:
            h, q, kt, v, o, u, bcol = _mlstm_in_proj(x, head, norm_mix[l], m_w_in[l], m_b_gate[l], LP)
            y = _mlstm_core(q, kt, v, o, u.T, bcol, m_g_out[l].reshape(-1), B, LP)
            w_out = m_w_out[l]
        else:
            j = l - n_a
            assert j == 0
            row0, seq = REAL0, S
            k_sh, vt_sh, q = _shared_kv_and_q(h, kv_norm, kv_w, kv_b_f, k_norm, norm_mix[l], f_w_q[j], f_q_norm[j],
                                              B, LP, row0)
            y = _attention(q, k_sh, vt_sh, B, S, LP).reshape(B * S, D)
            w_out = f_w_o[j]
        h = _mix_out_moe(y, w_out, h, rows_per_batch, row0, seq, norm_ffn[l], moe_w_group[l], moe_b_group[l],
                         moe_w_router[l], moe_b_router[l], moe_w_gate[l], moe_w_up[l], moe_w_down[l])
        rows_per_batch, row0 = seq, 0
    return h.reshape(B, S, D)
```
